```python
import jax, jax.numpy as jnp
from jax import lax
import numpy as np

D_MODEL = 1024
BATCH = 16
SEQ = 2048
DEPTH = 4

P_DIM = 256
N_EVEN = (DEPTH + 1) // 2
N_ODD = DEPTH // 2
EPS = 1e-6

A_HEADS = 8
A_HEAD_DIM = 128
A_WIDTH = A_HEADS * A_HEAD_DIM
A_CHUNK = 64
B_HEADS = 16
B_HEAD_DIM = 64
B_WIDTH = B_HEADS * B_HEAD_DIM
B_RANK_W = 64
B_RANK_A = 64
B_LN_EPS = 64e-5
RW_SLAB = 3 * B_WIDTH + B_RANK_W + B_RANK_A
EV_WIDTH = A_WIDTH + B_WIDTH
EV_IN = 3 * A_WIDTH + RW_SLAB + EV_WIDTH
C_HEADS = 8
C_HEAD_DIM = 64
C_WIDTH = C_HEADS * C_HEAD_DIM
C_BLOCK = 256
C_TOPK = 3
C_QCHUNK = 16
D_BLOCKS = 8
D_BLOCK_DIM = 128
D_WIDTH = D_BLOCKS * D_BLOCK_DIM
D_CONV = 4
LRU_C = 8.0
OD_WIDTH = C_WIDTH + D_WIDTH
OD_IN = 3 * C_WIDTH + D_WIDTH + OD_WIDTH

kernel_name = 'hybrid_hgrn2_rwkv7_moba_rglru_trunk'


def _rmsnorm(x, g):
    xf = x.astype(jnp.float32)
    y = xf * lax.rsqrt(jnp.mean(jnp.square(xf), axis=-1, keepdims=True) + EPS)
    return (y * g.astype(jnp.float32)).astype(x.dtype)


def _shift(z):
    return jnp.pad(z, ((0, 0), (1, 0), (0, 0)))[:, :-1]


def _hgrn2(q, f_logit, i, lb):
    bsz, seq, _ = q.shape
    log_f = jnp.logaddexp(jnp.log(lb), jnp.log1p(-lb) + jax.nn.log_sigmoid(f_logit))
    k = -jnp.expm1(log_f)
    q = jax.nn.silu(q)
    nc = seq // A_CHUNK

    def chunks(z):
        return z.reshape(bsz, nc, A_CHUNK, A_HEADS, A_HEAD_DIM).transpose(1, 0, 3, 2, 4)

    causal = jnp.tril(jnp.ones((A_CHUNK, A_CHUNK), dtype=bool))

    def step(state, inp):
        qc, gc, kc, vc = inp
        G = jnp.cumsum(gc, axis=2)
        diff = jnp.where(causal[:, :, None], G[:, :, :, None, :] - G[:, :, None, :, :], -jnp.inf)
        attn = jnp.einsum('bhtk,bhsk,bhtsk->bhts', qc, kc, jnp.exp(diff))
        o = jnp.einsum('bhts,bhsv->bhtv', attn, vc) + jnp.einsum('bhtk,bhkv->bhtv', qc * jnp.exp(G), state)
        g_last = G[:, :, -1:, :]
        state = jnp.exp(g_last[:, :, 0, :, None]) * state + jnp.einsum('bhsk,bhsv->bhkv', kc * jnp.exp(g_last - G), vc)
        return state, o

    s0 = jnp.zeros((bsz, A_HEADS, A_HEAD_DIM, A_HEAD_DIM), q.dtype)
    _, o = lax.scan(step, s0, (chunks(q), chunks(log_f), chunks(k), chunks(i)))
    return o.transpose(1, 0, 3, 2, 4).reshape(bsz, seq, A_HEADS, A_HEAD_DIM)


def _rwkv7(r, k, v, w_lr, a_lr, w0, w2, a0, a2, k_k, k_a, r_k, ln_g, ln_b):
    bsz, seq, _ = r.shape
    hs = (B_HEADS, B_HEAD_DIM)
    heads = lambda z: z.reshape(bsz, seq, B_HEADS, B_HEAD_DIM)
    w = -jax.nn.softplus(-(w0 + jnp.tanh(w_lr) @ w2)) - 0.5
    decay = jnp.exp(-jnp.exp(w))
    a = jax.nn.sigmoid(a0 + a_lr @ a2)
    r, k, v, a, decay = heads(r), heads(k), heads(v), heads(a), heads(decay)
    kk = k * k_k.reshape(hs)
    kk = kk / jnp.maximum(jnp.linalg.norm(kk, axis=-1, keepdims=True), 1e-12)
    k = k * (1.0 + (a - 1.0) * k_a.reshape(hs))

    def step(state, inp):
        rt, wt, kt, vt, kkt, at = inp
        sa = jnp.einsum('bhvk,bhk->bhv', state, -kkt)
        state = state * wt[:, :, None, :] + sa[..., None] * (kkt * at)[:, :, None, :] + vt[..., None] * kt[:, :, None, :]
        return state, jnp.einsum('bhvk,bhk->bhv', state, rt)

    tm = lambda z: jnp.moveaxis(z, 1, 0)
    s0 = jnp.zeros((bsz, B_HEADS, B_HEAD_DIM, B_HEAD_DIM), r.dtype)
    _, y = lax.scan(step, s0, (tm(r), tm(decay), tm(k), tm(v), tm(kk), tm(a)))
    y = jnp.moveaxis(y, 0, 1)
    mu = jnp.mean(y, axis=-1, keepdims=True)
    var = jnp.mean(jnp.square(y - mu), axis=-1, keepdims=True)
    y = (y - mu) * lax.rsqrt(var + B_LN_EPS) * ln_g.reshape(hs) + ln_b.reshape(hs)
    y = y + jnp.sum(r * k * r_k.reshape(hs), axis=-1, keepdims=True) * v
    return y.reshape(bsz, seq, B_WIDTH)


def _moba(q, k, v):
    bsz, seq, _ = q.shape
    nb = -(-seq // C_BLOCK)
    sp = nb * C_BLOCK
    nq = sp // C_QCHUNK
    topk = min(C_TOPK, nb)
    scale = C_HEAD_DIM ** -0.5

    def heads(z):
        z = jnp.pad(z, ((0, 0), (0, sp - seq), (0, 0)))
        return z.reshape(bsz, sp, C_HEADS, C_HEAD_DIM).transpose(0, 2, 1, 3)

    qh, kh, vh = heads(q), heads(k), heads(v)
    kb = kh.reshape(bsz, C_HEADS, nb, C_BLOCK, C_HEAD_DIM)
    vb = vh.reshape(bsz, C_HEADS, nb, C_BLOCK, C_HEAD_DIM)
    qblk = jnp.arange(sp) // C_BLOCK
    gate = jnp.einsum('bhtd,bhnd->bhtn', qh, jnp.mean(kb, axis=3)).astype(jnp.float32)
    gate = jnp.where(jnp.arange(nb)[None, :] < qblk[:, None], gate, -jnp.inf)
    _, idx = lax.top_k(gate, topk)
    valid = idx < qblk[:, None]

    to_chunks = lambda z: jnp.moveaxis(z.reshape(bsz, C_HEADS, nq, C_QCHUNK, z.shape[-1]), 2, 0)
    bi = jnp.arange(bsz)[:, None, None, None]
    hi = jnp.arange(C_HEADS)[None, :, None, None]
    key_off = jnp.arange(C_BLOCK)
    q_off = jnp.arange(C_QCHUNK)

    def attend(args):
        qc, ic, vc_, start = args
        k_sel = kb[bi, hi, ic]
        v_sel = vb[bi, hi, ic]
        blk = start // C_BLOCK
        k_own = lax.dynamic_index_in_dim(kb, blk, axis=2, keepdims=False)
        v_own = lax.dynamic_index_in_dim(vb, blk, axis=2, keepdims=False)
        s_sel = jnp.einsum('bhqd,bhqjkd->bhqjk', qc, k_sel).astype(jnp.float32) * scale
        s_sel = jnp.where(vc_[..., None], s_sel, -jnp.inf).reshape(bsz, C_HEADS, C_QCHUNK, topk * C_BLOCK)
        s_own = jnp.einsum('bhqd,bhkd->bhqk', qc, k_own).astype(jnp.float32) * scale
        causal = (blk * C_BLOCK + key_off)[None, :] <= (start + q_off)[:, None]
        s_own = jnp.where(causal, s_own, -jnp.inf)
        prob = jax.nn.softmax(jnp.concatenate([s_sel, s_own], axis=-1), axis=-1).astype(v.dtype)
        p_sel = prob[..., :topk * C_BLOCK].reshape(bsz, C_HEADS, C_QCHUNK, topk, C_BLOCK)
        p_own = prob[..., topk * C_BLOCK:]
        return jnp.einsum('bhqjk,bhqjkd->bhqd', p_sel, v_sel) + jnp.einsum('bhqk,bhkd->bhqd', p_own, v_own)

    starts = jnp.arange(nq, dtype=jnp.int32) * C_QCHUNK
    o = lax.map(attend, (to_chunks(qh), to_chunks(idx), to_chunks(valid), starts))
    o = o.transpose(1, 2, 0, 3, 4).reshape(bsz, C_HEADS, sp, C_HEAD_DIM)[:, :, :seq]
    return o.transpose(0, 2, 1, 3).reshape(bsz, seq, C_WIDTH)


def _rglru(xr, conv_w, conv_b, wa, ba, wx, bx, lam):
    bsz, seq, _ = xr.shape
    xf = xr.astype(jnp.float32)
    xc = lax.conv_general_dilated(xf, conv_w.astype(jnp.float32)[:, None, :], window_strides=(1,),
                                  padding=[(D_CONV - 1, 0)], dimension_numbers=('NWC', 'WIO', 'NWC'),
                                  feature_group_count=D_WIDTH) + conv_b
    blocks = xc.reshape(bsz, seq, D_BLOCKS, D_BLOCK_DIM)
    r = jax.nn.sigmoid(jnp.einsum('bsnd,nde->bsne', blocks, wa).reshape(bsz, seq, D_WIDTH) + ba)
    i = jax.nn.sigmoid(jnp.einsum('bsnd,nde->bsne', blocks, wx).reshape(bsz, seq, D_WIDTH) + bx)
    log_a = -LRU_C * r * jax.nn.softplus(-lam)
    a = jnp.exp(log_a)
    b = jnp.sqrt(-jnp.expm1(2.0 * log_a)) * (i * xc)

    def combine(lhs, rhs):
        return lhs[0] * rhs[0], rhs[0] * lhs[1] + rhs[1]

    _, h = lax.associative_scan(combine, (a, b), axis=1)
    return h


def _even_layer(x, norm_g, w_in, w_out, lb, hg_norm, mu, w0, w2, a0, a2, k_k, k_a, r_k, ln_g, ln_b):
    bsz, seq, _ = x.shape
    h = _rmsnorm(x, norm_g)
    z = (h @ w_in).astype(jnp.float32)
    za, zb, gate = jnp.split(z, [3 * A_WIDTH, 3 * A_WIDTH + RW_SLAB], axis=-1)
    qa, fa, ia = jnp.split(za, 3, axis=-1)
    zb = zb + (_shift(zb) - zb) * mu
    rb, kb, vb, wlr, alr = jnp.split(zb, [B_WIDTH, 2 * B_WIDTH, 3 * B_WIDTH, 3 * B_WIDTH + B_RANK_W], axis=-1)
    oa = _rmsnorm(_hgrn2(qa, fa, ia, lb), hg_norm.reshape(A_HEADS, A_HEAD_DIM)).reshape(bsz, seq, A_WIDTH)
    ob = _rwkv7(rb, kb, vb, wlr, alr, w0, w2, a0, a2, k_k, k_a, r_k, ln_g, ln_b)
    y = jnp.concatenate([oa, ob], axis=-1) * jax.nn.silu(gate)
    return y.astype(x.dtype) @ w_out


def _odd_layer(x, norm_g, w_in, w_out, conv_w, conv_b, wa, ba, wx, bx, lam):
    h = _rmsnorm(x, norm_g)
    z = h @ w_in
    qc, kc, vc, xd, gate = jnp.split(z, [C_WIDTH, 2 * C_WIDTH, 3 * C_WIDTH, 3 * C_WIDTH + D_WIDTH], axis=-1)
    oc = _moba(qc, kc, vc).astype(jnp.float32)
    od = _rglru(xd, conv_w, conv_b, wa, ba, wx, bx, lam)
    y = jnp.concatenate([oc, od], axis=-1) * jax.nn.silu(gate.astype(jnp.float32))
    return y.astype(x.dtype) @ w_out


def setup_inputs(seed: int = 0) -> dict:
    key = jax.random.key(seed)
    ks = iter(jax.random.split(key, 40))
    nrm = lambda shape, scale: scale * jax.random.normal(next(ks), shape, jnp.float32)
    gain = lambda shape: 1.0 + 0.1 * jax.random.normal(next(ks), shape, jnp.float32)
    uni = lambda shape, lo, hi: jax.random.uniform(next(ks), shape, jnp.float32, lo, hi)
    a8 = uni((N_ODD, D_WIDTH), 0.9, 0.999)
    sig = a8 ** (1.0 / LRU_C)
    lam = jnp.log(sig) - jnp.log1p(-sig)
    return {
        'x': nrm((BATCH, SEQ, D_MODEL), 1.0),
        'p': nrm((DEPTH, BATCH, SEQ, P_DIM), 1.0),
        'ev_norm': gain((N_EVEN, D_MODEL)),
        'ev_w_in': nrm((N_EVEN, D_MODEL, EV_IN), D_MODEL ** -0.5),
        'ev_w_out': nrm((N_EVEN, EV_WIDTH, D_MODEL), EV_WIDTH ** -0.5),
        'hg_lb_logits': 1.0 + nrm((N_EVEN, A_WIDTH), 0.5),
        'hg_norm': gain((N_EVEN, A_WIDTH)),
        'rw_mu': uni((N_EVEN, RW_SLAB), 0.0, 1.0),
        'rw_w0': uni((N_EVEN, B_WIDTH), -6.5, -1.5),
        'rw_w2': nrm((N_EVEN, B_RANK_W, B_WIDTH), 0.5 * B_RANK_W ** -0.5),
        'rw_a0': nrm((N_EVEN, B_WIDTH), 0.1),
        'rw_a2': nrm((N_EVEN, B_RANK_A, B_WIDTH), B_RANK_A ** -0.5),
        'rw_k_k': 0.85 + nrm((N_EVEN, B_WIDTH), 0.05),
        'rw_k_a': 1.0 + nrm((N_EVEN, B_WIDTH), 0.05),
        'rw_r_k': nrm((N_EVEN, B_WIDTH), 0.1),
        'rw_ln_g': gain((N_EVEN, B_WIDTH)),
        'rw_ln_b': nrm((N_EVEN, B_WIDTH), 0.01),
        'od_norm': gain((N_ODD, D_MODEL)),
        'od_w_in': nrm((N_ODD, D_MODEL, OD_IN), D_MODEL ** -0.5),
        'od_w_out': nrm((N_ODD, OD_WIDTH, D_MODEL), OD_WIDTH ** -0.5),
        'lru_conv_w': nrm((N_ODD, D_CONV, D_WIDTH), D_CONV ** -0.5),
        'lru_conv_b': nrm((N_ODD, D_WIDTH), 0.01),
        'lru_wa': nrm((N_ODD, D_BLOCKS, D_BLOCK_DIM, D_BLOCK_DIM), D_BLOCK_DIM ** -0.5),
        'lru_ba': nrm((N_ODD, D_WIDTH), 0.01),
        'lru_wx': nrm((N_ODD, D_BLOCKS, D_BLOCK_DIM, D_BLOCK_DIM), D_BLOCK_DIM ** -0.5),
        'lru_bx': nrm((N_ODD, D_WIDTH), 0.01),
        'lru_lambda': lam,
        'ple_proj': nrm((DEPTH, P_DIM, D_MODEL), P_DIM ** -0.5),
        'ple_gate': nrm((DEPTH, D_MODEL, D_MODEL), D_MODEL ** -0.5),
        'ple_norm': gain((DEPTH, D_MODEL)),
        'final_norm': gain((D_MODEL,)),
    }


def reference(x, p, ev_norm, ev_w_in, ev_w_out, hg_lb_logits, hg_norm, rw_mu, rw_w0, rw_w2, rw_a0, rw_a2,
              rw_k_k, rw_k_a, rw_r_k, rw_ln_g, rw_ln_b, od_norm, od_w_in, od_w_out, lru_conv_w, lru_conv_b,
              lru_wa, lru_ba, lru_wx, lru_bx, lru_lambda, ple_proj, ple_gate, ple_norm, final_norm):
    s = jax.nn.softmax(hg_lb_logits.astype(jnp.float32), axis=0)
    lower_bounds = jnp.maximum(jnp.cumsum(s, axis=0) - s[0], 0.0)
    for i in range(DEPTH):
        j = i // 2
        if i % 2 == 0:
            x = x + _even_layer(x, ev_norm[j], ev_w_in[j], ev_w_out[j], lower_bounds[j], hg_norm[j], rw_mu[j],
                                rw_w0[j], rw_w2[j], rw_a0[j], rw_a2[j], rw_k_k[j], rw_k_a[j], rw_r_k[j],
                                rw_ln_g[j], rw_ln_b[j])
        else:
            x = x + _odd_layer(x, od_norm[j], od_w_in[j], od_w_out[j], lru_conv_w[j], lru_conv_b[j],
                               lru_wa[j], lru_ba[j], lru_wx[j], lru_bx[j], lru_lambda[j])
        x = x + (p[i] @ ple_proj[i]) * jax.nn.sigmoid(_rmsnorm(x, ple_norm[i]) @ ple_gate[i])
    return _rmsnorm(x, final_norm)
```

```python
import functools
import math

import jax
import jax.numpy as jnp
from jax import lax
from jax.experimental import pallas as pl
from jax.experimental.pallas import tpu as pltpu

F32 = jnp.float32
BF16 = jnp.bfloat16

EPS = 1e-6
LANES = 128
SUBLANES = 8
VMEM_LIMIT = 48 * 1024 * 1024

A_HEADS, A_HEAD_DIM = 8, 128
B_HEADS, B_HEAD_DIM = 16, 64
B_RANK = 64
B_LN_EPS = 64e-5
C_HEADS, C_HEAD_DIM, C_BLOCK, C_TOPK = 8, 64, 256, 3
D_BLOCKS, D_BLOCK_DIM, D_CONV = 8, 128, 4
LRU_C = 8.0
P_DIM = 256

CHUNK = 64
ROWS = 256
NEG_BIG = -16384.0


def _nt(a, b):
    return lax.dot_general(a, b, (((1,), (1,)), ((), ())), preferred_element_type=F32)


def _tn(a, b):
    return lax.dot_general(a, b, (((0,), (0,)), ((), ())), preferred_element_type=F32)


def _nn(a, b):
    return jnp.dot(a, b, preferred_element_type=F32)


def _split3(x):
    h = x.astype(BF16)
    r = x - h.astype(F32)
    m = r.astype(BF16)
    l = (r - m.astype(F32)).astype(BF16)
    return h, m, l


def _nn_exact_lhs(lhs_bf16, x):
    h, m, l = _split3(x)
    return _nn(lhs_bf16, h) + _nn(lhs_bf16, m) + _nn(lhs_bf16, l)


def _nn_exact_rhs(x, rhs_bf16):
    h, m, l = _split3(x)
    return _nn(h, rhs_bf16) + _nn(m, rhs_bf16) + _nn(l, rhs_bf16)


def _sigmoid(x):
    e = jnp.exp(-jnp.abs(x))
    r = 1.0 / (1.0 + e)
    return jnp.where(x >= 0, r, e * r)


def _params(sem):
    return pltpu.CompilerParams(dimension_semantics=sem, vmem_limit_bytes=VMEM_LIMIT)


def _inproj_kernel(x_ref, g_ref, w_ref, o_ref, h_ref):
    @pl.when(pl.program_id(1) == 0)
    def _():
        x = x_ref[...]
        ms = jnp.mean(x * x, axis=-1, keepdims=True)
        h_ref[...] = (x * lax.rsqrt(ms + EPS) * g_ref[...]).astype(BF16)

    o_ref[...] = _nn(h_ref[...], w_ref[...])


def _inproj(x, g, w, tm, tn):
    t, d = x.shape
    n = w.shape[1]
    return pl.pallas_call(
        _inproj_kernel,
        grid=(t // tm, n // tn),
        in_specs=[
            pl.BlockSpec((tm, d), lambda i, j: (i, 0)),
            pl.BlockSpec((1, d), lambda i, j: (0, 0)),
            pl.BlockSpec((d, tn), lambda i, j: (0, j)),
        ],
        out_specs=pl.BlockSpec((tm, tn), lambda i, j: (i, j)),
        out_shape=jax.ShapeDtypeStruct((t, n), F32),
        scratch_shapes=[pltpu.VMEM((tm, d), BF16)],
        compiler_params=_params(("parallel", "arbitrary")),
    )(x, g.reshape(1, d), w)


_LEVELS = (1, 2, 4, 8, 16, 32)


def _boundary_bcast(g, m):
    c, w = g.shape
    if m >= SUBLANES:
        gr = g.reshape(c // (2 * m), 2 * m, w)
        b = jnp.broadcast_to(gr[:, m - 1:m, :], gr.shape)
        return b.reshape(c, w)
    gr = g.reshape(c // SUBLANES, SUBLANES, w)
    j = lax.broadcasted_iota(jnp.int32, gr.shape, 1)
    rows = [jnp.broadcast_to(gr[:, r:r + 1, :], gr.shape) for r in range(m - 1, SUBLANES, 2 * m)]
    b = rows[-1]
    for idx in range(len(rows) - 2, -1, -1):
        b = jnp.where(j < (idx + 1) * 2 * m, rows[idx], b)
    return b.reshape(c, w)


def _hgrn2_kernel(zq_ref, zf_ref, zi_ref, par_ref, o_ref, st_ref):
    c = CHUNK

    @pl.when(pl.program_id(2) == 0)
    def _():
        st_ref[...] = jnp.zeros_like(st_ref)

    row = lax.broadcasted_iota(jnp.int32, (c, c), 0)
    col = lax.broadcasted_iota(jnp.int32, (c, c), 1)
    ltri = (col <= row).astype(BF16)
    eye = col == row
    lvl_masks = []
    for m in _LEVELS:
        sh = int(math.log2(2 * m))
        same = (row >> sh) == (col >> sh)
        lvl_masks.append(same & ((row & (2 * m - 1)) >= m) & ((col & (2 * m - 1)) < m))
    rowi = lax.broadcasted_iota(jnp.int32, (c, LANES), 0)

    log_lb = par_ref[0:1, :]
    log_omlb = par_ref[1:2, :]
    omlb = par_ref[2:3, :]
    gain = par_ref[3:4, :]

    def chunk(ci, carry):
        sl = pl.ds(pl.multiple_of(ci * c, c), c)
        zq = zq_ref[sl, :]
        zf = zf_ref[sl, :]
        v = zi_ref[sl, :]

        e = jnp.exp(-jnp.abs(zf))
        r = 1.0 / (1.0 + e)
        sig_neg = jnp.where(zf >= 0, e * r, r)
        log_sig = jnp.minimum(zf, 0.0) - jnp.log(1.0 + e)
        bterm = log_omlb + log_sig
        mx = jnp.maximum(log_lb, bterm)
        log_f = mx + jnp.log(1.0 + jnp.exp(-jnp.abs(log_lb - bterm)))
        k = omlb * sig_neg
        q = zq * _sigmoid(zq)

        g = _nn_exact_lhs(ltri, log_f)
        g_last = g[c - 1:c, :]

        qb = q.astype(BF16)
        kb = k.astype(BF16)
        vb = v.astype(BF16)
        attn = jnp.where(eye, _nt(qb, kb), 0.0)
        for m, msk in zip(_LEVELS, lvl_masks):
            second = (rowi & (2 * m - 1)) >= m
            if m == 1:
                fac = jnp.where(second, jnp.exp(log_f), 1.0)
            else:
                bnd = _boundary_bcast(g, m)
                fac = jnp.exp(jnp.where(second, g - bnd, bnd - g))
            x = (jnp.where(second, q, k) * fac).astype(BF16)
            attn = attn + jnp.where(msk, _nt(x, x), 0.0)

        st = st_ref[...]
        o = _nn(attn.astype(BF16), vb) + _nt((q * jnp.exp(g)).astype(BF16), st.astype(BF16))
        khat = (k * jnp.exp(g_last - g)).astype(BF16)
        st_ref[...] = st * jnp.exp(g_last) + _tn(vb, khat)

        ms = jnp.mean(o * o, axis=-1, keepdims=True)
        o_ref[sl, :] = o * lax.rsqrt(ms + EPS) * gain
        return carry

    lax.fori_loop(0, ROWS // c, chunk, 0)


def _hgrn2(z, par, bsz, seq):
    t = z.shape[0]
    nr = seq // ROWS
    blk = lambda off: pl.BlockSpec((ROWS, LANES), lambda b, h, r: (b * nr + r, off + h))
    return pl.pallas_call(
        _hgrn2_kernel,
        grid=(bsz, A_HEADS, nr),
        in_specs=[blk(0), blk(A_HEADS), blk(2 * A_HEADS),
                  pl.BlockSpec((SUBLANES, LANES), lambda b, h, r: (0, h))],
        out_specs=pl.BlockSpec((ROWS, LANES), lambda b, h, r: (b * nr + r, h)),
        out_shape=jax.ShapeDtypeStruct((t, A_HEADS * A_HEAD_DIM), F32),
        scratch_shapes=[pltpu.VMEM((A_HEAD_DIM, A_HEAD_DIM), F32)],
        compiler_params=_params(("parallel", "parallel", "arbitrary")),
    )(z, z, z, par)


_EXP_M05 = math.exp(-0.5)
(_P_MU_R, _P_MU_K, _P_MU_V, _P_W0, _P_A0, _P_KK, _P_KA, _P_RK, _P_LNG, _P_LNB) = range(10)
_RW_NPAR = 16


def _rwkv_kernel(zr_ref, zk_ref, zv_ref, zwa_ref, par_ref, muwa_ref, w2_ref, a2_ref, o_ref,
                 st_ref, ext_ref):
    c = CHUNK
    hd = B_HEAD_DIM

    @pl.when(pl.program_id(2) == 0)
    def _():
        st_ref[...] = jnp.zeros_like(st_ref)
        ext_ref[:, 0:SUBLANES, :] = jnp.zeros((4, SUBLANES, LANES), F32)

    par = par_ref[...]
    prow = lambda i: par[i:i + 1, :]

    for idx, (ref, mu) in enumerate(((zr_ref, prow(_P_MU_R)), (zk_ref, prow(_P_MU_K)),
                                     (zv_ref, prow(_P_MU_V)), (zwa_ref, muwa_ref[...]))):
        cur = ref[...]
        ext_ref[idx, SUBLANES:SUBLANES + ROWS, :] = cur
        prev = ext_ref[idx, SUBLANES - 1:SUBLANES - 1 + ROWS, :]
        tail = cur[ROWS - SUBLANES:ROWS, :]
        ext_ref[idx, SUBLANES:SUBLANES + ROWS, :] = cur + (prev - cur) * mu
        ext_ref[idx, 0:SUBLANES, :] = tail

    lane = lax.broadcasted_iota(jnp.int32, (c, LANES), 1)
    lo_l = lane < hd
    lane2 = lax.broadcasted_iota(jnp.int32, (2 * c, LANES), 1)
    row2 = lax.broadcasted_iota(jnp.int32, (2 * c, LANES), 0)
    s_idx = lane2 & (c - 1)
    t_idx = row2 & (c - 1)
    tri = (s_idx < t_idx) | ((row2 >= c) & (s_idx == t_idx))
    bd = (row2 >> int(math.log2(c))) == (lane2 >> int(math.log2(c)))
    strict_bd = bd & (s_idx < t_idx)
    eye2 = (row2 == lane2).astype(F32)
    ltri = (lax.broadcasted_iota(jnp.int32, (c, c), 1)
            <= lax.broadcasted_iota(jnp.int32, (c, c), 0)).astype(BF16)
    seg = ((lax.broadcasted_iota(jnp.int32, (LANES, LANES), 0) >> int(math.log2(hd)))
           == (lax.broadcasted_iota(jnp.int32, (LANES, LANES), 1) >> int(math.log2(hd))))
    seg_b = seg.astype(BF16)

    def segsum(x):
        h = x.astype(BF16)
        l = (x - h.astype(F32)).astype(BF16)
        return _nn(h, seg_b) + _nn(l, seg_b)

    w2 = w2_ref[0]
    a2 = a2_ref[0]

    def chunk(ci, carry):
        sl = pl.ds(pl.multiple_of(ci * c, c) + SUBLANES, c)
        r = ext_ref[0, sl, :]
        kraw = ext_ref[1, sl, :]
        v = ext_ref[2, sl, :]
        wa = ext_ref[3, sl, :]

        xw = prow(_P_W0) + _nn(jnp.tanh(wa).astype(BF16), w2)
        g = -_EXP_M05 * _sigmoid(xw)
        alpha = _sigmoid(prow(_P_A0) + _nn(wa.astype(BF16), a2))
        kk = kraw * prow(_P_KK)
        nrm = jnp.sqrt(segsum(kk * kk))
        kk = kk / jnp.maximum(nrm, 1e-12)
        k = kraw * (1.0 + (alpha - 1.0) * prow(_P_KA))
        bv = kk * alpha

        gc = _nn_exact_lhs(ltri, g)
        g_last = gc[c - 1:c, :]
        e_pos = jnp.exp(gc)
        e_neg = jnp.exp(-gc)
        e_last = jnp.exp(g_last)
        at = -kk * jnp.exp(gc - g)
        rt = r * e_pos
        bt = bv * e_neg
        kt = k * e_neg
        bh = bt * e_last
        kh = kt * e_last

        zero = jnp.zeros_like(at)
        lhs_lo = jnp.concatenate([jnp.where(lo_l, at, zero), jnp.where(lo_l, rt, zero)], axis=0).astype(BF16)
        lhs_hi = jnp.concatenate([jnp.where(lo_l, zero, at), jnp.where(lo_l, zero, rt)], axis=0).astype(BF16)
        btb = bt.astype(BF16)
        ktb = kt.astype(BF16)
        aa_lo = jnp.where(tri, _nt(lhs_lo, jnp.concatenate([btb, ktb], axis=0)), 0.0)
        aa_hi = jnp.where(tri, _nt(lhs_hi, jnp.concatenate([ktb, btb], axis=0)), 0.0)

        a_rows = jnp.concatenate([aa_lo[0:c], aa_hi[0:c]], axis=0)
        abd = jnp.where(bd, a_rows, 0.0)
        akx = jnp.where(bd, 0.0, a_rows).astype(BF16)

        t_inv = eye2 + abd
        pw = abd
        for _ in range(int(math.log2(c)) - 1):
            pwb = pw.astype(BF16)
            pw = _nn(pwb, pwb)
            t_inv = t_inv + _nn(pw.astype(BF16), t_inv.astype(BF16))

        st = st_ref[...]
        stb = st.astype(BF16)
        vb = v.astype(BF16)
        vv = jnp.concatenate([vb, vb], axis=0)
        w_bd = _nt(lhs_lo_hi_a(lhs_lo, lhs_hi, c), stb) + _nn(akx, vv)
        w_bd = jnp.where(bd, w_bd, 0.0)
        u_bd = _nn(t_inv.astype(BF16), w_bd.astype(BF16))
        u = u_bd[0:c] + u_bd[c:2 * c]
        ub = u.astype(BF16)

        y_lo = _nn(aa_lo[c:2 * c].astype(BF16), jnp.concatenate([ub, vb], axis=0))
        y_hi = _nn(aa_hi[c:2 * c].astype(BF16), jnp.concatenate([vb, ub], axis=0))
        y = _nt(rt.astype(BF16), stb) + jnp.where(lo_l, y_lo, y_hi)

        st_new = st * e_last + _tn(jnp.concatenate([ub, vb], axis=0),
                                   jnp.concatenate([bh.astype(BF16), kh.astype(BF16)], axis=0))
        st_ref[...] = jnp.where(seg, st_new, 0.0)

        inv_n = 1.0 / hd
        mu = segsum(y) * inv_n
        d = y - mu
        var = segsum(d * d) * inv_n
        yn = d * lax.rsqrt(var + B_LN_EPS) * prow(_P_LNG) + prow(_P_LNB)
        yn = yn + segsum(r * k * prow(_P_RK)) * v
        o_ref[pl.ds(pl.multiple_of(ci * c, c), c), :] = yn
        return carry

    lax.fori_loop(0, ROWS // c, chunk, 0)


def lhs_lo_hi_a(lhs_lo, lhs_hi, c):
    return jnp.concatenate([lhs_lo[0:c], lhs_hi[0:c]], axis=0)


def _rwkv(z, par, muwa, w2p, a2p, bsz, seq, col_r, col_k, col_v, col_wa):
    t = z.shape[0]
    nr = seq // ROWS
    npair = B_HEADS // 2
    blk = lambda off: pl.BlockSpec((ROWS, LANES), lambda b, p, r: (b * nr + r, off + p))
    return pl.pallas_call(
        _rwkv_kernel,
        grid=(bsz, npair, nr),
        in_specs=[blk(col_r), blk(col_k), blk(col_v),
                  pl.BlockSpec((ROWS, LANES), lambda b, p, r: (b * nr + r, col_wa)),
                  pl.BlockSpec((_RW_NPAR, LANES), lambda b, p, r: (0, p)),
                  pl.BlockSpec((1, LANES), lambda b, p, r: (0, 0)),
                  pl.BlockSpec((1, LANES, LANES), lambda b, p, r: (p, 0, 0)),
                  pl.BlockSpec((1, LANES, LANES), lambda b, p, r: (p, 0, 0))],
        out_specs=pl.BlockSpec((ROWS, LANES), lambda b, p, r: (b * nr + r, p)),
        out_shape=jax.ShapeDtypeStruct((t, B_HEADS * B_HEAD_DIM), F32),
        scratch_shapes=[pltpu.VMEM((LANES, LANES), F32),
                        pltpu.VMEM((4, SUBLANES + ROWS, LANES), F32)],
        compiler_params=_params(("parallel", "parallel", "arbitrary")),
    )(z, z, z, z, par, muwa, w2p, a2p)


def _moba_kernel(q_ref, k_ref, v_ref, o_ref, kmean_ref):
    nb = kmean_ref.shape[0]
    blk = C_BLOCK
    hd = C_HEAD_DIM
    n = pl.program_id(2)

    @pl.when(n == 0)
    def _():
        for j in range(nb):
            kmean_ref[j:j + 1, :] = jnp.mean(k_ref[j * blk:(j + 1) * blk, :], axis=0, keepdims=True)

    q = q_ref[...]
    lane = lax.broadcasted_iota(jnp.int32, (blk, LANES), 1)
    lo_l = lane < hd
    kmean = kmean_ref[...]
    lane_m = lax.broadcasted_iota(jnp.int32, (nb, LANES), 1)
    jidx = lax.broadcasted_iota(jnp.int32, (nb, blk), 0)
    valid = jidx < n
    place_row = lax.broadcasted_iota(jnp.int32, (nb, LANES), 0)

    def select_bias(is_lo):
        km = jnp.where((lane_m < hd) if is_lo else (lane_m >= hd), kmean, 0.0)
        gate = lax.dot_general(km, q, (((1,), (1,)), ((), ())), precision=lax.Precision.HIGHEST,
                               preferred_element_type=F32)
        gm = jnp.where(valid, gate, -jnp.inf)
        cnt = jnp.zeros((nb, blk), F32)
        for j2 in range(nb):
            gj = gm[j2:j2 + 1, :]
            better = (gj > gm) | ((gj == gm) & (j2 < jidx))
            cnt = cnt + jnp.where(better, 1.0, 0.0)
        sel = valid & (cnt < float(C_TOPK))
        bias = jnp.where(sel, 0.0, NEG_BIG).astype(BF16)
        off = hd if is_lo else 0
        place = (lane_m == place_row + off).astype(BF16)
        return _tn(bias, place)

    scale = hd ** -0.5
    qs = q * scale
    q_lo = jnp.where(lo_l, qs, select_bias(True)).astype(BF16)
    q_hi = jnp.where(lo_l, select_bias(False), qs).astype(BF16)

    rowq = lax.broadcasted_iota(jnp.int32, (blk, blk), 0)
    colk = lax.broadcasted_iota(jnp.int32, (blk, blk), 1)
    causal = colk <= rowq

    start = pl.multiple_of(n * blk, blk)
    k_own = k_ref[pl.ds(start, blk), :].astype(BF16)
    v_own = v_ref[pl.ds(start, blk), :].astype(BF16)
    zero_b = jnp.zeros_like(k_own)

    hi_l = lane >= hd
    own_l = lambda is_lo: lo_l if is_lo else hi_l

    def first(qh, is_lo):
        kh = jnp.where(own_l(is_lo), k_own, zero_b)
        s = jnp.where(causal, _nt(qh, kh), -jnp.inf)
        m = jnp.max(s, axis=-1, keepdims=True)
        p = jnp.exp(s - m)
        return m, jnp.sum(p, axis=-1, keepdims=True), _nn(p.astype(BF16), v_own)

    def step(j, carry):
        st = pl.multiple_of(j * blk, blk)
        kj = k_ref[pl.ds(st, blk), :].astype(BF16)
        vj = v_ref[pl.ds(st, blk), :].astype(BF16)
        out = []
        for (m, l, acc), qh, is_lo in ((carry[0], q_lo, True), (carry[1], q_hi, False)):
            off = hd if is_lo else 0
            onehot = (lane == off + j).astype(BF16)
            kh = jnp.where(own_l(is_lo), kj, onehot)
            s = _nt(qh, kh)
            m_new = jnp.maximum(m, jnp.max(s, axis=-1, keepdims=True))
            corr = jnp.exp(m - m_new)
            p = jnp.exp(s - m_new)
            out.append((m_new, l * corr + jnp.sum(p, axis=-1, keepdims=True),
                        acc * corr + _nn(p.astype(BF16), vj)))
        return tuple(out)

    res = lax.fori_loop(0, n, step, (first(q_lo, True), first(q_hi, False)))
    (_, l_lo, acc_lo), (_, l_hi, acc_hi) = res
    o_ref[...] = jnp.where(lo_l, acc_lo / l_lo, acc_hi / l_hi)


def _moba(z, bsz, seq, col_q, col_k, col_v):
    t = z.shape[0]
    nb = seq // C_BLOCK
    npair = C_HEADS // 2
    return pl.pallas_call(
        _moba_kernel,
        grid=(bsz, npair, nb),
        in_specs=[pl.BlockSpec((C_BLOCK, LANES), lambda b, p, n: (b * nb + n, col_q + p)),
                  pl.BlockSpec((seq, LANES), lambda b, p, n: (b, col_k + p)),
                  pl.BlockSpec((seq, LANES), lambda b, p, n: (b, col_v + p))],
        out_specs=pl.BlockSpec((C_BLOCK, LANES), lambda b, p, n: (b * nb + n, p)),
        out_shape=jax.ShapeDtypeStruct((t, C_HEADS * C_HEAD_DIM), F32),
        scratch_shapes=[pltpu.VMEM((nb, LANES), F32)],
        compiler_params=_params(("parallel", "parallel", "arbitrary")),
    )(z, z, z)


def _rglru_kernel(x_ref, cw_ref, vec_ref, wa_ref, wx_ref, o_ref, ext_ref, a_ref, b_ref, h_ref):
    rows = x_ref.shape[0]
    width = x_ref.shape[1]

    @pl.when(pl.program_id(1) == 0)
    def _():
        ext_ref[0:SUBLANES, :] = jnp.zeros((SUBLANES, width), F32)
        h_ref[...] = jnp.zeros_like(h_ref)

    x = x_ref[...]
    ext_ref[SUBLANES:SUBLANES + rows, :] = x
    xc = vec_ref[0:1, :] + x * cw_ref[D_CONV - 1:D_CONV, :]
    for d in range(1, D_CONV):
        xc = xc + ext_ref[SUBLANES - d:SUBLANES - d + rows, :] * cw_ref[D_CONV - 1 - d:D_CONV - d, :]
    ext_ref[0:SUBLANES, :] = x[rows - SUBLANES:rows, :]

    ba = vec_ref[1:2, :]
    bx = vec_ref[2:3, :]
    lam = vec_ref[3:4, :]
    sp = jnp.maximum(-lam, 0.0) + jnp.log(1.0 + jnp.exp(-jnp.abs(lam)))
    for nblk in range(D_BLOCKS):
        cs = slice(nblk * D_BLOCK_DIM, (nblk + 1) * D_BLOCK_DIM)
        xb = xc[:, cs]
        xbb = xb.astype(BF16)
        rg = _sigmoid(_nn(xbb, wa_ref[nblk]) + ba[:, cs])
        ig = _sigmoid(_nn(xbb, wx_ref[nblk]) + bx[:, cs])
        log_a = -LRU_C * rg * sp[:, cs]
        th = jnp.tanh(log_a)
        one_minus_a2 = -2.0 * th / (1.0 - th)
        a_ref[:, cs] = jnp.exp(log_a)
        b_ref[:, cs] = jnp.sqrt(one_minus_a2) * (ig * xb)

    rowi = lax.broadcasted_iota(jnp.int32, (SUBLANES, width), 0)

    def group(gi, carry):
        sl = pl.ds(pl.multiple_of(gi * SUBLANES, SUBLANES), SUBLANES)
        a = a_ref[sl, :]
        b = b_ref[sl, :]
        for d in (1, 2, 4):
            keep = rowi >= d
            a_sh = jnp.where(keep, pltpu.roll(a, d, 0), 1.0)
            b_sh = jnp.where(keep, pltpu.roll(b, d, 0), 0.0)
            b = a * b_sh + b
            a = a * a_sh
        h = a * carry + b
        o_ref[sl, :] = h
        return jnp.broadcast_to(h[SUBLANES - 1:SUBLANES, :], (SUBLANES, width))

    h_ref[...] = lax.fori_loop(0, rows // SUBLANES, group, h_ref[...])


def _rglru(z, conv_w, vecs, wa, wx, bsz, seq, col_x):
    t = z.shape[0]
    width = D_BLOCKS * D_BLOCK_DIM
    nr = seq // ROWS
    full = lambda shape: pl.BlockSpec(shape, lambda b, r: (0,) * len(shape))
    return pl.pallas_call(
        _rglru_kernel,
        grid=(bsz, nr),
        in_specs=[pl.BlockSpec((ROWS, width), lambda b, r: (b * nr + r, col_x)),
                  full((D_CONV, width)), full((SUBLANES, width)),
                  full((D_BLOCKS, D_BLOCK_DIM, D_BLOCK_DIM)), full((D_BLOCKS, D_BLOCK_DIM, D_BLOCK_DIM))],
        out_specs=pl.BlockSpec((ROWS, width), lambda b, r: (b * nr + r, 0)),
        out_shape=jax.ShapeDtypeStruct((t, width), F32),
        scratch_shapes=[pltpu.VMEM((SUBLANES + ROWS, width), F32),
                        pltpu.VMEM((ROWS, width), F32),
                        pltpu.VMEM((ROWS, width), F32),
                        pltpu.VMEM((SUBLANES, width), F32)],
        compiler_params=_params(("parallel", "arbitrary")),
    )(z, conv_w, vecs, wa, wx)


def _outproj_kernel(*refs, nbr, final):
    br = refs[:nbr]
    gt = refs[nbr:2 * nbr]
    ws = refs[2 * nbr:3 * nbr]
    x_ref, p_ref, pp_ref, pg_ref, pn_ref = refs[3 * nbr:3 * nbr + 5]
    rest = refs[3 * nbr + 5:]
    if final:
        fn_ref, o_ref = rest
    else:
        (o_ref,) = rest

    acc = x_ref[...]
    for b, g, w in zip(br, gt, ws):
        gv = g[...]
        y = b[...] * (gv * _sigmoid(gv))
        acc = acc + _nn(y.astype(BF16), w[...])
    ms = jnp.mean(acc * acc, axis=-1, keepdims=True)
    hn = (acc * lax.rsqrt(ms + EPS) * pn_ref[...]).astype(BF16)
    gate = _sigmoid(_nn(hn, pg_ref[...]))
    out = acc + _nn(p_ref[...].astype(BF16), pp_ref[...]) * gate
    if final:
        ms2 = jnp.mean(out * out, axis=-1, keepdims=True)
        out = out * lax.rsqrt(ms2 + EPS) * fn_ref[...]
    o_ref[...] = out


def _outproj(branches, z, gate_cols, w_parts, x, p_i, ple_proj, ple_gate, ple_norm, final_norm, tm):
    t, d = x.shape
    nbr = len(branches)
    in_specs, args = [], []
    for b in branches:
        in_specs.append(pl.BlockSpec((tm, b.shape[1]), lambda i: (i, 0)))
        args.append(b)
    for b, gc in zip(branches, gate_cols):
        in_specs.append(pl.BlockSpec((tm, b.shape[1]), lambda i, gc=gc: (i, gc)))
        args.append(z)
    for w in w_parts:
        in_specs.append(pl.BlockSpec(w.shape, lambda i: (0, 0)))
        args.append(w)
    in_specs += [pl.BlockSpec((tm, d), lambda i: (i, 0)),
                 pl.BlockSpec((tm, P_DIM), lambda i: (i, 0)),
                 pl.BlockSpec((P_DIM, d), lambda i: (0, 0)),
                 pl.BlockSpec((d, d), lambda i: (0, 0)),
                 pl.BlockSpec((1, d), lambda i: (0, 0))]
    args += [x, p_i, ple_proj, ple_gate, ple_norm.reshape(1, d)]
    final = final_norm is not None
    if final:
        in_specs.append(pl.BlockSpec((1, d), lambda i: (0, 0)))
        args.append(final_norm.reshape(1, d))
    return pl.pallas_call(
        functools.partial(_outproj_kernel, nbr=nbr, final=final),
        grid=(t // tm,),
        in_specs=in_specs,
        out_specs=pl.BlockSpec((tm, d), lambda i: (i, 0)),
        out_shape=jax.ShapeDtypeStruct((t, d), F32),
        compiler_params=_params(("parallel",)),
    )(*args)


def _even_layer(x, bsz, seq, norm_g, w_in, w_out, lb, hg_norm, mu, w0, w2, a0, a2, k_k, k_a, r_k, ln_g, ln_b,
                p_i, ple_proj, ple_gate, ple_norm, final_norm):
    aw = A_HEADS * A_HEAD_DIM
    bw = B_HEADS * B_HEAD_DIM
    rw0 = 3 * aw
    lr0 = rw0 + 3 * bw
    g0 = lr0 + 2 * B_RANK
    w_perm = jnp.concatenate([w_in[:, :lr0], w_in[:, g0:], w_in[:, lr0:g0]], axis=1).astype(BF16)
    z = _inproj(x, norm_g, w_perm, tm=1024, tn=w_perm.shape[1] // 5)

    zero = jnp.zeros_like(lb)
    hpar = jnp.stack([jnp.log(lb), jnp.log1p(-lb), 1.0 - lb, hg_norm, zero, zero, zero, zero])
    oa = _hgrn2(z, hpar, bsz, seq)

    zrow = jnp.zeros((bw,), F32)
    rows = [mu[0:bw], mu[bw:2 * bw], mu[2 * bw:3 * bw], w0, a0, k_k, k_a, r_k, ln_g, ln_b]
    rpar = jnp.stack(rows + [zrow] * (_RW_NPAR - len(rows)))
    muwa = mu[3 * bw:].reshape(1, 2 * B_RANK)
    npair = B_HEADS // 2
    zpad = jnp.zeros((npair, B_RANK, LANES), F32)
    w2p = jnp.concatenate([w2.reshape(B_RANK, npair, LANES).transpose(1, 0, 2), zpad], axis=1).astype(BF16)
    a2p = jnp.concatenate([zpad, a2.reshape(B_RANK, npair, LANES).transpose(1, 0, 2)], axis=1).astype(BF16)
    cb = lambda c: c // LANES
    ob = _rwkv(z, rpar, muwa, w2p, a2p, bsz, seq, cb(rw0), cb(rw0 + bw), cb(rw0 + 2 * bw), cb(lr0 + aw + bw))

    gate0 = lr0
    w_out_b = w_out.astype(BF16)
    return _outproj([oa, ob], z, [gate0 // aw, (gate0 + aw) // bw], [w_out_b[:aw], w_out_b[aw:]],
                    x, p_i, ple_proj.astype(BF16), ple_gate.astype(BF16), ple_norm, final_norm, tm=256)


def _odd_layer(x, bsz, seq, norm_g, w_in, w_out, conv_w, conv_b, wa, ba, wx, bx, lam,
               p_i, ple_proj, ple_gate, ple_norm, final_norm):
    cw = C_HEADS * C_HEAD_DIM
    dw = D_BLOCKS * D_BLOCK_DIM
    x0 = 3 * cw
    g0 = x0 + dw
    w_perm = jnp.concatenate([w_in[:, :x0], w_in[:, g0:g0 + cw], w_in[:, x0:g0], w_in[:, g0 + cw:]],
                             axis=1).astype(BF16)
    z = _inproj(x, norm_g, w_perm, tm=1024, tn=w_perm.shape[1] // 4)

    cb = lambda c: c // LANES
    oc = _moba(z, bsz, seq, cb(0), cb(cw), cb(2 * cw))
    zero = jnp.zeros_like(lam)
    vecs = jnp.stack([conv_b, ba, bx, lam, zero, zero, zero, zero])
    od = _rglru(z, conv_w, vecs, wa.astype(BF16), wx.astype(BF16), bsz, seq, (x0 + cw) // dw)

    w_out_b = w_out.astype(BF16)
    return _outproj([oc, od], z, [x0 // cw, (x0 + cw + dw) // dw], [w_out_b[:cw], w_out_b[cw:]],
                    x, p_i, ple_proj.astype(BF16), ple_gate.astype(BF16), ple_norm, final_norm, tm=256)


def kernel(x, p, ev_norm, ev_w_in, ev_w_out, hg_lb_logits, hg_norm, rw_mu, rw_w0, rw_w2, rw_a0, rw_a2, rw_k_k, rw_k_a, rw_r_k, rw_ln_g, rw_ln_b, od_norm, od_w_in, od_w_out, lru_conv_w, lru_conv_b, lru_wa, lru_ba, lru_wx, lru_bx, lru_lambda, ple_proj, ple_gate, ple_norm, final_norm):
    bsz, seq, d = x.shape
    depth = p.shape[0]
    s = jax.nn.softmax(hg_lb_logits.astype(F32), axis=0)
    lower_bounds = jnp.maximum(jnp.cumsum(s, axis=0) - s[0], 0.0)
    xf = x.reshape(bsz * seq, d)
    pf = p.reshape(depth, bsz * seq, p.shape[-1])
    for i in range(depth):
        j = i // 2
        fin = final_norm if i == depth - 1 else None
        if i % 2 == 0:
            xf = _even_layer(xf, bsz, seq, ev_norm[j], ev_w_in[j], ev_w_out[j], lower_bounds[j], hg_norm[j],
                             rw_mu[j], rw_w0[j], rw_w2[j], rw_a0[j], rw_a2[j], rw_k_k[j], rw_k_a[j], rw_r_k[j],
                             rw_ln_g[j], rw_ln_b[j], pf[i], ple_proj[i], ple_gate[i], ple_norm[i], fin)
        else:
            xf = _odd_layer(xf, bsz, seq, od_norm[j], od_w_in[j], od_w_out[j], lru_conv_w[j], lru_conv_b[j],
                            lru_wa[j], lru_ba[j], lru_wx[j], lru_bx[j], lru_lambda[j],
                            pf[i], ple_proj[i], ple_gate[i], ple_norm[i], fin)
    return xf.reshape(bsz, seq, d)
```

```python
import functools
import math

import jax
import jax.numpy as jnp
from jax import lax
from jax.experimental import pallas as pl
from jax.experimental.pallas import tpu as pltpu

F32 = jnp.float32
BF16 = jnp.bfloat16

EPS = 1e-6
LANES = 128
SUBLANES = 8
VMEM_LIMIT = 48 * 1024 * 1024

A_HEADS, A_HEAD_DIM = 8, 128
B_HEADS, B_HEAD_DIM = 16, 64
B_RANK = 64
B_LN_EPS = 64e-5
C_HEADS, C_HEAD_DIM, C_BLOCK, C_TOPK = 8, 64, 256, 3
D_BLOCKS, D_BLOCK_DIM, D_CONV = 8, 128, 4
LRU_C = 8.0
P_DIM = 256

CHUNK = 64
ROWS = 256
NEG_BIG = -16384.0


def _nt(a, b):
    return lax.dot_general(a, b, (((1,), (1,)), ((), ())), preferred_element_type=F32)


def _tn(a, b):
    return lax.dot_general(a, b, (((0,), (0,)), ((), ())), preferred_element_type=F32)


def _nn(a, b):
    return jnp.dot(a, b, preferred_element_type=F32)


def _split3(x):
    h = x.astype(BF16)
    r = x - h.astype(F32)
    m = r.astype(BF16)
    l = (r - m.astype(F32)).astype(BF16)
    return h, m, l


def _nn_exact_lhs(lhs_bf16, x):
    h, m, l = _split3(x)
    return _nn(lhs_bf16, h) + _nn(lhs_bf16, m) + _nn(lhs_bf16, l)


def _nn_exact_rhs(x, rhs_bf16):
    h, m, l = _split3(x)
    return _nn(h, rhs_bf16) + _nn(m, rhs_bf16) + _nn(l, rhs_bf16)


def _sigmoid(x):
    e = jnp.exp(-jnp.abs(x))
    r = 1.0 / (1.0 + e)
    return jnp.where(x >= 0, r, e * r)


def _round_robin(gens):
    results = [None] * len(gens)
    live = list(range(len(gens)))
    while live:
        still = []
        for i in live:
            try:
                next(gens[i])
                still.append(i)
            except StopIteration as stop:
                results[i] = stop.value
        live = still
    return results


def _params(sem):
    return pltpu.CompilerParams(dimension_semantics=sem, vmem_limit_bytes=VMEM_LIMIT)


def _inproj_kernel(x_ref, g_ref, w_ref, o_ref, h_ref):
    @pl.when(pl.program_id(1) == 0)
    def _():
        x = x_ref[...]
        ms = jnp.mean(x * x, axis=-1, keepdims=True)
        h_ref[...] = (x * lax.rsqrt(ms + EPS) * g_ref[...]).astype(BF16)

    o_ref[...] = _nn(h_ref[...], w_ref[...])


def _inproj(x, g, w, tm, tn):
    t, d = x.shape
    n = w.shape[1]
    return pl.pallas_call(
        _inproj_kernel,
        grid=(t // tm, n // tn),
        in_specs=[
            pl.BlockSpec((tm, d), lambda i, j: (i, 0)),
            pl.BlockSpec((1, d), lambda i, j: (0, 0)),
            pl.BlockSpec((d, tn), lambda i, j: (0, j)),
        ],
        out_specs=pl.BlockSpec((tm, tn), lambda i, j: (i, j)),
        out_shape=jax.ShapeDtypeStruct((t, n), F32),
        scratch_shapes=[pltpu.VMEM((tm, d), BF16)],
        compiler_params=_params(("parallel", "arbitrary")),
        name="inproj",
    )(x, g.reshape(1, d), w)


_LEVELS = (1, 2, 4, 8, 16, 32)


def _boundary_bcast(g, m):
    c, w = g.shape
    if m >= SUBLANES:
        gr = g.reshape(c // (2 * m), 2 * m, w)
        b = jnp.broadcast_to(gr[:, m - 1:m, :], gr.shape)
        return b.reshape(c, w)
    gr = g.reshape(c // SUBLANES, SUBLANES, w)
    j = lax.broadcasted_iota(jnp.int32, gr.shape, 1)
    rows = [jnp.broadcast_to(gr[:, r:r + 1, :], gr.shape) for r in range(m - 1, SUBLANES, 2 * m)]
    b = rows[-1]
    for idx in range(len(rows) - 2, -1, -1):
        b = jnp.where(j < (idx + 1) * 2 * m, rows[idx], b)
    return b.reshape(c, w)


def _hgrn2_kernel(zq_ref, zf_ref, zi_ref, par_ref, o_ref, st_ref):
    c = CHUNK

    @pl.when(pl.program_id(2) == 0)
    def _():
        st_ref[...] = jnp.zeros_like(st_ref)

    row = lax.broadcasted_iota(jnp.int32, (c, c), 0)
    col = lax.broadcasted_iota(jnp.int32, (c, c), 1)
    ltri = (col <= row).astype(BF16)
    eye = col == row
    lvl_masks = []
    for m in _LEVELS:
        sh = int(math.log2(2 * m))
        same = (row >> sh) == (col >> sh)
        lvl_masks.append(same & ((row & (2 * m - 1)) >= m) & ((col & (2 * m - 1)) < m))
    rowi = lax.broadcasted_iota(jnp.int32, (c, LANES), 0)

    nheads = st_ref.shape[0]
    par_all = par_ref[...]

    def head_chunk(hi, zq, zf, v, st):
        cs = slice(hi * LANES, (hi + 1) * LANES)
        log_lb = par_all[0:1, cs]
        log_omlb = par_all[1:2, cs]
        omlb = par_all[2:3, cs]
        gain = par_all[3:4, cs]

        e = jnp.exp(-jnp.abs(zf))
        r = 1.0 / (1.0 + e)
        sig_neg = jnp.where(zf >= 0, e * r, r)
        log_sig = jnp.minimum(zf, 0.0) - jnp.log(1.0 + e)
        bterm = log_omlb + log_sig
        mx = jnp.maximum(log_lb, bterm)
        log_f = mx + jnp.log(1.0 + jnp.exp(-jnp.abs(log_lb - bterm)))
        k = omlb * sig_neg
        q = zq * _sigmoid(zq)

        g = _nn_exact_lhs(ltri, log_f)
        yield
        g_last = g[c - 1:c, :]

        qb = q.astype(BF16)
        kb = k.astype(BF16)
        vb = v.astype(BF16)
        attn = jnp.where(eye, _nt(qb, kb), 0.0)
        for m, msk in zip(_LEVELS, lvl_masks):
            second = (rowi & (2 * m - 1)) >= m
            if m == 1:
                fac = jnp.where(second, jnp.exp(log_f), 1.0)
            else:
                bnd = _boundary_bcast(g, m)
                fac = jnp.exp(jnp.where(second, g - bnd, bnd - g))
            x = (jnp.where(second, q, k) * fac).astype(BF16)
            attn = attn + jnp.where(msk, _nt(x, x), 0.0)
        yield

        o = _nn(attn.astype(BF16), vb) + _nt((q * jnp.exp(g)).astype(BF16), st.astype(BF16))
        khat = (k * jnp.exp(g_last - g)).astype(BF16)
        st_new = st * jnp.exp(g_last) + _tn(vb, khat)
        yield

        ms = jnp.mean(o * o, axis=-1, keepdims=True)
        return o * lax.rsqrt(ms + EPS) * gain, st_new

    def chunk(ci, carry):
        sl = pl.ds(pl.multiple_of(ci * c, c), c)
        gens = []
        for hi in range(nheads):
            cs = slice(hi * LANES, (hi + 1) * LANES)
            gens.append(head_chunk(hi, zq_ref[sl, cs], zf_ref[sl, cs], zi_ref[sl, cs], st_ref[hi]))
        for hi, (o, st_new) in enumerate(_round_robin(gens)):
            o_ref[sl, hi * LANES:(hi + 1) * LANES] = o
            st_ref[hi] = st_new
        return carry

    lax.fori_loop(0, ROWS // c, chunk, 0)


HG_HEADS_PER_STEP = 8


def _hgrn2(z, par, bsz, seq):
    t = z.shape[0]
    nr = seq // ROWS
    nh = HG_HEADS_PER_STEP
    ngrp = A_HEADS // nh
    width = nh * LANES
    blk = lambda off: pl.BlockSpec((ROWS, width), lambda b, h, r: (b * nr + r, off * ngrp + h))
    return pl.pallas_call(
        _hgrn2_kernel,
        grid=(bsz, ngrp, nr),
        in_specs=[blk(0), blk(1), blk(2),
                  pl.BlockSpec((SUBLANES, width), lambda b, h, r: (0, h))],
        out_specs=pl.BlockSpec((ROWS, width), lambda b, h, r: (b * nr + r, h)),
        out_shape=jax.ShapeDtypeStruct((t, A_HEADS * A_HEAD_DIM), F32),
        scratch_shapes=[pltpu.VMEM((nh, A_HEAD_DIM, A_HEAD_DIM), F32)],
        compiler_params=_params(("parallel", "parallel", "arbitrary")),
        name="hgrn2",
    )(z, z, z, par)


_EXP_M05 = math.exp(-0.5)
(_P_MU_R, _P_MU_K, _P_MU_V, _P_W0, _P_A0, _P_KK, _P_KA, _P_RK, _P_LNG, _P_LNB) = range(10)
_RW_NPAR = 16


def _rwkv_kernel(zr_ref, zk_ref, zv_ref, zwa_ref, par_ref, muwa_ref, w2_ref, a2_ref, o_ref,
                 st_ref, ext_ref, extwa_ref):
    c = CHUNK
    hd = B_HEAD_DIM
    npair = st_ref.shape[0]

    @pl.when(pl.program_id(2) == 0)
    def _():
        st_ref[...] = jnp.zeros_like(st_ref)
        ext_ref[:, 0:SUBLANES, :] = jnp.zeros((3, SUBLANES, npair * LANES), F32)
        extwa_ref[0:SUBLANES, :] = jnp.zeros((SUBLANES, LANES), F32)

    par_all = par_ref[...]

    def shift_mix(ref, ext, mu):
        cur = ref[...]
        ext[SUBLANES:SUBLANES + ROWS, :] = cur
        prev = ext[SUBLANES - 1:SUBLANES - 1 + ROWS, :]
        tail = cur[ROWS - SUBLANES:ROWS, :]
        ext[SUBLANES:SUBLANES + ROWS, :] = cur + (prev - cur) * mu
        ext[0:SUBLANES, :] = tail

    shift_mix(zr_ref, ext_ref.at[0], par_all[_P_MU_R:_P_MU_R + 1, :])
    shift_mix(zk_ref, ext_ref.at[1], par_all[_P_MU_K:_P_MU_K + 1, :])
    shift_mix(zv_ref, ext_ref.at[2], par_all[_P_MU_V:_P_MU_V + 1, :])
    shift_mix(zwa_ref, extwa_ref, muwa_ref[...])

    lane = lax.broadcasted_iota(jnp.int32, (c, LANES), 1)
    lo_l = lane < hd
    lane2 = lax.broadcasted_iota(jnp.int32, (2 * c, LANES), 1)
    row2 = lax.broadcasted_iota(jnp.int32, (2 * c, LANES), 0)
    s_idx = lane2 & (c - 1)
    t_idx = row2 & (c - 1)
    tri = (s_idx < t_idx) | ((row2 >= c) & (s_idx == t_idx))
    bd = (row2 >> int(math.log2(c))) == (lane2 >> int(math.log2(c)))
    strict_bd = bd & (s_idx < t_idx)
    eye2 = (row2 == lane2).astype(F32)
    ltri = (lax.broadcasted_iota(jnp.int32, (c, c), 1)
            <= lax.broadcasted_iota(jnp.int32, (c, c), 0)).astype(BF16)
    seg = ((lax.broadcasted_iota(jnp.int32, (LANES, LANES), 0) >> int(math.log2(hd)))
           == (lax.broadcasted_iota(jnp.int32, (LANES, LANES), 1) >> int(math.log2(hd))))
    seg_b = seg.astype(BF16)

    def segsum(x):
        h = x.astype(BF16)
        l = (x - h.astype(F32)).astype(BF16)
        return _nn(h, seg_b) + _nn(l, seg_b)

    def pair_chunk(pi, r, kraw, v, st, w2, a2, wab, twab):
        cs = slice(pi * LANES, (pi + 1) * LANES)
        prow = lambda i: par_all[i:i + 1, cs]

        xw = prow(_P_W0) + _nn(twab, w2)
        g = -_EXP_M05 * _sigmoid(xw)
        alpha = _sigmoid(prow(_P_A0) + _nn(wab, a2))
        kk = kraw * prow(_P_KK)
        nrm = jnp.sqrt(segsum(kk * kk))
        yield
        kk = kk / jnp.maximum(nrm, 1e-12)
        k = kraw * (1.0 + (alpha - 1.0) * prow(_P_KA))
        bv = kk * alpha

        gc = _nn_exact_lhs(ltri, g)
        yield
        g_last = gc[c - 1:c, :]
        e_pos = jnp.exp(gc)
        e_neg = jnp.exp(-gc)
        e_last = jnp.exp(g_last)
        at = -kk * jnp.exp(gc - g)
        rt = r * e_pos
        bt = bv * e_neg
        kt = k * e_neg
        bh = bt * e_last
        kh = kt * e_last

        zero = jnp.zeros_like(at)
        lhs_lo = jnp.concatenate([jnp.where(lo_l, at, zero), jnp.where(lo_l, rt, zero)], axis=0).astype(BF16)
        lhs_hi = jnp.concatenate([jnp.where(lo_l, zero, at), jnp.where(lo_l, zero, rt)], axis=0).astype(BF16)
        btb = bt.astype(BF16)
        ktb = kt.astype(BF16)
        aa_lo = jnp.where(tri, _nt(lhs_lo, jnp.concatenate([btb, ktb], axis=0)), 0.0)
        aa_hi = jnp.where(tri, _nt(lhs_hi, jnp.concatenate([ktb, btb], axis=0)), 0.0)
        yield

        a_rows = jnp.concatenate([aa_lo[0:c], aa_hi[0:c]], axis=0)
        abd = jnp.where(bd, a_rows, 0.0)
        akx = jnp.where(bd, 0.0, a_rows).astype(BF16)

        t_inv = eye2 + abd
        pw = abd
        stb = st.astype(BF16)
        vb = v.astype(BF16)
        vv = jnp.concatenate([vb, vb], axis=0)
        w_bd = _nt(jnp.concatenate([lhs_lo[0:c], lhs_hi[0:c]], axis=0), stb) + _nn(akx, vv)
        w_bd = jnp.where(bd, w_bd, 0.0).astype(BF16)
        rk_sum = segsum(r * k * prow(_P_RK))
        for _ in range(int(math.log2(c)) - 1):
            pwb = pw.astype(BF16)
            pw = _nn(pwb, pwb)
            yield
            t_inv = t_inv + _nn(pw.astype(BF16), t_inv.astype(BF16))
        yield

        u_bd = _nn(t_inv.astype(BF16), w_bd)
        yield
        u = u_bd[0:c] + u_bd[c:2 * c]
        ub = u.astype(BF16)

        y_lo = _nn(aa_lo[c:2 * c].astype(BF16), jnp.concatenate([ub, vb], axis=0))
        y_hi = _nn(aa_hi[c:2 * c].astype(BF16), jnp.concatenate([vb, ub], axis=0))
        y = _nt(rt.astype(BF16), stb) + jnp.where(lo_l, y_lo, y_hi)

        st_new = st * e_last + _tn(jnp.concatenate([ub, vb], axis=0),
                                   jnp.concatenate([bh.astype(BF16), kh.astype(BF16)], axis=0))
        st_new = jnp.where(seg, st_new, 0.0)
        yield

        inv_n = 1.0 / hd
        mu = segsum(y) * inv_n
        yield
        d = y - mu
        var = segsum(d * d) * inv_n
        yield
        yn = d * lax.rsqrt(var + B_LN_EPS) * prow(_P_LNG) + prow(_P_LNB)
        yn = yn + rk_sum * v
        return yn, st_new

    def chunk(ci, carry):
        sl = pl.ds(pl.multiple_of(ci * c, c) + SUBLANES, c)
        wa = extwa_ref[sl, :]
        wab = wa.astype(BF16)
        twab = jnp.tanh(wa).astype(BF16)
        args = []
        for pi in range(npair):
            cs = slice(pi * LANES, (pi + 1) * LANES)
            args.append((pi, ext_ref[0, sl, cs], ext_ref[1, sl, cs], ext_ref[2, sl, cs], st_ref[pi],
                         w2_ref[pi], a2_ref[pi], wab, twab))
        outs = _round_robin([pair_chunk(*a) for a in args])
        for pi, (yn, st_new) in enumerate(outs):
            o_ref[pl.ds(pl.multiple_of(ci * c, c), c), pi * LANES:(pi + 1) * LANES] = yn
            st_ref[pi] = st_new
        return carry

    lax.fori_loop(0, ROWS // c, chunk, 0)


RW_PAIRS_PER_STEP = 8


def _rwkv(z, par, muwa, w2p, a2p, bsz, seq, col_r, col_k, col_v, col_wa):
    t = z.shape[0]
    nr = seq // ROWS
    npp = RW_PAIRS_PER_STEP
    ngrp = (B_HEADS // 2) // npp
    width = npp * LANES
    blk = lambda off: pl.BlockSpec((ROWS, width), lambda b, p, r: (b * nr + r, off // npp + p))
    return pl.pallas_call(
        _rwkv_kernel,
        grid=(bsz, ngrp, nr),
        in_specs=[blk(col_r), blk(col_k), blk(col_v),
                  pl.BlockSpec((ROWS, LANES), lambda b, p, r: (b * nr + r, col_wa)),
                  pl.BlockSpec((_RW_NPAR, width), lambda b, p, r: (0, p)),
                  pl.BlockSpec((1, LANES), lambda b, p, r: (0, 0)),
                  pl.BlockSpec((npp, LANES, LANES), lambda b, p, r: (p, 0, 0)),
                  pl.BlockSpec((npp, LANES, LANES), lambda b, p, r: (p, 0, 0))],
        out_specs=pl.BlockSpec((ROWS, width), lambda b, p, r: (b * nr + r, p)),
        out_shape=jax.ShapeDtypeStruct((t, B_HEADS * B_HEAD_DIM), F32),
        scratch_shapes=[pltpu.VMEM((npp, LANES, LANES), F32),
                        pltpu.VMEM((3, SUBLANES + ROWS, width), F32),
                        pltpu.VMEM((SUBLANES + ROWS, LANES), F32)],
        compiler_params=_params(("parallel", "parallel", "arbitrary")),
        name="rwkv7",
    )(z, z, z, z, par, muwa, w2p, a2p)


def _moba_kernel(q_ref, k_ref, v_ref, o_ref, kmean_ref):
    nb = kmean_ref.shape[0]
    blk = C_BLOCK
    hd = C_HEAD_DIM
    n = pl.program_id(2)

    npair = q_ref.shape[1] // LANES

    @pl.when(n == 0)
    def _():
        for j in range(nb):
            kmean_ref[j:j + 1, :] = jnp.mean(k_ref[j * blk:(j + 1) * blk, :], axis=0, keepdims=True)

    lane = lax.broadcasted_iota(jnp.int32, (blk, LANES), 1)
    lo_l = lane < hd
    lane_m = lax.broadcasted_iota(jnp.int32, (nb, LANES), 1)
    jidx = lax.broadcasted_iota(jnp.int32, (nb, blk), 0)
    valid = jidx < n
    place_row = lax.broadcasted_iota(jnp.int32, (nb, LANES), 0)

    def select_bias(q, kmean, is_lo):
        km = jnp.where((lane_m < hd) if is_lo else (lane_m >= hd), kmean, 0.0)
        gate = lax.dot_general(km, q, (((1,), (1,)), ((), ())), precision=lax.Precision.HIGHEST,
                               preferred_element_type=F32)
        gm = jnp.where(valid, gate, -jnp.inf)
        cnt = jnp.zeros((nb, blk), F32)
        for j2 in range(nb):
            gj = gm[j2:j2 + 1, :]
            better = (gj > gm) | ((gj == gm) & (j2 < jidx))
            cnt = cnt + jnp.where(better, 1.0, 0.0)
        sel = valid & (cnt < float(C_TOPK))
        bias = jnp.where(sel, 0.0, NEG_BIG).astype(BF16)
        off = hd if is_lo else 0
        place = (lane_m == place_row + off).astype(BF16)
        return _tn(bias, place)

    scale = hd ** -0.5
    hi_l = lane >= hd
    heads = [(pi, is_lo) for pi in range(npair) for is_lo in (True, False)]
    q_aug = []
    for pi, is_lo in heads:
        cs = slice(pi * LANES, (pi + 1) * LANES)
        q = q_ref[:, cs]
        bias = select_bias(q, kmean_ref[:, cs], is_lo)
        q_aug.append(jnp.where(lo_l if is_lo else hi_l, q * scale, bias).astype(BF16))

    rowq = lax.broadcasted_iota(jnp.int32, (blk, blk), 0)
    colk = lax.broadcasted_iota(jnp.int32, (blk, blk), 1)
    causal = colk <= rowq

    def head_block(qh, kh, vj, prev, mask_causal):
        s = _nt(qh, kh)
        yield
        if mask_causal:
            s = jnp.where(causal, s, -jnp.inf)
        smax = jnp.max(s, axis=-1, keepdims=True)
        if prev is None:
            m_new = smax
            p = jnp.exp(s - m_new)
            l_new = jnp.sum(p, axis=-1, keepdims=True)
            pv = _nn(p.astype(BF16), vj)
            yield
            return m_new, l_new, pv
        m, l, acc = prev
        m_new = jnp.maximum(m, smax)
        corr = jnp.exp(m - m_new)
        p = jnp.exp(s - m_new)
        l_new = l * corr + jnp.sum(p, axis=-1, keepdims=True)
        pv = _nn(p.astype(BF16), vj)
        yield
        return m_new, l_new, acc * corr + pv

    def visit(j, carry, own):
        st = pl.multiple_of(j * blk, blk)
        gens = []
        for hidx, (pi, is_lo) in enumerate(heads):
            cs = slice(pi * LANES, (pi + 1) * LANES)
            kj = k_ref[pl.ds(st, blk), cs].astype(BF16)
            vj = v_ref[pl.ds(st, blk), cs].astype(BF16)
            if own:
                other = jnp.zeros_like(kj)
            else:
                other = (lane == (hd if is_lo else 0) + j).astype(BF16)
            kh = jnp.where(lo_l if is_lo else hi_l, kj, other)
            gens.append(head_block(q_aug[hidx], kh, vj, None if own else carry[hidx], own))
        return tuple(_round_robin(gens))

    res = lax.fori_loop(0, n, lambda j, carry: visit(j, carry, False), visit(n, None, True))
    for pi in range(npair):
        (_, l_lo, acc_lo), (_, l_hi, acc_hi) = res[2 * pi], res[2 * pi + 1]
        o_ref[:, pi * LANES:(pi + 1) * LANES] = jnp.where(lo_l, acc_lo / l_lo, acc_hi / l_hi)


MB_PAIRS_PER_STEP = 2


def _moba(z, bsz, seq, col_q, col_k, col_v):
    t = z.shape[0]
    nb = seq // C_BLOCK
    npp = MB_PAIRS_PER_STEP
    ngrp = (C_HEADS // 2) // npp
    width = npp * LANES
    return pl.pallas_call(
        _moba_kernel,
        grid=(bsz, ngrp, nb),
        in_specs=[pl.BlockSpec((C_BLOCK, width), lambda b, p, n: (b * nb + n, col_q // npp + p)),
                  pl.BlockSpec((seq, width), lambda b, p, n: (b, col_k // npp + p)),
                  pl.BlockSpec((seq, width), lambda b, p, n: (b, col_v // npp + p))],
        out_specs=pl.BlockSpec((C_BLOCK, width), lambda b, p, n: (b * nb + n, p)),
        out_shape=jax.ShapeDtypeStruct((t, C_HEADS * C_HEAD_DIM), F32),
        scratch_shapes=[pltpu.VMEM((nb, width), F32)],
        compiler_params=_params(("parallel", "parallel", "arbitrary")),
        name="moba",
    )(z, z, z)


def _rglru_kernel(x_ref, cw_ref, vec_ref, wa_ref, wx_ref, o_ref, ext_ref, a_ref, b_ref, h_ref):
    rows = x_ref.shape[0]
    width = x_ref.shape[1]

    @pl.when(pl.program_id(1) == 0)
    def _():
        ext_ref[0:SUBLANES, :] = jnp.zeros((SUBLANES, width), F32)
        h_ref[...] = jnp.zeros_like(h_ref)

    x = x_ref[...]
    ext_ref[SUBLANES:SUBLANES + rows, :] = x
    xc = vec_ref[0:1, :] + x * cw_ref[D_CONV - 1:D_CONV, :]
    for d in range(1, D_CONV):
        xc = xc + ext_ref[SUBLANES - d:SUBLANES - d + rows, :] * cw_ref[D_CONV - 1 - d:D_CONV - d, :]
    ext_ref[0:SUBLANES, :] = x[rows - SUBLANES:rows, :]

    ba = vec_ref[1:2, :]
    bx = vec_ref[2:3, :]
    lam = vec_ref[3:4, :]
    sp = jnp.maximum(-lam, 0.0) + jnp.log(1.0 + jnp.exp(-jnp.abs(lam)))
    for nblk in range(D_BLOCKS):
        cs = slice(nblk * D_BLOCK_DIM, (nblk + 1) * D_BLOCK_DIM)
        xb = xc[:, cs]
        xbb = xb.astype(BF16)
        rg = _sigmoid(_nn(xbb, wa_ref[nblk]) + ba[:, cs])
        ig = _sigmoid(_nn(xbb, wx_ref[nblk]) + bx[:, cs])
        log_a = -LRU_C * rg * sp[:, cs]
        th = jnp.tanh(log_a)
        one_minus_a2 = -2.0 * th / (1.0 - th)
        a_ref[:, cs] = jnp.exp(log_a)
        b_ref[:, cs] = jnp.sqrt(one_minus_a2) * (ig * xb)

    rowi = lax.broadcasted_iota(jnp.int32, (SUBLANES, width), 0)

    def group(gi, carry):
        sl = pl.ds(pl.multiple_of(gi * SUBLANES, SUBLANES), SUBLANES)
        a = a_ref[sl, :]
        b = b_ref[sl, :]
        for d in (1, 2, 4):
            keep = rowi >= d
            a_sh = jnp.where(keep, pltpu.roll(a, d, 0), 1.0)
            b_sh = jnp.where(keep, pltpu.roll(b, d, 0), 0.0)
            b = a * b_sh + b
            a = a * a_sh
        h = a * carry + b
        o_ref[sl, :] = h
        return jnp.broadcast_to(h[SUBLANES - 1:SUBLANES, :], (SUBLANES, width))

    h_ref[...] = lax.fori_loop(0, rows // SUBLANES, group, h_ref[...])


def _rglru(z, conv_w, vecs, wa, wx, bsz, seq, col_x):
    t = z.shape[0]
    width = D_BLOCKS * D_BLOCK_DIM
    nr = seq // ROWS
    full = lambda shape: pl.BlockSpec(shape, lambda b, r: (0,) * len(shape))
    return pl.pallas_call(
        _rglru_kernel,
        grid=(bsz, nr),
        in_specs=[pl.BlockSpec((ROWS, width), lambda b, r: (b * nr + r, col_x)),
                  full((D_CONV, width)), full((SUBLANES, width)),
                  full((D_BLOCKS, D_BLOCK_DIM, D_BLOCK_DIM)), full((D_BLOCKS, D_BLOCK_DIM, D_BLOCK_DIM))],
        out_specs=pl.BlockSpec((ROWS, width), lambda b, r: (b * nr + r, 0)),
        out_shape=jax.ShapeDtypeStruct((t, width), F32),
        scratch_shapes=[pltpu.VMEM((SUBLANES + ROWS, width), F32),
                        pltpu.VMEM((ROWS, width), F32),
                        pltpu.VMEM((ROWS, width), F32),
                        pltpu.VMEM((SUBLANES, width), F32)],
        compiler_params=_params(("parallel", "arbitrary")),
        name="rglru",
    )(z, conv_w, vecs, wa, wx)


def _outproj_kernel(*refs, nbr, final):
    br = refs[:nbr]
    gt = refs[nbr:2 * nbr]
    ws = refs[2 * nbr:3 * nbr]
    x_ref, p_ref, pp_ref, pg_ref, pn_ref = refs[3 * nbr:3 * nbr + 5]
    rest = refs[3 * nbr + 5:]
    if final:
        fn_ref, o_ref = rest
    else:
        (o_ref,) = rest

    acc = x_ref[...]
    for b, g, w in zip(br, gt, ws):
        gv = g[...]
        y = b[...] * (gv * _sigmoid(gv))
        acc = acc + _nn(y.astype(BF16), w[...])
    ms = jnp.mean(acc * acc, axis=-1, keepdims=True)
    hn = (acc * lax.rsqrt(ms + EPS) * pn_ref[...]).astype(BF16)
    gate = _sigmoid(_nn(hn, pg_ref[...]))
    out = acc + _nn(p_ref[...].astype(BF16), pp_ref[...]) * gate
    if final:
        ms2 = jnp.mean(out * out, axis=-1, keepdims=True)
        out = out * lax.rsqrt(ms2 + EPS) * fn_ref[...]
    o_ref[...] = out


def _outproj(branches, z, gate_cols, w_parts, x, p_i, ple_proj, ple_gate, ple_norm, final_norm, tm):
    t, d = x.shape
    nbr = len(branches)
    in_specs, args = [], []
    for b in branches:
        in_specs.append(pl.BlockSpec((tm, b.shape[1]), lambda i: (i, 0)))
        args.append(b)
    for b, gc in zip(branches, gate_cols):
        in_specs.append(pl.BlockSpec((tm, b.shape[1]), lambda i, gc=gc: (i, gc)))
        args.append(z)
    for w in w_parts:
        in_specs.append(pl.BlockSpec(w.shape, lambda i: (0, 0)))
        args.append(w)
    in_specs += [pl.BlockSpec((tm, d), lambda i: (i, 0)),
                 pl.BlockSpec((tm, P_DIM), lambda i: (i, 0)),
                 pl.BlockSpec((P_DIM, d), lambda i: (0, 0)),
                 pl.BlockSpec((d, d), lambda i: (0, 0)),
                 pl.BlockSpec((1, d), lambda i: (0, 0))]
    args += [x, p_i, ple_proj, ple_gate, ple_norm.reshape(1, d)]
    final = final_norm is not None
    if final:
        in_specs.append(pl.BlockSpec((1, d), lambda i: (0, 0)))
        args.append(final_norm.reshape(1, d))
    return pl.pallas_call(
        functools.partial(_outproj_kernel, nbr=nbr, final=final),
        grid=(t // tm,),
        in_specs=in_specs,
        out_specs=pl.BlockSpec((tm, d), lambda i: (i, 0)),
        out_shape=jax.ShapeDtypeStruct((t, d), F32),
        compiler_params=_params(("parallel",)),
        name="outproj",
    )(*args)


def _even_layer(x, bsz, seq, norm_g, w_in, w_out, lb, hg_norm, mu, w0, w2, a0, a2, k_k, k_a, r_k, ln_g, ln_b,
                p_i, ple_proj, ple_gate, ple_norm, final_norm):
    aw = A_HEADS * A_HEAD_DIM
    bw = B_HEADS * B_HEAD_DIM
    rw0 = 3 * aw
    lr0 = rw0 + 3 * bw
    g0 = lr0 + 2 * B_RANK
    w_perm = jnp.concatenate([w_in[:, :lr0], w_in[:, g0:], w_in[:, lr0:g0]], axis=1).astype(BF16)
    z = _inproj(x, norm_g, w_perm, tm=1024, tn=w_perm.shape[1] // 5)

    zero = jnp.zeros_like(lb)
    hpar = jnp.stack([jnp.log(lb), jnp.log1p(-lb), 1.0 - lb, hg_norm, zero, zero, zero, zero])
    oa = _hgrn2(z, hpar, bsz, seq)

    zrow = jnp.zeros((bw,), F32)
    rows = [mu[0:bw], mu[bw:2 * bw], mu[2 * bw:3 * bw], w0, a0, k_k, k_a, r_k, ln_g, ln_b]
    rpar = jnp.stack(rows + [zrow] * (_RW_NPAR - len(rows)))
    muwa = mu[3 * bw:].reshape(1, 2 * B_RANK)
    npair = B_HEADS // 2
    zpad = jnp.zeros((npair, B_RANK, LANES), F32)
    w2p = jnp.concatenate([w2.reshape(B_RANK, npair, LANES).transpose(1, 0, 2), zpad], axis=1).astype(BF16)
    a2p = jnp.concatenate([zpad, a2.reshape(B_RANK, npair, LANES).transpose(1, 0, 2)], axis=1).astype(BF16)
    cb = lambda c: c // LANES
    ob = _rwkv(z, rpar, muwa, w2p, a2p, bsz, seq, cb(rw0), cb(rw0 + bw), cb(rw0 + 2 * bw), cb(lr0 + aw + bw))

    gate0 = lr0
    w_out_b = w_out.astype(BF16)
    return _outproj([oa, ob], z, [gate0 // aw, (gate0 + aw) // bw], [w_out_b[:aw], w_out_b[aw:]],
                    x, p_i, ple_proj.astype(BF16), ple_gate.astype(BF16), ple_norm, final_norm, tm=256)


def _odd_layer(x, bsz, seq, norm_g, w_in, w_out, conv_w, conv_b, wa, ba, wx, bx, lam,
               p_i, ple_proj, ple_gate, ple_norm, final_norm):
    cw = C_HEADS * C_HEAD_DIM
    dw = D_BLOCKS * D_BLOCK_DIM
    x0 = 3 * cw
    g0 = x0 + dw
    w_perm = jnp.concatenate([w_in[:, :x0], w_in[:, g0:g0 + cw], w_in[:, x0:g0], w_in[:, g0 + cw:]],
                             axis=1).astype(BF16)
    z = _inproj(x, norm_g, w_perm, tm=1024, tn=w_perm.shape[1] // 4)

    cb = lambda c: c // LANES
    oc = _moba(z, bsz, seq, cb(0), cb(cw), cb(2 * cw))
    zero = jnp.zeros_like(lam)
    vecs = jnp.stack([conv_b, ba, bx, lam, zero, zero, zero, zero])
    od = _rglru(z, conv_w, vecs, wa.astype(BF16), wx.astype(BF16), bsz, seq, (x0 + cw) // dw)

    w_out_b = w_out.astype(BF16)
    return _outproj([oc, od], z, [x0 // cw, (x0 + cw + dw) // dw], [w_out_b[:cw], w_out_b[cw:]],
                    x, p_i, ple_proj.astype(BF16), ple_gate.astype(BF16), ple_norm, final_norm, tm=256)


def kernel(x, p, ev_norm, ev_w_in, ev_w_out, hg_lb_logits, hg_norm, rw_mu, rw_w0, rw_w2, rw_a0, rw_a2, rw_k_k, rw_k_a, rw_r_k, rw_ln_g, rw_ln_b, od_norm, od_w_in, od_w_out, lru_conv_w, lru_conv_b, lru_wa, lru_ba, lru_wx, lru_bx, lru_lambda, ple_proj, ple_gate, ple_norm, final_norm):
    bsz, seq, d = x.shape
    depth = p.shape[0]
    s = jax.nn.softmax(hg_lb_logits.astype(F32), axis=0)
    lower_bounds = jnp.maximum(jnp.cumsum(s, axis=0) - s[0], 0.0)
    xf = x.reshape(bsz * seq, d)
    pf = p.reshape(depth, bsz * seq, p.shape[-1])
    for i in range(depth):
        j = i // 2
        fin = final_norm if i == depth - 1 else None
        if i % 2 == 0:
            xf = _even_layer(xf, bsz, seq, ev_norm[j], ev_w_in[j], ev_w_out[j], lower_bounds[j], hg_norm[j],
                             rw_mu[j], rw_w0[j], rw_w2[j], rw_a0[j], rw_a2[j], rw_k_k[j], rw_k_a[j], rw_r_k[j],
                             rw_ln_g[j], rw_ln_b[j], pf[i], ple_proj[i], ple_gate[i], ple_norm[i], fin)
        else:
            xf = _odd_layer(xf, bsz, seq, od_norm[j], od_w_in[j], od_w_out[j], lru_conv_w[j], lru_conv_b[j],
                            lru_wa[j], lru_ba[j], lru_wx[j], lru_bx[j], lru_lambda[j],
                            pf[i], ple_proj[i], ple_gate[i], ple_norm[i], fin)
    return xf.reshape(bsz, seq, d)
```

```python
import functools
import math

import jax
import jax.numpy as jnp
from jax import lax
from jax.experimental import pallas as pl
from jax.experimental.pallas import tpu as pltpu

F32 = jnp.float32
BF16 = jnp.bfloat16

EPS = 1e-6
LANES = 128
SUBLANES = 8
VMEM_LIMIT = 48 * 1024 * 1024

A_HEADS, A_HEAD_DIM = 8, 128
B_HEADS, B_HEAD_DIM = 16, 64
B_RANK = 64
B_LN_EPS = 64e-5
C_HEADS, C_HEAD_DIM, C_BLOCK, C_TOPK = 8, 64, 256, 3
D_BLOCKS, D_BLOCK_DIM, D_CONV = 8, 128, 4
LRU_C = 8.0
P_DIM = 256

CHUNK = 64
ROWS = 256
NEG_BIG = -16384.0


def _nt(a, b):
    return lax.dot_general(a, b, (((1,), (1,)), ((), ())), preferred_element_type=F32)


def _tn(a, b):
    return lax.dot_general(a, b, (((0,), (0,)), ((), ())), preferred_element_type=F32)


def _nn(a, b):
    return jnp.dot(a, b, preferred_element_type=F32)


def _split3(x):
    h = x.astype(BF16)
    r = x - h.astype(F32)
    m = r.astype(BF16)
    l = (r - m.astype(F32)).astype(BF16)
    return h, m, l


def _nn_exact_lhs(lhs_bf16, x):
    h, m, l = _split3(x)
    return _nn(lhs_bf16, h) + _nn(lhs_bf16, m) + _nn(lhs_bf16, l)


def _nn_exact_rhs(x, rhs_bf16):
    h, m, l = _split3(x)
    return _nn(h, rhs_bf16) + _nn(m, rhs_bf16) + _nn(l, rhs_bf16)


def _sigmoid(x):
    e = jnp.exp(-jnp.abs(x))
    r = 1.0 / (1.0 + e)
    return jnp.where(x >= 0, r, e * r)


def _round_robin(gens, skew=0):
    results = [None] * len(gens)
    done = [False] * len(gens)
    rnd = 0
    while not all(done):
        for i in reversed(range(len(gens))):
            if done[i] or rnd < i * skew:
                continue
            try:
                next(gens[i])
            except StopIteration as stop:
                results[i] = stop.value
                done[i] = True
        rnd += 1
    return results


def _params(sem):
    return pltpu.CompilerParams(dimension_semantics=sem, vmem_limit_bytes=VMEM_LIMIT)


def _inproj_kernel(x_ref, g_ref, w_ref, o_ref, h_ref):
    @pl.when(pl.program_id(1) == 0)
    def _():
        x = x_ref[...]
        ms = jnp.mean(x * x, axis=-1, keepdims=True)
        h_ref[...] = (x * lax.rsqrt(ms + EPS) * g_ref[...]).astype(BF16)

    o_ref[...] = _nn(h_ref[...], w_ref[...])


def _inproj(x, g, w, tm, tn):
    t, d = x.shape
    n = w.shape[1]
    return pl.pallas_call(
        _inproj_kernel,
        grid=(t // tm, n // tn),
        in_specs=[
            pl.BlockSpec((tm, d), lambda i, j: (i, 0)),
            pl.BlockSpec((1, d), lambda i, j: (0, 0)),
            pl.BlockSpec((d, tn), lambda i, j: (0, j)),
        ],
        out_specs=pl.BlockSpec((tm, tn), lambda i, j: (i, j)),
        out_shape=jax.ShapeDtypeStruct((t, n), F32),
        scratch_shapes=[pltpu.VMEM((tm, d), BF16)],
        compiler_params=_params(("parallel", "arbitrary")),
        name="inproj",
    )(x, g.reshape(1, d), w)


_LEVELS = (1, 2, 4, 8, 16, 32)


def _boundary_bcast(g, m):
    c, w = g.shape
    if m >= SUBLANES:
        gr = g.reshape(c // (2 * m), 2 * m, w)
        b = jnp.broadcast_to(gr[:, m - 1:m, :], gr.shape)
        return b.reshape(c, w)
    gr = g.reshape(c // SUBLANES, SUBLANES, w)
    j = lax.broadcasted_iota(jnp.int32, gr.shape, 1)
    rows = [jnp.broadcast_to(gr[:, r:r + 1, :], gr.shape) for r in range(m - 1, SUBLANES, 2 * m)]
    b = rows[-1]
    for idx in range(len(rows) - 2, -1, -1):
        b = jnp.where(j < (idx + 1) * 2 * m, rows[idx], b)
    return b.reshape(c, w)


def _hgrn2_kernel(zq_ref, zf_ref, zi_ref, par_ref, o_ref, st_ref):
    c = CHUNK

    @pl.when(pl.program_id(2) == 0)
    def _():
        st_ref[...] = jnp.zeros_like(st_ref)

    row = lax.broadcasted_iota(jnp.int32, (c, c), 0)
    col = lax.broadcasted_iota(jnp.int32, (c, c), 1)
    ltri = (col <= row).astype(BF16)
    eye = col == row
    lvl_masks = []
    for m in _LEVELS:
        sh = int(math.log2(2 * m))
        same = (row >> sh) == (col >> sh)
        lvl_masks.append(same & ((row & (2 * m - 1)) >= m) & ((col & (2 * m - 1)) < m))
    rowi = lax.broadcasted_iota(jnp.int32, (c, LANES), 0)

    nheads = st_ref.shape[0]
    par_all = par_ref[...]

    def head_chunk(hi, zq, zf, v, st):
        cs = slice(hi * LANES, (hi + 1) * LANES)
        log_lb = par_all[0:1, cs]
        log_omlb = par_all[1:2, cs]
        omlb = par_all[2:3, cs]
        gain = par_all[3:4, cs]

        e = jnp.exp(-jnp.abs(zf))
        r = 1.0 / (1.0 + e)
        sig_neg = jnp.where(zf >= 0, e * r, r)
        log_sig = jnp.minimum(zf, 0.0) - jnp.log(1.0 + e)
        bterm = log_omlb + log_sig
        mx = jnp.maximum(log_lb, bterm)
        log_f = mx + jnp.log(1.0 + jnp.exp(-jnp.abs(log_lb - bterm)))
        k = omlb * sig_neg
        q = zq * _sigmoid(zq)

        g = _nn_exact_lhs(ltri, log_f)
        yield
        g_last = g[c - 1:c, :]

        qb = q.astype(BF16)
        kb = k.astype(BF16)
        vb = v.astype(BF16)
        attn = jnp.where(eye, _nt(qb, kb), 0.0)
        for m, msk in zip(_LEVELS, lvl_masks):
            second = (rowi & (2 * m - 1)) >= m
            if m == 1:
                fac = jnp.where(second, jnp.exp(log_f), 1.0)
            else:
                bnd = _boundary_bcast(g, m)
                fac = jnp.exp(jnp.where(second, g - bnd, bnd - g))
            x = (jnp.where(second, q, k) * fac).astype(BF16)
            attn = attn + jnp.where(msk, _nt(x, x), 0.0)
        yield

        o = _nn(attn.astype(BF16), vb) + _nt((q * jnp.exp(g)).astype(BF16), st.astype(BF16))
        khat = (k * jnp.exp(g_last - g)).astype(BF16)
        st_new = st * jnp.exp(g_last) + _tn(vb, khat)
        yield

        ms = jnp.mean(o * o, axis=-1, keepdims=True)
        return o * lax.rsqrt(ms + EPS) * gain, st_new

    def chunk(ci, carry):
        sl = pl.ds(pl.multiple_of(ci * c, c), c)
        gens = []
        for hi in range(nheads):
            cs = slice(hi * LANES, (hi + 1) * LANES)
            gens.append(head_chunk(hi, zq_ref[sl, cs], zf_ref[sl, cs], zi_ref[sl, cs], st_ref[hi]))
        for hi, (o, st_new) in enumerate(_round_robin(gens)):
            o_ref[sl, hi * LANES:(hi + 1) * LANES] = o
            st_ref[hi] = st_new
        return carry

    lax.fori_loop(0, ROWS // c, chunk, 0)


HG_HEADS_PER_STEP = 8


def _hgrn2(z, par, bsz, seq):
    t = z.shape[0]
    nr = seq // ROWS
    nh = HG_HEADS_PER_STEP
    ngrp = A_HEADS // nh
    width = nh * LANES
    blk = lambda off: pl.BlockSpec((ROWS, width), lambda b, h, r: (b * nr + r, off * ngrp + h))
    return pl.pallas_call(
        _hgrn2_kernel,
        grid=(bsz, ngrp, nr),
        in_specs=[blk(0), blk(1), blk(2),
                  pl.BlockSpec((SUBLANES, width), lambda b, h, r: (0, h))],
        out_specs=pl.BlockSpec((ROWS, width), lambda b, h, r: (b * nr + r, h)),
        out_shape=jax.ShapeDtypeStruct((t, A_HEADS * A_HEAD_DIM), F32),
        scratch_shapes=[pltpu.VMEM((nh, A_HEAD_DIM, A_HEAD_DIM), F32)],
        compiler_params=_params(("parallel", "parallel", "arbitrary")),
        name="hgrn2",
    )(z, z, z, par)


_EXP_M05 = math.exp(-0.5)
(_P_MU_R, _P_MU_K, _P_MU_V, _P_W0, _P_A0, _P_KK, _P_KA, _P_RK, _P_LNG, _P_LNB) = range(10)
_RW_NPAR = 16


def _rwkv_kernel(zr_ref, zk_ref, zv_ref, zwa_ref, par_ref, muwa_ref, w2_ref, a2_ref, o_ref,
                 st_ref, ext_ref, extwa_ref):
    c = CHUNK
    hd = B_HEAD_DIM
    nbat, npair = st_ref.shape[0], st_ref.shape[1]

    @pl.when(pl.program_id(2) == 0)
    def _():
        st_ref[...] = jnp.zeros_like(st_ref)
        ext_ref[:, :, 0:SUBLANES, :] = jnp.zeros((nbat, 3, SUBLANES, npair * LANES), F32)
        extwa_ref[:, 0:SUBLANES, :] = jnp.zeros((nbat, SUBLANES, LANES), F32)

    par_all = par_ref[...]

    @pl.when(pl.program_id(2) > 0)
    def _():
        ext_ref[:, :, 0:SUBLANES, :] = ext_ref[:, :, ROWS:SUBLANES + ROWS, :]
        extwa_ref[:, 0:SUBLANES, :] = extwa_ref[:, ROWS:SUBLANES + ROWS, :]

    for bi in range(nbat):
        ext_ref[bi, 0, SUBLANES:SUBLANES + ROWS, :] = zr_ref[bi]
        ext_ref[bi, 1, SUBLANES:SUBLANES + ROWS, :] = zk_ref[bi]
        ext_ref[bi, 2, SUBLANES:SUBLANES + ROWS, :] = zv_ref[bi]
        extwa_ref[bi, SUBLANES:SUBLANES + ROWS, :] = zwa_ref[bi]

    def mixed(ext, ci, cs, mu):
        win = ext[pl.ds(pl.multiple_of(ci * c, c), c + SUBLANES), cs]
        cur = win[SUBLANES:SUBLANES + c]
        prev = win[SUBLANES - 1:SUBLANES - 1 + c]
        return cur + (prev - cur) * mu

    lane = lax.broadcasted_iota(jnp.int32, (c, LANES), 1)
    lo_l = lane < hd
    lane2 = lax.broadcasted_iota(jnp.int32, (2 * c, LANES), 1)
    row2 = lax.broadcasted_iota(jnp.int32, (2 * c, LANES), 0)
    s_idx = lane2 & (c - 1)
    t_idx = row2 & (c - 1)
    tri = (s_idx < t_idx) | ((row2 >= c) & (s_idx == t_idx))
    eye_s = ((lane & (c - 1)) == lax.broadcasted_iota(jnp.int32, (c, LANES), 0)).astype(F32)
    ltri = (lax.broadcasted_iota(jnp.int32, (c, c), 1)
            <= lax.broadcasted_iota(jnp.int32, (c, c), 0)).astype(BF16)
    ltri2 = jnp.concatenate([ltri, ltri], axis=1)
    seg = ((lax.broadcasted_iota(jnp.int32, (LANES, LANES), 0) >> int(math.log2(hd)))
           == (lax.broadcasted_iota(jnp.int32, (LANES, LANES), 1) >> int(math.log2(hd))))
    seg_b = seg.astype(BF16)
    seg2 = jnp.concatenate([seg_b, seg_b], axis=0)

    def split2(x):
        h = x.astype(BF16)
        return h, (x - h.astype(F32)).astype(BF16)

    def segsum(x):
        return _nn(jnp.concatenate(split2(x), axis=1), seg2)

    def by_head(x):
        zero = jnp.zeros_like(x)
        return jnp.concatenate([jnp.where(lo_l, x, zero), jnp.where(lo_l, zero, x)], axis=0)

    def pair_chunk(pi, r, kraw, v, st, w2, a2, wab, twab):
        cs = slice(pi * LANES, (pi + 1) * LANES)
        prow = lambda i: par_all[i:i + 1, cs]

        xw = prow(_P_W0) + _nn(twab, w2)
        g = -_EXP_M05 * _sigmoid(xw)
        alpha = _sigmoid(prow(_P_A0) + _nn(wab, a2))
        yield
        kk = kraw * prow(_P_KK)
        k = kraw * (1.0 + (alpha - 1.0) * prow(_P_KA))
        sums = segsum(jnp.concatenate([kk * kk, r * k * prow(_P_RK)], axis=0))
        gc = _nn(ltri2, jnp.concatenate(split2(g), axis=0))
        yield
        kk = kk / jnp.maximum(jnp.sqrt(sums[0:c]), 1e-12)
        rk_sum = sums[c:2 * c]
        bv = kk * alpha
        g_last = gc[c - 1:c, :]
        e_pos = jnp.exp(gc)
        e_neg = jnp.exp(-gc)
        e_last = jnp.exp(g_last)
        at = -kk * jnp.exp(gc - g)
        rt = r * e_pos
        bt = bv * e_neg
        kt = k * e_neg
        bh = bt * e_last
        kh = kt * e_last

        atb = at.astype(BF16)
        rtb = rt.astype(BF16)
        lhs = jnp.concatenate([atb, rtb], axis=0)
        ab_rb = jnp.where(tri, _nt(lhs, by_head(bt.astype(BF16))), 0.0)
        ak_rk = jnp.where(tri, _nt(lhs, by_head(kt.astype(BF16))), 0.0)
        yield

        stb = st.astype(BF16)
        vh = by_head(v.astype(BF16))
        w = _nt(atb, stb) + _nn(ak_rk[0:c].astype(BF16), vh)
        pw = ab_rb[0:c]
        t_inv = eye_s + pw
        for _ in range(int(math.log2(c)) - 1):
            pwb = pw.astype(BF16)
            pw = _nn(pwb, by_head(pwb))
            yield
            t_inv = t_inv + _nn(pw.astype(BF16), by_head(t_inv.astype(BF16)))
        yield

        u = _nn(t_inv.astype(BF16), by_head(w.astype(BF16)))
        yield
        ub = u.astype(BF16)
        a_r = jnp.concatenate([ab_rb[c:2 * c], ak_rk[c:2 * c]], axis=1).astype(BF16)
        y = _nt(rtb, stb) + _nn(a_r, jnp.concatenate([by_head(ub), vh], axis=0))

        st_new = st * e_last + _tn(jnp.concatenate([ub, v.astype(BF16)], axis=0),
                                   jnp.concatenate([bh.astype(BF16), kh.astype(BF16)], axis=0))
        st_new = jnp.where(seg, st_new, 0.0)
        yield

        inv_n = 1.0 / hd
        moments = segsum(jnp.concatenate([y, y * y], axis=0)) * inv_n
        yield
        mu = moments[0:c]
        var = moments[c:2 * c] - mu * mu
        yn = (y - mu) * lax.rsqrt(var + B_LN_EPS) * prow(_P_LNG) + prow(_P_LNB)
        yn = yn + rk_sum * v
        return yn, st_new

    def chunk(ci, carry):
        args = []
        for bi in range(nbat):
            wa = mixed(extwa_ref.at[bi], ci, slice(None), muwa_ref[...])
            wab = wa.astype(BF16)
            twab = jnp.tanh(wa).astype(BF16)
            for pi in range(npair):
                cs = slice(pi * LANES, (pi + 1) * LANES)
                mu_of = lambda i: par_all[i:i + 1, cs]
                args.append((pi, mixed(ext_ref.at[bi, 0], ci, cs, mu_of(_P_MU_R)),
                             mixed(ext_ref.at[bi, 1], ci, cs, mu_of(_P_MU_K)),
                             mixed(ext_ref.at[bi, 2], ci, cs, mu_of(_P_MU_V)),
                             st_ref[bi, pi], w2_ref[pi], a2_ref[pi], wab, twab))
        outs = _round_robin([pair_chunk(*a) for a in args])
        for idx, (yn, st_new) in enumerate(outs):
            bi, pi = divmod(idx, npair)
            o_ref[bi, pl.ds(pl.multiple_of(ci * c, c), c), pi * LANES:(pi + 1) * LANES] = yn
            st_ref[bi, pi] = st_new
        return carry

    lax.fori_loop(0, ROWS // c, chunk, 0)


RW_PAIRS_PER_STEP = 8
RW_SEQS_PER_STEP = 2


def _rwkv(z, par, muwa, w2p, a2p, bsz, seq, col_r, col_k, col_v, col_wa):
    t, n = z.shape
    nr = seq // ROWS
    npp = RW_PAIRS_PER_STEP
    nsq = RW_SEQS_PER_STEP
    ngrp = (B_HEADS // 2) // npp
    width = npp * LANES
    z3 = z.reshape(bsz, seq, n)
    blk = lambda off: pl.BlockSpec((nsq, ROWS, width), lambda b, p, r: (b, r, off // npp + p))
    out = pl.pallas_call(
        _rwkv_kernel,
        grid=(bsz // nsq, ngrp, nr),
        in_specs=[blk(col_r), blk(col_k), blk(col_v),
                  pl.BlockSpec((nsq, ROWS, LANES), lambda b, p, r: (b, r, col_wa)),
                  pl.BlockSpec((_RW_NPAR, width), lambda b, p, r: (0, p)),
                  pl.BlockSpec((1, LANES), lambda b, p, r: (0, 0)),
                  pl.BlockSpec((npp, LANES, LANES), lambda b, p, r: (p, 0, 0)),
                  pl.BlockSpec((npp, LANES, LANES), lambda b, p, r: (p, 0, 0))],
        out_specs=pl.BlockSpec((nsq, ROWS, width), lambda b, p, r: (b, r, p)),
        out_shape=jax.ShapeDtypeStruct((bsz, seq, B_HEADS * B_HEAD_DIM), F32),
        scratch_shapes=[pltpu.VMEM((nsq, npp, LANES, LANES), F32),
                        pltpu.VMEM((nsq, 3, SUBLANES + ROWS, width), F32),
                        pltpu.VMEM((nsq, SUBLANES + ROWS, LANES), F32)],
        compiler_params=_params(("parallel", "parallel", "arbitrary")),
        name="rwkv7",
    )(z3, z3, z3, z3, par, muwa, w2p, a2p)
    return out.reshape(t, B_HEADS * B_HEAD_DIM)


def _moba_kernel(q_ref, k_ref, v_ref, o_ref, kmean_ref, kb_ref, vt_ref, bias_ref):
    nb = kmean_ref.shape[0]
    blk = C_BLOCK
    hd = C_HEAD_DIM
    n = pl.program_id(2)
    npair = q_ref.shape[1] // LANES

    @pl.when(n == 0)
    def _():
        for j in range(nb):
            kj = k_ref[j * blk:(j + 1) * blk, :]
            kmean_ref[j:j + 1, :] = jnp.mean(kj, axis=0, keepdims=True)
            kb_ref[j] = kj.astype(BF16)
            for pi in range(npair):
                vt_ref[pi, j] = v_ref[j * blk:(j + 1) * blk, pi * LANES:(pi + 1) * LANES].T.astype(BF16)

    lane_m = lax.broadcasted_iota(jnp.int32, (nb, LANES), 1)
    jidx = lax.broadcasted_iota(jnp.int32, (nb, blk), 0)
    valid = jidx < n
    row_d = lax.broadcasted_iota(jnp.int32, (LANES, blk), 0)
    lo_d = row_d < hd
    key_i = lax.broadcasted_iota(jnp.int32, (blk, blk), 0)
    qry_i = lax.broadcasted_iota(jnp.int32, (blk, blk), 1)
    causal = key_i <= qry_i
    qscale = (hd ** -0.5) * math.log2(math.e)

    def select_bias(km, q_t):
        gate = lax.dot_general(km, q_t, (((1,), (0,)), ((), ())), precision=lax.Precision.HIGHEST,
                               preferred_element_type=F32)
        gm = jnp.where(valid, gate, -jnp.inf)
        cnt = jnp.zeros((nb, blk), F32)
        for j2 in range(nb):
            gj = gm[j2:j2 + 1, :]
            better = (gj > gm) | ((gj == gm) & (j2 < jidx))
            cnt = cnt + jnp.where(better, 1.0, 0.0)
        sel = valid & (cnt < float(C_TOPK))
        return jnp.where(sel, 0.0, NEG_BIG)

    qt_heads = []
    for pi in range(npair):
        cs = slice(pi * LANES, (pi + 1) * LANES)
        q_t = q_ref[:, cs].T
        kmean = kmean_ref[:, cs]
        bias_ref[2 * pi] = select_bias(jnp.where(lane_m < hd, kmean, 0.0), q_t)
        bias_ref[2 * pi + 1] = select_bias(jnp.where(lane_m >= hd, kmean, 0.0), q_t)
        qs = q_t * qscale
        qt_heads.append((jnp.where(lo_d, qs, 0.0).astype(BF16), jnp.where(lo_d, 0.0, qs).astype(BF16)))

    def pair_block(pi, j, prev, own):
        kj = kb_ref[j, :, pi * LANES:(pi + 1) * LANES]
        s = [_nn(kj, qt) for qt in qt_heads[pi]]
        yield
        if own:
            s = [jnp.where(causal, x, -jnp.inf) for x in s]
        else:
            s = [x + bias_ref[2 * pi + h, pl.ds(j, 1), :] for h, x in enumerate(s)]
        smax = [jnp.max(x, axis=0, keepdims=True) for x in s]
        m_new = smax if own else [jnp.maximum(m, x) for m, x in zip(prev[0], smax)]
        p = [jnp.exp2(x - m) for x, m in zip(s, m_new)]
        psum = [jnp.sum(x, axis=0, keepdims=True) for x in p]
        vt = vt_ref[pi, j]
        pv = jnp.concatenate([_nn(vt[0:hd], p[0].astype(BF16)), _nn(vt[hd:], p[1].astype(BF16))], axis=0)
        yield
        if own:
            return m_new, psum, pv
        corr = [jnp.exp2(m - mn) for m, mn in zip(prev[0], m_new)]
        l_new = [l * c + x for l, c, x in zip(prev[1], corr, psum)]
        return m_new, l_new, prev[2] * jnp.where(lo_d, corr[0], corr[1]) + pv

    def visit(j, carry, own):
        return tuple(_round_robin([pair_block(pi, j, None if own else carry[pi], own) for pi in range(npair)]))

    res = lax.fori_loop(0, n, lambda j, carry: visit(j, carry, False), visit(n, None, True))
    for pi in range(npair):
        _, l, acc = res[pi]
        out_t = acc / jnp.where(lo_d, l[0], l[1])
        o_ref[:, pi * LANES:(pi + 1) * LANES] = out_t.T


MB_PAIRS_PER_STEP = 4


def _moba(z, bsz, seq, col_q, col_k, col_v):
    t = z.shape[0]
    nb = seq // C_BLOCK
    npp = MB_PAIRS_PER_STEP
    ngrp = (C_HEADS // 2) // npp
    width = npp * LANES
    return pl.pallas_call(
        _moba_kernel,
        grid=(bsz, ngrp, nb),
        in_specs=[pl.BlockSpec((C_BLOCK, width), lambda b, p, n: (b * nb + n, col_q // npp + p)),
                  pl.BlockSpec((seq, width), lambda b, p, n: (b, col_k // npp + p)),
                  pl.BlockSpec((seq, width), lambda b, p, n: (b, col_v // npp + p))],
        out_specs=pl.BlockSpec((C_BLOCK, width), lambda b, p, n: (b * nb + n, p)),
        out_shape=jax.ShapeDtypeStruct((t, C_HEADS * C_HEAD_DIM), F32),
        scratch_shapes=[pltpu.VMEM((nb, width), F32),
                        pltpu.VMEM((nb, C_BLOCK, width), BF16),
                        pltpu.VMEM((npp, nb, LANES, C_BLOCK), BF16),
                        pltpu.VMEM((2 * npp, nb, C_BLOCK), F32)],
        compiler_params=_params(("parallel", "parallel", "arbitrary")),
        name="moba",
    )(z, z, z)


def _rglru_kernel(x_ref, cw_ref, vec_ref, wa_ref, wx_ref, o_ref, ext_ref, a_ref, b_ref, h_ref):
    rows = x_ref.shape[0]
    width = x_ref.shape[1]

    @pl.when(pl.program_id(1) == 0)
    def _():
        ext_ref[0:SUBLANES, :] = jnp.zeros((SUBLANES, width), F32)
        h_ref[...] = jnp.zeros_like(h_ref)

    x = x_ref[...]
    ext_ref[SUBLANES:SUBLANES + rows, :] = x
    xc = vec_ref[0:1, :] + x * cw_ref[D_CONV - 1:D_CONV, :]
    for d in range(1, D_CONV):
        xc = xc + ext_ref[SUBLANES - d:SUBLANES - d + rows, :] * cw_ref[D_CONV - 1 - d:D_CONV - d, :]
    ext_ref[0:SUBLANES, :] = x[rows - SUBLANES:rows, :]

    ba = vec_ref[1:2, :]
    bx = vec_ref[2:3, :]
    lam = vec_ref[3:4, :]
    sp = jnp.maximum(-lam, 0.0) + jnp.log(1.0 + jnp.exp(-jnp.abs(lam)))
    for nblk in range(D_BLOCKS):
        cs = slice(nblk * D_BLOCK_DIM, (nblk + 1) * D_BLOCK_DIM)
        xb = xc[:, cs]
        xbb = xb.astype(BF16)
        rg = _sigmoid(_nn(xbb, wa_ref[nblk]) + ba[:, cs])
        ig = _sigmoid(_nn(xbb, wx_ref[nblk]) + bx[:, cs])
        log_a = -LRU_C * rg * sp[:, cs]
        th = jnp.tanh(log_a)
        one_minus_a2 = -2.0 * th / (1.0 - th)
        a_ref[:, cs] = jnp.exp(log_a)
        b_ref[:, cs] = jnp.sqrt(one_minus_a2) * (ig * xb)

    rowi = lax.broadcasted_iota(jnp.int32, (SUBLANES, width), 0)

    def group(gi, carry):
        sl = pl.ds(pl.multiple_of(gi * SUBLANES, SUBLANES), SUBLANES)
        a = a_ref[sl, :]
        b = b_ref[sl, :]
        for d in (1, 2, 4):
            keep = rowi >= d
            a_sh = jnp.where(keep, pltpu.roll(a, d, 0), 1.0)
            b_sh = jnp.where(keep, pltpu.roll(b, d, 0), 0.0)
            b = a * b_sh + b
            a = a * a_sh
        h = a * carry + b
        o_ref[sl, :] = h
        return jnp.broadcast_to(h[SUBLANES - 1:SUBLANES, :], (SUBLANES, width))

    h_ref[...] = lax.fori_loop(0, rows // SUBLANES, group, h_ref[...])


def _rglru(z, conv_w, vecs, wa, wx, bsz, seq, col_x):
    t = z.shape[0]
    width = D_BLOCKS * D_BLOCK_DIM
    nr = seq // ROWS
    full = lambda shape: pl.BlockSpec(shape, lambda b, r: (0,) * len(shape))
    return pl.pallas_call(
        _rglru_kernel,
        grid=(bsz, nr),
        in_specs=[pl.BlockSpec((ROWS, width), lambda b, r: (b * nr + r, col_x)),
                  full((D_CONV, width)), full((SUBLANES, width)),
                  full((D_BLOCKS, D_BLOCK_DIM, D_BLOCK_DIM)), full((D_BLOCKS, D_BLOCK_DIM, D_BLOCK_DIM))],
        out_specs=pl.BlockSpec((ROWS, width), lambda b, r: (b * nr + r, 0)),
        out_shape=jax.ShapeDtypeStruct((t, width), F32),
        scratch_shapes=[pltpu.VMEM((SUBLANES + ROWS, width), F32),
                        pltpu.VMEM((ROWS, width), F32),
                        pltpu.VMEM((ROWS, width), F32),
                        pltpu.VMEM((SUBLANES, width), F32)],
        compiler_params=_params(("parallel", "arbitrary")),
        name="rglru",
    )(z, conv_w, vecs, wa, wx)


def _outproj_kernel(*refs, nbr, final):
    br = refs[:nbr]
    gt = refs[nbr:2 * nbr]
    ws = refs[2 * nbr:3 * nbr]
    x_ref, p_ref, pp_ref, pg_ref, pn_ref = refs[3 * nbr:3 * nbr + 5]
    rest = refs[3 * nbr + 5:]
    if final:
        fn_ref, o_ref = rest
    else:
        (o_ref,) = rest

    acc = x_ref[...]
    for b, g, w in zip(br, gt, ws):
        gv = g[...]
        y = b[...] * (gv * _sigmoid(gv))
        acc = acc + _nn(y.astype(BF16), w[...])
    ms = jnp.mean(acc * acc, axis=-1, keepdims=True)
    hn = (acc * lax.rsqrt(ms + EPS) * pn_ref[...]).astype(BF16)
    gate = _sigmoid(_nn(hn, pg_ref[...]))
    out = acc + _nn(p_ref[...].astype(BF16), pp_ref[...]) * gate
    if final:
        ms2 = jnp.mean(out * out, axis=-1, keepdims=True)
        out = out * lax.rsqrt(ms2 + EPS) * fn_ref[...]
    o_ref[...] = out


def _outproj(branches, z, gate_cols, w_parts, x, p_i, ple_proj, ple_gate, ple_norm, final_norm, tm):
    t, d = x.shape
    nbr = len(branches)
    in_specs, args = [], []
    for b in branches:
        in_specs.append(pl.BlockSpec((tm, b.shape[1]), lambda i: (i, 0)))
        args.append(b)
    for b, gc in zip(branches, gate_cols):
        in_specs.append(pl.BlockSpec((tm, b.shape[1]), lambda i, gc=gc: (i, gc)))
        args.append(z)
    for w in w_parts:
        in_specs.append(pl.BlockSpec(w.shape, lambda i: (0, 0)))
        args.append(w)
    in_specs += [pl.BlockSpec((tm, d), lambda i: (i, 0)),
                 pl.BlockSpec((tm, P_DIM), lambda i: (i, 0)),
                 pl.BlockSpec((P_DIM, d), lambda i: (0, 0)),
                 pl.BlockSpec((d, d), lambda i: (0, 0)),
                 pl.BlockSpec((1, d), lambda i: (0, 0))]
    args += [x, p_i, ple_proj, ple_gate, ple_norm.reshape(1, d)]
    final = final_norm is not None
    if final:
        in_specs.append(pl.BlockSpec((1, d), lambda i: (0, 0)))
        args.append(final_norm.reshape(1, d))
    return pl.pallas_call(
        functools.partial(_outproj_kernel, nbr=nbr, final=final),
        grid=(t // tm,),
        in_specs=in_specs,
        out_specs=pl.BlockSpec((tm, d), lambda i: (i, 0)),
        out_shape=jax.ShapeDtypeStruct((t, d), F32),
        compiler_params=_params(("parallel",)),
        name="outproj",
    )(*args)


def _even_layer(x, bsz, seq, norm_g, w_in, w_out, lb, hg_norm, mu, w0, w2, a0, a2, k_k, k_a, r_k, ln_g, ln_b,
                p_i, ple_proj, ple_gate, ple_norm, final_norm):
    aw = A_HEADS * A_HEAD_DIM
    bw = B_HEADS * B_HEAD_DIM
    rw0 = 3 * aw
    lr0 = rw0 + 3 * bw
    g0 = lr0 + 2 * B_RANK
    w_perm = jnp.concatenate([w_in[:, :lr0], w_in[:, g0:], w_in[:, lr0:g0]], axis=1).astype(BF16)
    z = _inproj(x, norm_g, w_perm, tm=1024, tn=w_perm.shape[1] // 5)

    zero = jnp.zeros_like(lb)
    hpar = jnp.stack([jnp.log(lb), jnp.log1p(-lb), 1.0 - lb, hg_norm, zero, zero, zero, zero])
    oa = _hgrn2(z, hpar, bsz, seq)

    zrow = jnp.zeros((bw,), F32)
    rows = [mu[0:bw], mu[bw:2 * bw], mu[2 * bw:3 * bw], w0, a0, k_k, k_a, r_k, ln_g, ln_b]
    rpar = jnp.stack(rows + [zrow] * (_RW_NPAR - len(rows)))
    muwa = mu[3 * bw:].reshape(1, 2 * B_RANK)
    npair = B_HEADS // 2
    zpad = jnp.zeros((npair, B_RANK, LANES), F32)
    w2p = jnp.concatenate([w2.reshape(B_RANK, npair, LANES).transpose(1, 0, 2), zpad], axis=1).astype(BF16)
    a2p = jnp.concatenate([zpad, a2.reshape(B_RANK, npair, LANES).transpose(1, 0, 2)], axis=1).astype(BF16)
    cb = lambda c: c // LANES
    ob = _rwkv(z, rpar, muwa, w2p, a2p, bsz, seq, cb(rw0), cb(rw0 + bw), cb(rw0 + 2 * bw), cb(lr0 + aw + bw))

    gate0 = lr0
    w_out_b = w_out.astype(BF16)
    return _outproj([oa, ob], z, [gate0 // aw, (gate0 + aw) // bw], [w_out_b[:aw], w_out_b[aw:]],
                    x, p_i, ple_proj.astype(BF16), ple_gate.astype(BF16), ple_norm, final_norm, tm=256)


def _odd_layer(x, bsz, seq, norm_g, w_in, w_out, conv_w, conv_b, wa, ba, wx, bx, lam,
               p_i, ple_proj, ple_gate, ple_norm, final_norm):
    cw = C_HEADS * C_HEAD_DIM
    dw = D_BLOCKS * D_BLOCK_DIM
    x0 = 3 * cw
    g0 = x0 + dw
    w_perm = jnp.concatenate([w_in[:, :x0], w_in[:, g0:g0 + cw], w_in[:, x0:g0], w_in[:, g0 + cw:]],
                             axis=1).astype(BF16)
    z = _inproj(x, norm_g, w_perm, tm=1024, tn=w_perm.shape[1] // 4)

    cb = lambda c: c // LANES
    oc = _moba(z, bsz, seq, cb(0), cb(cw), cb(2 * cw))
    zero = jnp.zeros_like(lam)
    vecs = jnp.stack([conv_b, ba, bx, lam, zero, zero, zero, zero])
    od = _rglru(z, conv_w, vecs, wa.astype(BF16), wx.astype(BF16), bsz, seq, (x0 + cw) // dw)

    w_out_b = w_out.astype(BF16)
    return _outproj([oc, od], z, [x0 // cw, (x0 + cw + dw) // dw], [w_out_b[:cw], w_out_b[cw:]],
                    x, p_i, ple_proj.astype(BF16), ple_gate.astype(BF16), ple_norm, final_norm, tm=256)


def kernel(x, p, ev_norm, ev_w_in, ev_w_out, hg_lb_logits, hg_norm, rw_mu, rw_w0, rw_w2, rw_a0, rw_a2, rw_k_k, rw_k_a, rw_r_k, rw_ln_g, rw_ln_b, od_norm, od_w_in, od_w_out, lru_conv_w, lru_conv_b, lru_wa, lru_ba, lru_wx, lru_bx, lru_lambda, ple_proj, ple_gate, ple_norm, final_norm):
    bsz, seq, d = x.shape
    depth = p.shape[0]
    s = jax.nn.softmax(hg_lb_logits.astype(F32), axis=0)
    lower_bounds = jnp.maximum(jnp.cumsum(s, axis=0) - s[0], 0.0)
    xf = x.reshape(bsz * seq, d)
    pf = p.reshape(depth, bsz * seq, p.shape[-1])
    for i in range(depth):
        j = i // 2
        fin = final_norm if i == depth - 1 else None
        if i % 2 == 0:
            xf = _even_layer(xf, bsz, seq, ev_norm[j], ev_w_in[j], ev_w_out[j], lower_bounds[j], hg_norm[j],
                             rw_mu[j], rw_w0[j], rw_w2[j], rw_a0[j], rw_a2[j], rw_k_k[j], rw_k_a[j], rw_r_k[j],
                             rw_ln_g[j], rw_ln_b[j], pf[i], ple_proj[i], ple_gate[i], ple_norm[i], fin)
        else:
            xf = _odd_layer(xf, bsz, seq, od_norm[j], od_w_in[j], od_w_out[j], lru_conv_w[j], lru_conv_b[j],
                            lru_wa[j], lru_ba[j], lru_wx[j], lru_bx[j], lru_lambda[j],
                            pf[i], ple_proj[i], ple_gate[i], ple_norm[i], fin)
    return xf.reshape(bsz, seq, d)
```

```python
import functools
import math

import jax
import jax.numpy as jnp
from jax import lax
from jax.experimental import pallas as pl
from jax.experimental.pallas import tpu as pltpu

F32 = jnp.float32
BF16 = jnp.bfloat16

EPS = 1e-6
LANES = 128
SUBLANES = 8
VMEM_LIMIT = 56 * 1024 * 1024

A_HEADS, A_HEAD_DIM = 8, 128
B_HEADS, B_HEAD_DIM = 16, 64
B_RANK = 64
B_LN_EPS = 64e-5
C_HEADS, C_HEAD_DIM, C_BLOCK, C_TOPK = 8, 64, 256, 3
D_BLOCKS, D_BLOCK_DIM, D_CONV = 8, 128, 4
LRU_C = 8.0
P_DIM = 256

OUTPROJ_ROWS = 512
CHUNK = 64
ROWS = 256
NEG_BIG = -16384.0


def _nt(a, b):
    return lax.dot_general(a, b, (((1,), (1,)), ((), ())), preferred_element_type=F32)


def _tn(a, b):
    return lax.dot_general(a, b, (((0,), (0,)), ((), ())), preferred_element_type=F32)


def _nn(a, b):
    return jnp.dot(a, b, preferred_element_type=F32)


def _split3(x):
    h = x.astype(BF16)
    r = x - h.astype(F32)
    m = r.astype(BF16)
    l = (r - m.astype(F32)).astype(BF16)
    return h, m, l


def _nn_exact_lhs(lhs_bf16, x):
    h, m, l = _split3(x)
    return _nn(lhs_bf16, h) + _nn(lhs_bf16, m) + _nn(lhs_bf16, l)


def _nn_exact_rhs(x, rhs_bf16):
    h, m, l = _split3(x)
    return _nn(h, rhs_bf16) + _nn(m, rhs_bf16) + _nn(l, rhs_bf16)


def _sigmoid(x):
    return 0.5 * jnp.tanh(0.5 * x) + 0.5


def _round_robin(gens, starts=None):
    starts = starts or [0] * len(gens)
    results = [None] * len(gens)
    done = [False] * len(gens)
    rnd = 0
    while not all(done):
        for i in range(len(gens)):
            if done[i] or rnd < starts[i]:
                continue
            try:
                next(gens[i])
            except StopIteration as stop:
                results[i] = stop.value
                done[i] = True
        rnd += 1
    return results


def _params(sem):
    return pltpu.CompilerParams(dimension_semantics=sem, vmem_limit_bytes=VMEM_LIMIT)


def _inproj_kernel(x_ref, g_ref, w_ref, o_ref, h_ref):
    @pl.when(pl.program_id(1) == 0)
    def _():
        x = x_ref[...]
        ms = jnp.mean(x * x, axis=-1, keepdims=True)
        h_ref[...] = (x * lax.rsqrt(ms + EPS) * g_ref[...]).astype(BF16)

    o_ref[...] = _nn(h_ref[...], w_ref[...])


def _inproj(x, g, w, tm, tn):
    t, d = x.shape
    n = w.shape[1]
    return pl.pallas_call(
        _inproj_kernel,
        grid=(t // tm, n // tn),
        in_specs=[
            pl.BlockSpec((tm, d), lambda i, j: (i, 0)),
            pl.BlockSpec((1, d), lambda i, j: (0, 0)),
            pl.BlockSpec((d, tn), lambda i, j: (0, j)),
        ],
        out_specs=pl.BlockSpec((tm, tn), lambda i, j: (i, j)),
        out_shape=jax.ShapeDtypeStruct((t, n), F32),
        scratch_shapes=[pltpu.VMEM((tm, d), BF16)],
        compiler_params=_params(("parallel", "arbitrary")),
        name="inproj",
    )(x, g.reshape(1, d), w)


_LEVELS = (1, 2, 4, 8, 16, 32)


def _boundary_bcast(g, m):
    c, w = g.shape
    if m >= SUBLANES:
        gr = g.reshape(c // (2 * m), 2 * m, w)
        b = jnp.broadcast_to(gr[:, m - 1:m, :], gr.shape)
        return b.reshape(c, w)
    gr = g.reshape(c // SUBLANES, SUBLANES, w)
    j = lax.broadcasted_iota(jnp.int32, gr.shape, 1)
    rows = [jnp.broadcast_to(gr[:, r:r + 1, :], gr.shape) for r in range(m - 1, SUBLANES, 2 * m)]
    b = rows[-1]
    for idx in range(len(rows) - 2, -1, -1):
        b = jnp.where(j < (idx + 1) * 2 * m, rows[idx], b)
    return b.reshape(c, w)


def _hgrn2_chains(zq_ref, zf_ref, zi_ref, par_ref, o_ref, st_ref, first_block):
    c = CHUNK

    @pl.when(first_block)
    def _():
        st_ref[...] = jnp.zeros_like(st_ref)

    row = lax.broadcasted_iota(jnp.int32, (c, c), 0)
    col = lax.broadcasted_iota(jnp.int32, (c, c), 1)
    ltri = (col <= row).astype(BF16)
    ltri2 = jnp.concatenate([ltri, ltri], axis=1)
    eye = col == row
    lvl_masks = []
    for m in _LEVELS:
        sh = int(math.log2(2 * m))
        same = (row >> sh) == (col >> sh)
        lvl_masks.append(same & ((row & (2 * m - 1)) >= m) & ((col & (2 * m - 1)) < m))
    rowi = lax.broadcasted_iota(jnp.int32, (c, LANES), 0)

    nseq, nheads = st_ref.shape[0], st_ref.shape[1]
    par_all = par_ref[...]

    def head_chunk(hi, zq, zf, v, st):
        cs = slice(hi * LANES, (hi + 1) * LANES)
        log_lb = par_all[0:1, cs]
        log_omlb = par_all[1:2, cs]
        omlb = par_all[2:3, cs]
        gain = par_all[3:4, cs]

        e = jnp.exp(-jnp.abs(zf))
        log_sig = jnp.minimum(zf, 0.0) - jnp.log(1.0 + e)
        sig_neg = _sigmoid(-zf)
        bterm = log_omlb + log_sig
        mx = jnp.maximum(log_lb, bterm)
        log_f = mx + jnp.log(1.0 + jnp.exp(-jnp.abs(log_lb - bterm)))
        k = omlb * sig_neg
        q = zq * _sigmoid(zq)

        lf_h = log_f.astype(BF16)
        lf_l = (log_f - lf_h.astype(F32)).astype(BF16)
        g = _nn(ltri2, jnp.concatenate([lf_h, lf_l], axis=0))
        yield
        g_last = g[c - 1:c, :]

        qb = q.astype(BF16)
        kb = k.astype(BF16)
        vb = v.astype(BF16)
        attn = jnp.where(eye, _nt(qb, kb), 0.0)
        for m, msk in zip(_LEVELS, lvl_masks):
            second = (rowi & (2 * m - 1)) >= m
            if m == 1:
                fac = jnp.where(second, jnp.exp(log_f), 1.0)
            else:
                fac = jnp.exp(-jnp.abs(g - _boundary_bcast(g, m)))
            x = (jnp.where(second, q, k) * fac).astype(BF16)
            attn = attn + jnp.where(msk, _nt(x, x), 0.0)
        yield

        o = _nn(attn.astype(BF16), vb) + _nt((q * jnp.exp(g)).astype(BF16), st.astype(BF16))
        khat = (k * jnp.exp(g_last - g)).astype(BF16)
        st_new = st * jnp.exp(g_last) + _tn(vb, khat)
        yield

        ms = jnp.mean(o * o, axis=-1, keepdims=True)
        return o * lax.rsqrt(ms + EPS) * gain, st_new

    def chains(ci):
        sl = pl.ds(pl.multiple_of(ci * c, c), c)
        out = []
        for bi in range(nseq):
            for hi in range(nheads):
                cs = slice(hi * LANES, (hi + 1) * LANES)

                def store(res, bi=bi, hi=hi, cs=cs):
                    o_ref[bi, sl, cs] = res[0]
                    st_ref[bi, hi] = res[1]

                out.append((head_chunk(hi, zq_ref[bi, sl, cs], zf_ref[bi, sl, cs], zi_ref[bi, sl, cs],
                                       st_ref[bi, hi]), store))
        return out

    return chains


_EXP_M05 = math.exp(-0.5)
(_P_MU_R, _P_MU_K, _P_MU_V, _P_W0, _P_A0, _P_KK, _P_KA, _P_RK, _P_LNG, _P_LNB) = range(10)
_RW_NPAR = 16


def _rwkv_chains(zr_ref, zk_ref, zv_ref, zwa_ref, par_ref, muwa_ref, w2_ref, a2_ref, o_ref,
                 st_ref, ext_ref, extwa_ref, first_block):
    c = CHUNK
    hd = B_HEAD_DIM
    nbat, npair = st_ref.shape[0], st_ref.shape[1]

    @pl.when(first_block)
    def _():
        st_ref[...] = jnp.zeros_like(st_ref)
        ext_ref[:, :, 0:SUBLANES, :] = jnp.zeros((nbat, 3, SUBLANES, npair * LANES), F32)
        extwa_ref[:, 0:SUBLANES, :] = jnp.zeros((nbat, SUBLANES, LANES), F32)

    par_all = par_ref[...]

    @pl.when(jnp.logical_not(first_block))
    def _():
        ext_ref[:, :, 0:SUBLANES, :] = ext_ref[:, :, ROWS:SUBLANES + ROWS, :]
        extwa_ref[:, 0:SUBLANES, :] = extwa_ref[:, ROWS:SUBLANES + ROWS, :]

    for bi in range(nbat):
        ext_ref[bi, 0, SUBLANES:SUBLANES + ROWS, :] = zr_ref[bi]
        ext_ref[bi, 1, SUBLANES:SUBLANES + ROWS, :] = zk_ref[bi]
        ext_ref[bi, 2, SUBLANES:SUBLANES + ROWS, :] = zv_ref[bi]
        extwa_ref[bi, SUBLANES:SUBLANES + ROWS, :] = zwa_ref[bi]

    def mixed(ext, ci, cs, mu):
        win = ext[pl.ds(pl.multiple_of(ci * c, c), c + SUBLANES), cs]
        cur = win[SUBLANES:SUBLANES + c]
        prev = win[SUBLANES - 1:SUBLANES - 1 + c]
        return cur + (prev - cur) * mu

    lane = lax.broadcasted_iota(jnp.int32, (c, LANES), 1)
    lo_l = lane < hd
    lane2 = lax.broadcasted_iota(jnp.int32, (2 * c, LANES), 1)
    row2 = lax.broadcasted_iota(jnp.int32, (2 * c, LANES), 0)
    s_idx = lane2 & (c - 1)
    t_idx = row2 & (c - 1)
    tri = (s_idx < t_idx) | ((row2 >= c) & (s_idx == t_idx))
    eye_s = ((lane & (c - 1)) == lax.broadcasted_iota(jnp.int32, (c, LANES), 0)).astype(F32)
    ltri = (lax.broadcasted_iota(jnp.int32, (c, c), 1)
            <= lax.broadcasted_iota(jnp.int32, (c, c), 0)).astype(BF16)
    ltri2 = jnp.concatenate([ltri, ltri], axis=1)
    seg = ((lax.broadcasted_iota(jnp.int32, (LANES, LANES), 0) >> int(math.log2(hd)))
           == (lax.broadcasted_iota(jnp.int32, (LANES, LANES), 1) >> int(math.log2(hd))))
    seg_b = seg.astype(BF16)
    seg2 = jnp.concatenate([seg_b, seg_b], axis=0)

    def split2(x):
        h = x.astype(BF16)
        return h, (x - h.astype(F32)).astype(BF16)

    def segsum(x):
        return _nn(jnp.concatenate(split2(x), axis=1), seg2)

    def by_head(x):
        zero = jnp.zeros_like(x)
        return jnp.concatenate([jnp.where(lo_l, x, zero), jnp.where(lo_l, zero, x)], axis=0)

    def pair_chunk(pi, r, kraw, v, st, w2, a2, wab, twab):
        cs = slice(pi * LANES, (pi + 1) * LANES)
        prow = lambda i: par_all[i:i + 1, cs]

        xw = prow(_P_W0) + _nn(twab, w2)
        g = -_EXP_M05 * _sigmoid(xw)
        alpha = _sigmoid(prow(_P_A0) + _nn(wab, a2))
        yield
        kk = kraw * prow(_P_KK)
        k = kraw * (1.0 + (alpha - 1.0) * prow(_P_KA))
        sums = segsum(jnp.concatenate([kk * kk, r * k * prow(_P_RK)], axis=0))
        gc = _nn(ltri2, jnp.concatenate(split2(g), axis=0))
        yield
        kk = kk / jnp.maximum(jnp.sqrt(sums[0:c]), 1e-12)
        rk_sum = sums[c:2 * c]
        bv = kk * alpha
        g_last = gc[c - 1:c, :]
        e_pos = jnp.exp(gc)
        e_neg = jnp.exp(-gc)
        e_last = jnp.exp(g_last)
        at = -kk * jnp.exp(gc - g)
        rt = r * e_pos
        bt = bv * e_neg
        kt = k * e_neg
        bh = bt * e_last
        kh = kt * e_last

        atb = at.astype(BF16)
        rtb = rt.astype(BF16)
        lhs = jnp.concatenate([atb, rtb], axis=0)
        ab_rb = jnp.where(tri, _nt(lhs, by_head(bt.astype(BF16))), 0.0)
        ak_rk = jnp.where(tri, _nt(lhs, by_head(kt.astype(BF16))), 0.0)
        yield

        stb = st.astype(BF16)
        vh = by_head(v.astype(BF16))
        w = _nt(atb, stb) + _nn(ak_rk[0:c].astype(BF16), vh)
        pw = ab_rb[0:c]
        t_inv = eye_s + pw
        for _ in range(int(math.log2(c)) - 1):
            pwb = pw.astype(BF16)
            pw = _nn(pwb, by_head(pwb))
            yield
            t_inv = t_inv + _nn(pw.astype(BF16), by_head(t_inv.astype(BF16)))
        yield

        u = _nn(t_inv.astype(BF16), by_head(w.astype(BF16)))
        yield
        ub = u.astype(BF16)
        a_r = jnp.concatenate([ab_rb[c:2 * c], ak_rk[c:2 * c]], axis=1).astype(BF16)
        y = _nt(rtb, stb) + _nn(a_r, jnp.concatenate([by_head(ub), vh], axis=0))

        st_new = st * e_last + _tn(jnp.concatenate([ub, v.astype(BF16)], axis=0),
                                   jnp.concatenate([bh.astype(BF16), kh.astype(BF16)], axis=0))
        st_new = jnp.where(seg, st_new, 0.0)
        yield

        inv_n = 1.0 / hd
        moments = segsum(jnp.concatenate([y, y * y], axis=0)) * inv_n
        yield
        mu = moments[0:c]
        var = moments[c:2 * c] - mu * mu
        yn = (y - mu) * lax.rsqrt(var + B_LN_EPS) * prow(_P_LNG) + prow(_P_LNB)
        yn = yn + rk_sum * v
        return yn, st_new

    def chains(ci):
        out = []
        for bi in range(nbat):
            wa = mixed(extwa_ref.at[bi], ci, slice(None), muwa_ref[...])
            wab = wa.astype(BF16)
            twab = jnp.tanh(wa).astype(BF16)
            for pi in range(npair):
                cs = slice(pi * LANES, (pi + 1) * LANES)
                mu_of = lambda i: par_all[i:i + 1, cs]

                def store(res, bi=bi, pi=pi, cs=cs):
                    o_ref[bi, pl.ds(pl.multiple_of(ci * c, c), c), cs] = res[0]
                    st_ref[bi, pi] = res[1]

                out.append((pair_chunk(pi, mixed(ext_ref.at[bi, 0], ci, cs, mu_of(_P_MU_R)),
                                       mixed(ext_ref.at[bi, 1], ci, cs, mu_of(_P_MU_K)),
                                       mixed(ext_ref.at[bi, 2], ci, cs, mu_of(_P_MU_V)),
                                       st_ref[bi, pi], w2_ref[pi], a2_ref[pi], wab, twab), store))
        return out

    return chains


EV_SEQS_PER_STEP = 2
HG_STAGES = 3
RW_STAGES = 13


def _even_mixer_kernel(zq_ref, zf_ref, zi_ref, zr_ref, zk_ref, zv_ref, zwa_ref, hpar_ref, rpar_ref, muwa_ref,
                       w2_ref, a2_ref, oa_ref, ob_ref, hst_ref, rst_ref, ext_ref, extwa_ref):
    first = pl.program_id(1) == 0
    hg = _hgrn2_chains(zq_ref, zf_ref, zi_ref, hpar_ref, oa_ref, hst_ref, first)
    rw = _rwkv_chains(zr_ref, zk_ref, zv_ref, zwa_ref, rpar_ref, muwa_ref, w2_ref, a2_ref, ob_ref,
                      rst_ref, ext_ref, extwa_ref, first)

    def chunk(ci, carry):
        hg_chains = hg(ci)
        rw_chains = rw(ci)
        span = max(RW_STAGES - HG_STAGES, 0)
        starts = [(i * span) // max(len(hg_chains) - 1, 1) for i in range(len(hg_chains))] + [0] * len(rw_chains)
        both = hg_chains + rw_chains
        for (_, store), res in zip(both, _round_robin([g for g, _ in both], starts)):
            store(res)
        return carry

    lax.fori_loop(0, ROWS // CHUNK, chunk, 0)


def _even_mixer(z, hpar, rpar, muwa, w2p, a2p, bsz, seq, col_rkv, col_wa):
    t, n = z.shape
    nr = seq // ROWS
    nsq = EV_SEQS_PER_STEP
    aw = A_HEADS * A_HEAD_DIM
    bw = B_HEADS * B_HEAD_DIM
    npair = B_HEADS // 2
    z3 = z.reshape(bsz, seq, n)
    wide = lambda width, col: pl.BlockSpec((nsq, ROWS, width), lambda b, r: (b, r, col // width))
    const = lambda shape: pl.BlockSpec(shape, lambda b, r: (0,) * len(shape))
    oa, ob = pl.pallas_call(
        _even_mixer_kernel,
        grid=(bsz // nsq, nr),
        in_specs=[wide(aw, 0), wide(aw, aw), wide(aw, 2 * aw),
                  wide(bw, col_rkv), wide(bw, col_rkv + bw), wide(bw, col_rkv + 2 * bw),
                  wide(LANES, col_wa),
                  const((SUBLANES, aw)), const((_RW_NPAR, bw)), const((1, LANES)),
                  const((npair, LANES, LANES)), const((npair, LANES, LANES))],
        out_specs=[pl.BlockSpec((nsq, ROWS, aw), lambda b, r: (b, r, 0)),
                   pl.BlockSpec((nsq, ROWS, bw), lambda b, r: (b, r, 0))],
        out_shape=[jax.ShapeDtypeStruct((bsz, seq, aw), F32), jax.ShapeDtypeStruct((bsz, seq, bw), F32)],
        scratch_shapes=[pltpu.VMEM((nsq, A_HEADS, A_HEAD_DIM, A_HEAD_DIM), F32),
                        pltpu.VMEM((nsq, npair, LANES, LANES), F32),
                        pltpu.VMEM((nsq, 3, SUBLANES + ROWS, bw), F32),
                        pltpu.VMEM((nsq, SUBLANES + ROWS, LANES), F32)],
        compiler_params=_params(("parallel", "arbitrary")),
        name="even_mixer",
    )(z3, z3, z3, z3, z3, z3, z3, hpar, rpar, muwa, w2p, a2p)
    return oa.reshape(t, aw), ob.reshape(t, bw)


def _moba_kernel(q_ref, k_ref, v_ref, o_ref, kmean_ref, kb_ref, vt_ref, bias_ref):
    nb = kmean_ref.shape[0]
    blk = C_BLOCK
    hd = C_HEAD_DIM
    n = pl.program_id(2)
    npair = q_ref.shape[1] // LANES

    @pl.when(n == 0)
    def _():
        for j in range(nb):
            kj = k_ref[j * blk:(j + 1) * blk, :]
            kmean_ref[j:j + 1, :] = jnp.mean(kj, axis=0, keepdims=True)
            kb_ref[j] = kj.astype(BF16)
            for pi in range(npair):
                vt_ref[pi, j] = v_ref[j * blk:(j + 1) * blk, pi * LANES:(pi + 1) * LANES].T.astype(BF16)

    lane_m = lax.broadcasted_iota(jnp.int32, (nb, LANES), 1)
    jidx = lax.broadcasted_iota(jnp.int32, (nb, blk), 0)
    valid = jidx < n
    row_d = lax.broadcasted_iota(jnp.int32, (LANES, blk), 0)
    lo_d = row_d < hd
    key_i = lax.broadcasted_iota(jnp.int32, (blk, blk), 0)
    qry_i = lax.broadcasted_iota(jnp.int32, (blk, blk), 1)
    causal = key_i <= qry_i
    qscale = (hd ** -0.5) * math.log2(math.e)

    def select_bias(km, q_t):
        gate = lax.dot_general(km, q_t, (((1,), (0,)), ((), ())), precision=lax.Precision.HIGHEST,
                               preferred_element_type=F32)
        gm = jnp.where(valid, gate, -jnp.inf)
        cnt = jnp.zeros((nb, blk), F32)
        for j2 in range(nb):
            gj = gm[j2:j2 + 1, :]
            better = (gj > gm) | ((gj == gm) & (j2 < jidx))
            cnt = cnt + jnp.where(better, 1.0, 0.0)
        sel = valid & (cnt < float(C_TOPK))
        return jnp.where(sel, 0.0, NEG_BIG)

    qt_heads = []
    for pi in range(npair):
        cs = slice(pi * LANES, (pi + 1) * LANES)
        q_t = q_ref[:, cs].T
        kmean = kmean_ref[:, cs]
        bias_ref[2 * pi] = select_bias(jnp.where(lane_m < hd, kmean, 0.0), q_t)
        bias_ref[2 * pi + 1] = select_bias(jnp.where(lane_m >= hd, kmean, 0.0), q_t)
        qs = q_t * qscale
        qt_heads.append((jnp.where(lo_d, qs, 0.0).astype(BF16), jnp.where(lo_d, 0.0, qs).astype(BF16)))

    nq = blk // MB_QUERY_TILE
    lo_q = lo_d[:, 0:MB_QUERY_TILE]

    def pair_block(pi, qi, j, prev, own):
        qs = slice(qi * MB_QUERY_TILE, (qi + 1) * MB_QUERY_TILE)
        kj = kb_ref[j, :, pi * LANES:(pi + 1) * LANES]
        s = [_nn(kj, qt[:, qs]) for qt in qt_heads[pi]]
        yield
        if own:
            s = [jnp.where(causal[:, qs], x, -jnp.inf) for x in s]
        else:
            s = [x + bias_ref[2 * pi + h, pl.ds(j, 1), :][:, qs] for h, x in enumerate(s)]
        smax = [jnp.max(x, axis=0, keepdims=True) for x in s]
        m_new = smax if own else [jnp.maximum(m, x) for m, x in zip(prev[0], smax)]
        p = [jnp.exp2(x - m) for x, m in zip(s, m_new)]
        psum = [jnp.sum(x, axis=0, keepdims=True) for x in p]
        vt = vt_ref[pi, j]
        pv = jnp.concatenate([_nn(vt[0:hd], p[0].astype(BF16)), _nn(vt[hd:], p[1].astype(BF16))], axis=0)
        yield
        if own:
            return m_new, psum, pv
        corr = [jnp.exp2(m - mn) for m, mn in zip(prev[0], m_new)]
        l_new = [l * c + x for l, c, x in zip(prev[1], corr, psum)]
        return m_new, l_new, prev[2] * jnp.where(lo_q, corr[0], corr[1]) + pv

    def visit(j, carry, own):
        gens = [pair_block(i // nq, i % nq, j, None if own else carry[i], own) for i in range(npair * nq)]
        return tuple(_round_robin(gens))

    res = lax.fori_loop(0, n, lambda j, carry: visit(j, carry, False), visit(n, None, True))
    for pi in range(npair):
        tiles = []
        for qi in range(nq):
            _, l, acc = res[pi * nq + qi]
            tiles.append(acc / jnp.where(lo_q, l[0], l[1]))
        out_t = jnp.concatenate(tiles, axis=1)
        o_ref[:, pi * LANES:(pi + 1) * LANES] = out_t.T


MB_PAIRS_PER_STEP = 4
MB_QUERY_TILE = 256


def _moba(z, bsz, seq, col_q, col_k, col_v):
    t = z.shape[0]
    nb = seq // C_BLOCK
    npp = MB_PAIRS_PER_STEP
    ngrp = (C_HEADS // 2) // npp
    width = npp * LANES
    return pl.pallas_call(
        _moba_kernel,
        grid=(bsz, ngrp, nb),
        in_specs=[pl.BlockSpec((C_BLOCK, width), lambda b, p, n: (b * nb + n, col_q // npp + p)),
                  pl.BlockSpec((seq, width), lambda b, p, n: (b, col_k // npp + p)),
                  pl.BlockSpec((seq, width), lambda b, p, n: (b, col_v // npp + p))],
        out_specs=pl.BlockSpec((C_BLOCK, width), lambda b, p, n: (b * nb + n, p)),
        out_shape=jax.ShapeDtypeStruct((t, C_HEADS * C_HEAD_DIM), F32),
        scratch_shapes=[pltpu.VMEM((nb, width), F32),
                        pltpu.VMEM((nb, C_BLOCK, width), BF16),
                        pltpu.VMEM((npp, nb, LANES, C_BLOCK), BF16),
                        pltpu.VMEM((2 * npp, nb, C_BLOCK), F32)],
        compiler_params=_params(("parallel", "parallel", "arbitrary")),
        name="moba",
    )(z, z, z)


def _rglru_kernel(x_ref, cw_ref, vec_ref, wa_ref, wx_ref, o_ref, ext_ref, a_ref, b_ref, h_ref):
    rows = x_ref.shape[0]
    width = x_ref.shape[1]

    @pl.when(pl.program_id(1) == 0)
    def _():
        ext_ref[0:SUBLANES, :] = jnp.zeros((SUBLANES, width), F32)
        h_ref[...] = jnp.zeros_like(h_ref)

    x = x_ref[...]
    ext_ref[SUBLANES:SUBLANES + rows, :] = x
    xc = vec_ref[0:1, :] + x * cw_ref[D_CONV - 1:D_CONV, :]
    for d in range(1, D_CONV):
        xc = xc + ext_ref[SUBLANES - d:SUBLANES - d + rows, :] * cw_ref[D_CONV - 1 - d:D_CONV - d, :]
    ext_ref[0:SUBLANES, :] = x[rows - SUBLANES:rows, :]

    ba = vec_ref[1:2, :]
    bx = vec_ref[2:3, :]
    lam = vec_ref[3:4, :]
    sp = jnp.maximum(-lam, 0.0) + jnp.log(1.0 + jnp.exp(-jnp.abs(lam)))
    for nblk in range(D_BLOCKS):
        cs = slice(nblk * D_BLOCK_DIM, (nblk + 1) * D_BLOCK_DIM)
        xb = xc[:, cs]
        xbb = xb.astype(BF16)
        rg = _sigmoid(_nn(xbb, wa_ref[nblk]) + ba[:, cs])
        ig = _sigmoid(_nn(xbb, wx_ref[nblk]) + bx[:, cs])
        log_a = -LRU_C * rg * sp[:, cs]
        th = jnp.tanh(log_a)
        one_minus_a2 = -2.0 * th / (1.0 - th)
        a_ref[:, cs] = jnp.exp(log_a)
        b_ref[:, cs] = jnp.sqrt(one_minus_a2) * (ig * xb)

    rowi = lax.broadcasted_iota(jnp.int32, (SUBLANES, width), 0)

    def group(gi, carry):
        sl = pl.ds(pl.multiple_of(gi * SUBLANES, SUBLANES), SUBLANES)
        a = a_ref[sl, :]
        b = b_ref[sl, :]
        for d in (1, 2, 4):
            keep = rowi >= d
            a_sh = jnp.where(keep, pltpu.roll(a, d, 0), 1.0)
            b_sh = jnp.where(keep, pltpu.roll(b, d, 0), 0.0)
            b = a * b_sh + b
            a = a * a_sh
        h = a * carry + b
        o_ref[sl, :] = h
        return jnp.broadcast_to(h[SUBLANES - 1:SUBLANES, :], (SUBLANES, width))

    h_ref[...] = lax.fori_loop(0, rows // SUBLANES, group, h_ref[...])


def _rglru(z, conv_w, vecs, wa, wx, bsz, seq, col_x):
    t = z.shape[0]
    width = D_BLOCKS * D_BLOCK_DIM
    nr = seq // ROWS
    full = lambda shape: pl.BlockSpec(shape, lambda b, r: (0,) * len(shape))
    return pl.pallas_call(
        _rglru_kernel,
        grid=(bsz, nr),
        in_specs=[pl.BlockSpec((ROWS, width), lambda b, r: (b * nr + r, col_x)),
                  full((D_CONV, width)), full((SUBLANES, width)),
                  full((D_BLOCKS, D_BLOCK_DIM, D_BLOCK_DIM)), full((D_BLOCKS, D_BLOCK_DIM, D_BLOCK_DIM))],
        out_specs=pl.BlockSpec((ROWS, width), lambda b, r: (b * nr + r, 0)),
        out_shape=jax.ShapeDtypeStruct((t, width), F32),
        scratch_shapes=[pltpu.VMEM((SUBLANES + ROWS, width), F32),
                        pltpu.VMEM((ROWS, width), F32),
                        pltpu.VMEM((ROWS, width), F32),
                        pltpu.VMEM((SUBLANES, width), F32)],
        compiler_params=_params(("parallel", "arbitrary")),
        name="rglru",
    )(z, conv_w, vecs, wa, wx)


def _outproj_kernel(*refs, nbr, final):
    br = refs[:nbr]
    gt = refs[nbr:2 * nbr]
    ws = refs[2 * nbr:3 * nbr]
    x_ref, p_ref, pp_ref, pg_ref, pn_ref = refs[3 * nbr:3 * nbr + 5]
    rest = refs[3 * nbr + 5:]
    if final:
        fn_ref, o_ref = rest
    else:
        (o_ref,) = rest

    acc = x_ref[...]
    for b, g, w in zip(br, gt, ws):
        gv = g[...]
        y = b[...] * (gv * _sigmoid(gv))
        acc = acc + _nn(y.astype(BF16), w[...])
    ms = jnp.mean(acc * acc, axis=-1, keepdims=True)
    hn = (acc * lax.rsqrt(ms + EPS) * pn_ref[...]).astype(BF16)
    gate = _sigmoid(_nn(hn, pg_ref[...]))
    out = acc + _nn(p_ref[...].astype(BF16), pp_ref[...]) * gate
    if final:
        ms2 = jnp.mean(out * out, axis=-1, keepdims=True)
        out = out * lax.rsqrt(ms2 + EPS) * fn_ref[...]
    o_ref[...] = out


def _outproj(branches, z, gate_cols, w_parts, x, p_i, ple_proj, ple_gate, ple_norm, final_norm, tm):
    t, d = x.shape
    nbr = len(branches)
    in_specs, args = [], []
    for b in branches:
        in_specs.append(pl.BlockSpec((tm, b.shape[1]), lambda i: (i, 0)))
        args.append(b)
    for b, gc in zip(branches, gate_cols):
        in_specs.append(pl.BlockSpec((tm, b.shape[1]), lambda i, gc=gc: (i, gc)))
        args.append(z)
    for w in w_parts:
        in_specs.append(pl.BlockSpec(w.shape, lambda i: (0, 0)))
        args.append(w)
    in_specs += [pl.BlockSpec((tm, d), lambda i: (i, 0)),
                 pl.BlockSpec((tm, P_DIM), lambda i: (i, 0)),
                 pl.BlockSpec((P_DIM, d), lambda i: (0, 0)),
                 pl.BlockSpec((d, d), lambda i: (0, 0)),
                 pl.BlockSpec((1, d), lambda i: (0, 0))]
    args += [x, p_i, ple_proj, ple_gate, ple_norm.reshape(1, d)]
    final = final_norm is not None
    if final:
        in_specs.append(pl.BlockSpec((1, d), lambda i: (0, 0)))
        args.append(final_norm.reshape(1, d))
    return pl.pallas_call(
        functools.partial(_outproj_kernel, nbr=nbr, final=final),
        grid=(t // tm,),
        in_specs=in_specs,
        out_specs=pl.BlockSpec((tm, d), lambda i: (i, 0)),
        out_shape=jax.ShapeDtypeStruct((t, d), F32),
        compiler_params=_params(("parallel",)),
        name="outproj",
    )(*args)


def _even_layer(x, bsz, seq, norm_g, w_in, w_out, lb, hg_norm, mu, w0, w2, a0, a2, k_k, k_a, r_k, ln_g, ln_b,
                p_i, ple_proj, ple_gate, ple_norm, final_norm):
    aw = A_HEADS * A_HEAD_DIM
    bw = B_HEADS * B_HEAD_DIM
    rw0 = 3 * aw
    lr0 = rw0 + 3 * bw
    g0 = lr0 + 2 * B_RANK
    w_perm = jnp.concatenate([w_in[:, :lr0], w_in[:, g0:], w_in[:, lr0:g0]], axis=1).astype(BF16)
    z = _inproj(x, norm_g, w_perm, tm=1024, tn=w_perm.shape[1] // 5)

    zero = jnp.zeros_like(lb)
    hpar = jnp.stack([jnp.log(lb), jnp.log1p(-lb), 1.0 - lb, hg_norm, zero, zero, zero, zero])

    zrow = jnp.zeros((bw,), F32)
    rows = [mu[0:bw], mu[bw:2 * bw], mu[2 * bw:3 * bw], w0, a0, k_k, k_a, r_k, ln_g, ln_b]
    rpar = jnp.stack(rows + [zrow] * (_RW_NPAR - len(rows)))
    muwa = mu[3 * bw:].reshape(1, 2 * B_RANK)
    npair = B_HEADS // 2
    zpad = jnp.zeros((npair, B_RANK, LANES), F32)
    w2p = jnp.concatenate([w2.reshape(B_RANK, npair, LANES).transpose(1, 0, 2), zpad], axis=1).astype(BF16)
    a2p = jnp.concatenate([zpad, a2.reshape(B_RANK, npair, LANES).transpose(1, 0, 2)], axis=1).astype(BF16)
    oa, ob = _even_mixer(z, hpar, rpar, muwa, w2p, a2p, bsz, seq, rw0, lr0 + aw + bw)

    gate0 = lr0
    w_out_b = w_out.astype(BF16)
    return _outproj([oa, ob], z, [gate0 // aw, (gate0 + aw) // bw], [w_out_b[:aw], w_out_b[aw:]],
                    x, p_i, ple_proj.astype(BF16), ple_gate.astype(BF16), ple_norm, final_norm, tm=OUTPROJ_ROWS)


def _odd_layer(x, bsz, seq, norm_g, w_in, w_out, conv_w, conv_b, wa, ba, wx, bx, lam,
               p_i, ple_proj, ple_gate, ple_norm, final_norm):
    cw = C_HEADS * C_HEAD_DIM
    dw = D_BLOCKS * D_BLOCK_DIM
    x0 = 3 * cw
    g0 = x0 + dw
    w_perm = jnp.concatenate([w_in[:, :x0], w_in[:, g0:g0 + cw], w_in[:, x0:g0], w_in[:, g0 + cw:]],
                             axis=1).astype(BF16)
    z = _inproj(x, norm_g, w_perm, tm=1024, tn=w_perm.shape[1] // 4)

    cb = lambda c: c // LANES
    oc = _moba(z, bsz, seq, cb(0), cb(cw), cb(2 * cw))
    zero = jnp.zeros_like(lam)
    vecs = jnp.stack([conv_b, ba, bx, lam, zero, zero, zero, zero])
    od = _rglru(z, conv_w, vecs, wa.astype(BF16), wx.astype(BF16), bsz, seq, (x0 + cw) // dw)

    w_out_b = w_out.astype(BF16)
    return _outproj([oc, od], z, [x0 // cw, (x0 + cw + dw) // dw], [w_out_b[:cw], w_out_b[cw:]],
                    x, p_i, ple_proj.astype(BF16), ple_gate.astype(BF16), ple_norm, final_norm, tm=OUTPROJ_ROWS)


def kernel(x, p, ev_norm, ev_w_in, ev_w_out, hg_lb_logits, hg_norm, rw_mu, rw_w0, rw_w2, rw_a0, rw_a2, rw_k_k, rw_k_a, rw_r_k, rw_ln_g, rw_ln_b, od_norm, od_w_in, od_w_out, lru_conv_w, lru_conv_b, lru_wa, lru_ba, lru_wx, lru_bx, lru_lambda, ple_proj, ple_gate, ple_norm, final_norm):
    bsz, seq, d = x.shape
    depth = p.shape[0]
    s = jax.nn.softmax(hg_lb_logits.astype(F32), axis=0)
    lower_bounds = jnp.maximum(jnp.cumsum(s, axis=0) - s[0], 0.0)
    xf = x.reshape(bsz * seq, d)
    pf = p.reshape(depth, bsz * seq, p.shape[-1])
    for i in range(depth):
        j = i // 2
        fin = final_norm if i == depth - 1 else None
        if i % 2 == 0:
            xf = _even_layer(xf, bsz, seq, ev_norm[j], ev_w_in[j], ev_w_out[j], lower_bounds[j], hg_norm[j],
                             rw_mu[j], rw_w0[j], rw_w2[j], rw_a0[j], rw_a2[j], rw_k_k[j], rw_k_a[j], rw_r_k[j],
                             rw_ln_g[j], rw_ln_b[j], pf[i], ple_proj[i], ple_gate[i], ple_norm[i], fin)
        else:
            xf = _odd_layer(xf, bsz, seq, od_norm[j], od_w_in[j], od_w_out[j], lru_conv_w[j], lru_conv_b[j],
                            lru_wa[j], lru_ba[j], lru_wx[j], lru_bx[j], lru_lambda[j],
                            pf[i], ple_proj[i], ple_gate[i], ple_norm[i], fin)
    return xf.reshape(bsz, seq, d)
```

```python
import functools
import math

import jax
import jax.numpy as jnp
from jax import lax
from jax.experimental import pallas as pl
from jax.experimental.pallas import tpu as pltpu

F32 = jnp.float32
BF16 = jnp.bfloat16

EPS = 1e-6
LANES = 128
SUBLANES = 8
VMEM_LIMIT = 56 * 1024 * 1024

A_HEADS, A_HEAD_DIM = 8, 128
B_HEADS, B_HEAD_DIM = 16, 64
B_RANK = 64
B_LN_EPS = 64e-5
C_HEADS, C_HEAD_DIM, C_BLOCK, C_TOPK = 8, 64, 256, 3
D_BLOCKS, D_BLOCK_DIM, D_CONV = 8, 128, 4
LRU_C = 8.0
P_DIM = 256

OUTPROJ_ROWS = 512
CHUNK = 64
ROWS = 256
NEG_BIG = -16384.0


def _nt(a, b):
    return lax.dot_general(a, b, (((1,), (1,)), ((), ())), preferred_element_type=F32)


def _tn(a, b):
    return lax.dot_general(a, b, (((0,), (0,)), ((), ())), preferred_element_type=F32)


def _nn(a, b):
    return jnp.dot(a, b, preferred_element_type=F32)


def _split3(x):
    h = x.astype(BF16)
    r = x - h.astype(F32)
    m = r.astype(BF16)
    l = (r - m.astype(F32)).astype(BF16)
    return h, m, l


def _nn_exact_lhs(lhs_bf16, x):
    h, m, l = _split3(x)
    return _nn(lhs_bf16, h) + _nn(lhs_bf16, m) + _nn(lhs_bf16, l)


def _nn_exact_rhs(x, rhs_bf16):
    h, m, l = _split3(x)
    return _nn(h, rhs_bf16) + _nn(m, rhs_bf16) + _nn(l, rhs_bf16)


def _sigmoid(x):
    return 0.5 * jnp.tanh(0.5 * x) + 0.5


def _round_robin(gens, starts=None):
    starts = starts or [0] * len(gens)
    results = [None] * len(gens)
    done = [False] * len(gens)
    rnd = 0
    while not all(done):
        for i in range(len(gens)):
            if done[i] or rnd < starts[i]:
                continue
            try:
                next(gens[i])
            except StopIteration as stop:
                results[i] = stop.value
                done[i] = True
        rnd += 1
    return results


def _params(sem):
    return pltpu.CompilerParams(dimension_semantics=sem, vmem_limit_bytes=VMEM_LIMIT)


def _inproj_kernel(x_ref, g_ref, w_ref, o_ref, h_ref):
    @pl.when(pl.program_id(1) == 0)
    def _():
        x = x_ref[...]
        ms = jnp.mean(x * x, axis=-1, keepdims=True)
        h_ref[...] = (x * lax.rsqrt(ms + EPS) * g_ref[...]).astype(BF16)

    o_ref[...] = _nn(h_ref[...], w_ref[...]).astype(o_ref.dtype)


MXU_COLS = 256
INPROJ_ROWS = 1024


def _inproj(x, g, w, ntiles):
    t, d = x.shape
    tm = INPROJ_ROWS
    quantum = MXU_COLS * ntiles
    n = -(-w.shape[1] // quantum) * quantum
    w = jnp.pad(w, ((0, 0), (0, n - w.shape[1])))
    tn = n // ntiles
    return pl.pallas_call(
        _inproj_kernel,
        grid=(t // tm, n // tn),
        in_specs=[
            pl.BlockSpec((tm, d), lambda i, j: (i, 0)),
            pl.BlockSpec((1, d), lambda i, j: (0, 0)),
            pl.BlockSpec((d, tn), lambda i, j: (0, j)),
        ],
        out_specs=pl.BlockSpec((tm, tn), lambda i, j: (i, j)),
        out_shape=jax.ShapeDtypeStruct((t, n), BF16),
        scratch_shapes=[pltpu.VMEM((tm, d), BF16)],
        compiler_params=_params(("parallel", "arbitrary")),
        name="inproj",
    )(x, g.reshape(1, d), w)


_LEVELS = (1, 2, 4, 8, 16, 32)


def _boundary_bcast(g, m):
    c, w = g.shape
    if m >= SUBLANES:
        gr = g.reshape(c // (2 * m), 2 * m, w)
        b = jnp.broadcast_to(gr[:, m - 1:m, :], gr.shape)
        return b.reshape(c, w)
    gr = g.reshape(c // SUBLANES, SUBLANES, w)
    j = lax.broadcasted_iota(jnp.int32, gr.shape, 1)
    rows = [jnp.broadcast_to(gr[:, r:r + 1, :], gr.shape) for r in range(m - 1, SUBLANES, 2 * m)]
    b = rows[-1]
    for idx in range(len(rows) - 2, -1, -1):
        b = jnp.where(j < (idx + 1) * 2 * m, rows[idx], b)
    return b.reshape(c, w)


def _hgrn2_chains(zq_ref, zf_ref, zi_ref, par_ref, o_ref, st_ref, first_block):
    c = CHUNK

    @pl.when(first_block)
    def _():
        st_ref[...] = jnp.zeros_like(st_ref)

    row = lax.broadcasted_iota(jnp.int32, (c, c), 0)
    col = lax.broadcasted_iota(jnp.int32, (c, c), 1)
    ltri = (col <= row).astype(BF16)
    ltri2 = jnp.concatenate([ltri, ltri], axis=1)
    eye = col == row
    lvl_masks = []
    for m in _LEVELS:
        sh = int(math.log2(2 * m))
        same = (row >> sh) == (col >> sh)
        lvl_masks.append(same & ((row & (2 * m - 1)) >= m) & ((col & (2 * m - 1)) < m))
    rowi = lax.broadcasted_iota(jnp.int32, (c, LANES), 0)

    nseq, nheads = st_ref.shape[0], st_ref.shape[1]
    par_all = par_ref[...]

    def head_chunk(hi, zq, zf, v, st):
        cs = slice(hi * LANES, (hi + 1) * LANES)
        log_lb = par_all[0:1, cs]
        log_omlb = par_all[1:2, cs]
        omlb = par_all[2:3, cs]
        gain = par_all[3:4, cs]

        e = jnp.exp(-jnp.abs(zf))
        log_sig = jnp.minimum(zf, 0.0) - jnp.log(1.0 + e)
        sig_neg = _sigmoid(-zf)
        bterm = log_omlb + log_sig
        mx = jnp.maximum(log_lb, bterm)
        log_f = mx + jnp.log(1.0 + jnp.exp(-jnp.abs(log_lb - bterm)))
        k = omlb * sig_neg
        q = zq * _sigmoid(zq)

        lf_h = log_f.astype(BF16)
        lf_l = (log_f - lf_h.astype(F32)).astype(BF16)
        g = _nn(ltri2, jnp.concatenate([lf_h, lf_l], axis=0))
        yield
        g_last = g[c - 1:c, :]

        qb = q.astype(BF16)
        kb = k.astype(BF16)
        vb = v.astype(BF16)
        attn = jnp.where(eye, _nt(qb, kb), 0.0)
        for m, msk in zip(_LEVELS, lvl_masks):
            second = (rowi & (2 * m - 1)) >= m
            if m == 1:
                fac = jnp.where(second, jnp.exp(log_f), 1.0)
            else:
                fac = jnp.exp(-jnp.abs(g - _boundary_bcast(g, m)))
            x = (jnp.where(second, q, k) * fac).astype(BF16)
            attn = attn + jnp.where(msk, _nt(x, x), 0.0)
        yield

        o = _nn(attn.astype(BF16), vb) + _nt((q * jnp.exp(g)).astype(BF16), st.astype(BF16))
        khat = (k * jnp.exp(g_last - g)).astype(BF16)
        st_new = st * jnp.exp(g_last) + _tn(vb, khat)
        yield

        ms = jnp.mean(o * o, axis=-1, keepdims=True)
        return o * lax.rsqrt(ms + EPS) * gain, st_new

    def chains(ci):
        sl = pl.ds(pl.multiple_of(ci * c, c), c)
        out = []
        for bi in range(nseq):
            for hi in range(nheads):
                cs = slice(hi * LANES, (hi + 1) * LANES)

                def store(res, bi=bi, hi=hi, cs=cs):
                    o_ref[bi, sl, cs] = res[0]
                    st_ref[bi, hi] = res[1]

                load = lambda ref: ref[bi, sl, cs].astype(F32)
                out.append((head_chunk(hi, load(zq_ref), load(zf_ref), load(zi_ref), st_ref[bi, hi]), store))
        return out

    return chains


_EXP_M05 = math.exp(-0.5)
(_P_MU_R, _P_MU_K, _P_MU_V, _P_W0, _P_A0, _P_KK, _P_KA, _P_RK, _P_LNG, _P_LNB) = range(10)
_RW_NPAR = 16


def _rwkv_chains(zr_ref, zk_ref, zv_ref, zwa_ref, par_ref, muwa_ref, w2_ref, a2_ref, o_ref,
                 st_ref, ext_ref, extwa_ref, first_block):
    c = CHUNK
    hd = B_HEAD_DIM
    nbat, npair = st_ref.shape[0], st_ref.shape[1]

    @pl.when(first_block)
    def _():
        st_ref[...] = jnp.zeros_like(st_ref)
        ext_ref[:, :, 0:SUBLANES, :] = jnp.zeros((nbat, 3, SUBLANES, npair * LANES), F32)
        extwa_ref[:, 0:SUBLANES, :] = jnp.zeros((nbat, SUBLANES, LANES), F32)

    par_all = par_ref[...]

    @pl.when(jnp.logical_not(first_block))
    def _():
        ext_ref[:, :, 0:SUBLANES, :] = ext_ref[:, :, ROWS:SUBLANES + ROWS, :]
        extwa_ref[:, 0:SUBLANES, :] = extwa_ref[:, ROWS:SUBLANES + ROWS, :]

    for bi in range(nbat):
        ext_ref[bi, 0, SUBLANES:SUBLANES + ROWS, :] = zr_ref[bi].astype(F32)
        ext_ref[bi, 1, SUBLANES:SUBLANES + ROWS, :] = zk_ref[bi].astype(F32)
        ext_ref[bi, 2, SUBLANES:SUBLANES + ROWS, :] = zv_ref[bi].astype(F32)
        extwa_ref[bi, SUBLANES:SUBLANES + ROWS, :] = zwa_ref[bi].astype(F32)

    def mixed(ext, ci, cs, mu):
        win = ext[pl.ds(pl.multiple_of(ci * c, c), c + SUBLANES), cs]
        cur = win[SUBLANES:SUBLANES + c]
        prev = win[SUBLANES - 1:SUBLANES - 1 + c]
        return cur + (prev - cur) * mu

    lane = lax.broadcasted_iota(jnp.int32, (c, LANES), 1)
    lo_l = lane < hd
    lane2 = lax.broadcasted_iota(jnp.int32, (2 * c, LANES), 1)
    row2 = lax.broadcasted_iota(jnp.int32, (2 * c, LANES), 0)
    s_idx = lane2 & (c - 1)
    t_idx = row2 & (c - 1)
    tri = (s_idx < t_idx) | ((row2 >= c) & (s_idx == t_idx))
    eye_s = ((lane & (c - 1)) == lax.broadcasted_iota(jnp.int32, (c, LANES), 0)).astype(F32)
    ltri = (lax.broadcasted_iota(jnp.int32, (c, c), 1)
            <= lax.broadcasted_iota(jnp.int32, (c, c), 0)).astype(BF16)
    ltri2 = jnp.concatenate([ltri, ltri], axis=1)
    seg = ((lax.broadcasted_iota(jnp.int32, (LANES, LANES), 0) >> int(math.log2(hd)))
           == (lax.broadcasted_iota(jnp.int32, (LANES, LANES), 1) >> int(math.log2(hd))))
    seg_b = seg.astype(BF16)
    seg2 = jnp.concatenate([seg_b, seg_b], axis=0)

    def split2(x):
        h = x.astype(BF16)
        return h, (x - h.astype(F32)).astype(BF16)

    def segsum(x):
        return _nn(jnp.concatenate(split2(x), axis=1), seg2)

    def by_head(x):
        zero = jnp.zeros_like(x)
        return jnp.concatenate([jnp.where(lo_l, x, zero), jnp.where(lo_l, zero, x)], axis=0)

    def pair_chunk(pi, r, kraw, v, st, w2, a2, wab, twab):
        cs = slice(pi * LANES, (pi + 1) * LANES)
        prow = lambda i: par_all[i:i + 1, cs]

        xw = prow(_P_W0) + _nn(twab, w2)
        g = -_EXP_M05 * _sigmoid(xw)
        alpha = _sigmoid(prow(_P_A0) + _nn(wab, a2))
        yield
        kk = kraw * prow(_P_KK)
        k = kraw * (1.0 + (alpha - 1.0) * prow(_P_KA))
        sums = segsum(jnp.concatenate([kk * kk, r * k * prow(_P_RK)], axis=0))
        gc = _nn(ltri2, jnp.concatenate(split2(g), axis=0))
        yield
        kk = kk / jnp.maximum(jnp.sqrt(sums[0:c]), 1e-12)
        rk_sum = sums[c:2 * c]
        bv = kk * alpha
        g_last = gc[c - 1:c, :]
        e_pos = jnp.exp(gc)
        e_neg = jnp.exp(-gc)
        e_last = jnp.exp(g_last)
        at = -kk * jnp.exp(gc - g)
        rt = r * e_pos
        bt = bv * e_neg
        kt = k * e_neg
        bh = bt * e_last
        kh = kt * e_last

        atb = at.astype(BF16)
        rtb = rt.astype(BF16)
        lhs = jnp.concatenate([atb, rtb], axis=0)
        ab_rb = jnp.where(tri, _nt(lhs, by_head(bt.astype(BF16))), 0.0)
        ak_rk = jnp.where(tri, _nt(lhs, by_head(kt.astype(BF16))), 0.0)
        yield

        stb = st.astype(BF16)
        vh = by_head(v.astype(BF16))
        w = _nt(atb, stb) + _nn(ak_rk[0:c].astype(BF16), vh)
        pw = ab_rb[0:c]
        t_inv = eye_s + pw
        for _ in range(int(math.log2(c)) - 1):
            pwb = pw.astype(BF16)
            pw = _nn(pwb, by_head(pwb))
            yield
            t_inv = t_inv + _nn(pw.astype(BF16), by_head(t_inv.astype(BF16)))
        yield

        u = _nn(t_inv.astype(BF16), by_head(w.astype(BF16)))
        yield
        ub = u.astype(BF16)
        a_r = jnp.concatenate([ab_rb[c:2 * c], ak_rk[c:2 * c]], axis=1).astype(BF16)
        y = _nt(rtb, stb) + _nn(a_r, jnp.concatenate([by_head(ub), vh], axis=0))

        st_new = st * e_last + _tn(jnp.concatenate([ub, v.astype(BF16)], axis=0),
                                   jnp.concatenate([bh.astype(BF16), kh.astype(BF16)], axis=0))
        st_new = jnp.where(seg, st_new, 0.0)
        yield

        inv_n = 1.0 / hd
        moments = segsum(jnp.concatenate([y, y * y], axis=0)) * inv_n
        yield
        mu = moments[0:c]
        var = moments[c:2 * c] - mu * mu
        yn = (y - mu) * lax.rsqrt(var + B_LN_EPS) * prow(_P_LNG) + prow(_P_LNB)
        yn = yn + rk_sum * v
        return yn, st_new

    def chains(ci):
        out = []
        for bi in range(nbat):
            wa = mixed(extwa_ref.at[bi], ci, slice(None), muwa_ref[...])
            wab = wa.astype(BF16)
            twab = jnp.tanh(wa).astype(BF16)
            for pi in range(npair):
                cs = slice(pi * LANES, (pi + 1) * LANES)
                mu_of = lambda i: par_all[i:i + 1, cs]

                def store(res, bi=bi, pi=pi, cs=cs):
                    o_ref[bi, pl.ds(pl.multiple_of(ci * c, c), c), cs] = res[0]
                    st_ref[bi, pi] = res[1]

                out.append((pair_chunk(pi, mixed(ext_ref.at[bi, 0], ci, cs, mu_of(_P_MU_R)),
                                       mixed(ext_ref.at[bi, 1], ci, cs, mu_of(_P_MU_K)),
                                       mixed(ext_ref.at[bi, 2], ci, cs, mu_of(_P_MU_V)),
                                       st_ref[bi, pi], w2_ref[pi], a2_ref[pi], wab, twab), store))
        return out

    return chains


EV_SEQS_PER_STEP = 2
HG_STAGES = 3
RW_STAGES = 13


def _even_mixer_kernel(zq_ref, zf_ref, zi_ref, zr_ref, zk_ref, zv_ref, zwa_ref, hpar_ref, rpar_ref, muwa_ref,
                       w2_ref, a2_ref, oa_ref, ob_ref, hst_ref, rst_ref, ext_ref, extwa_ref):
    first = pl.program_id(1) == 0
    hg = _hgrn2_chains(zq_ref, zf_ref, zi_ref, hpar_ref, oa_ref, hst_ref, first)
    rw = _rwkv_chains(zr_ref, zk_ref, zv_ref, zwa_ref, rpar_ref, muwa_ref, w2_ref, a2_ref, ob_ref,
                      rst_ref, ext_ref, extwa_ref, first)

    def chunk(ci, carry):
        hg_chains = hg(ci)
        rw_chains = rw(ci)
        span = max(RW_STAGES - HG_STAGES, 0)
        starts = [(i * span) // max(len(hg_chains) - 1, 1) for i in range(len(hg_chains))] + [0] * len(rw_chains)
        both = hg_chains + rw_chains
        for (_, store), res in zip(both, _round_robin([g for g, _ in both], starts)):
            store(res)
        return carry

    lax.fori_loop(0, ROWS // CHUNK, chunk, 0)


def _even_mixer(z, hpar, rpar, muwa, w2p, a2p, bsz, seq, col_rkv, col_wa):
    t, n = z.shape
    nr = seq // ROWS
    nsq = EV_SEQS_PER_STEP
    aw = A_HEADS * A_HEAD_DIM
    bw = B_HEADS * B_HEAD_DIM
    npair = B_HEADS // 2
    z3 = z.reshape(bsz, seq, n)
    wide = lambda width, col: pl.BlockSpec((nsq, ROWS, width), lambda b, r: (b, r, col // width))
    const = lambda shape: pl.BlockSpec(shape, lambda b, r: (0,) * len(shape))
    oa, ob = pl.pallas_call(
        _even_mixer_kernel,
        grid=(bsz // nsq, nr),
        in_specs=[wide(aw, 0), wide(aw, aw), wide(aw, 2 * aw),
                  wide(bw, col_rkv), wide(bw, col_rkv + bw), wide(bw, col_rkv + 2 * bw),
                  wide(LANES, col_wa),
                  const((SUBLANES, aw)), const((_RW_NPAR, bw)), const((1, LANES)),
                  const((npair, LANES, LANES)), const((npair, LANES, LANES))],
        out_specs=[pl.BlockSpec((nsq, ROWS, aw), lambda b, r: (b, r, 0)),
                   pl.BlockSpec((nsq, ROWS, bw), lambda b, r: (b, r, 0))],
        out_shape=[jax.ShapeDtypeStruct((bsz, seq, aw), F32), jax.ShapeDtypeStruct((bsz, seq, bw), F32)],
        scratch_shapes=[pltpu.VMEM((nsq, A_HEADS, A_HEAD_DIM, A_HEAD_DIM), F32),
                        pltpu.VMEM((nsq, npair, LANES, LANES), F32),
                        pltpu.VMEM((nsq, 3, SUBLANES + ROWS, bw), F32),
                        pltpu.VMEM((nsq, SUBLANES + ROWS, LANES), F32)],
        compiler_params=_params(("parallel", "arbitrary")),
        name="even_mixer",
    )(z3, z3, z3, z3, z3, z3, z3, hpar, rpar, muwa, w2p, a2p)
    return oa.reshape(t, aw), ob.reshape(t, bw)


def _moba_kernel(q_ref, k_ref, v_ref, o_ref, kmean_ref, kb_ref, vt_ref, bias_ref):
    nb = kmean_ref.shape[0]
    blk = C_BLOCK
    hd = C_HEAD_DIM
    n = pl.program_id(2)
    npair = q_ref.shape[1] // LANES

    @pl.when(n == 0)
    def _():
        for j in range(nb):
            kj = k_ref[j * blk:(j + 1) * blk, :]
            kmean_ref[j:j + 1, :] = jnp.mean(kj.astype(F32), axis=0, keepdims=True)
            kb_ref[j] = kj
            for pi in range(npair):
                vj = v_ref[j * blk:(j + 1) * blk, pi * LANES:(pi + 1) * LANES]
                vt_ref[pi, j] = vj.astype(F32).T.astype(BF16)

    lane_m = lax.broadcasted_iota(jnp.int32, (nb, LANES), 1)
    jidx = lax.broadcasted_iota(jnp.int32, (nb, blk), 0)
    valid = jidx < n
    row_d = lax.broadcasted_iota(jnp.int32, (LANES, blk), 0)
    lo_d = row_d < hd
    key_i = lax.broadcasted_iota(jnp.int32, (blk, blk), 0)
    qry_i = lax.broadcasted_iota(jnp.int32, (blk, blk), 1)
    causal = key_i <= qry_i
    qscale = (hd ** -0.5) * math.log2(math.e)

    def select_bias(km, q_t):
        gate = lax.dot_general(km, q_t, (((1,), (0,)), ((), ())), precision=lax.Precision.HIGHEST,
                               preferred_element_type=F32)
        gm = jnp.where(valid, gate, -jnp.inf)
        cnt = jnp.zeros((nb, blk), F32)
        for j2 in range(nb):
            gj = gm[j2:j2 + 1, :]
            better = (gj > gm) | ((gj == gm) & (j2 < jidx))
            cnt = cnt + jnp.where(better, 1.0, 0.0)
        sel = valid & (cnt < float(C_TOPK))
        return jnp.where(sel, 0.0, NEG_BIG)

    qt_heads = []
    for pi in range(npair):
        cs = slice(pi * LANES, (pi + 1) * LANES)
        q_t = q_ref[:, cs].astype(F32).T
        kmean = kmean_ref[:, cs]
        bias_ref[2 * pi] = select_bias(jnp.where(lane_m < hd, kmean, 0.0), q_t)
        bias_ref[2 * pi + 1] = select_bias(jnp.where(lane_m >= hd, kmean, 0.0), q_t)
        qs = q_t * qscale
        qt_heads.append((jnp.where(lo_d, qs, 0.0).astype(BF16), jnp.where(lo_d, 0.0, qs).astype(BF16)))

    nq = blk // MB_QUERY_TILE
    lo_q = lo_d[:, 0:MB_QUERY_TILE]

    def pair_block(pi, qi, j, prev, own):
        qs = slice(qi * MB_QUERY_TILE, (qi + 1) * MB_QUERY_TILE)
        kj = kb_ref[j, :, pi * LANES:(pi + 1) * LANES]
        s = [_nn(kj, qt[:, qs]) for qt in qt_heads[pi]]
        yield
        if own:
            s = [jnp.where(causal[:, qs], x, -jnp.inf) for x in s]
        else:
            s = [x + bias_ref[2 * pi + h, pl.ds(j, 1), :][:, qs] for h, x in enumerate(s)]
        smax = [jnp.max(x, axis=0, keepdims=True) for x in s]
        m_new = smax if own else [jnp.maximum(m, x) for m, x in zip(prev[0], smax)]
        p = [jnp.exp2(x - m) for x, m in zip(s, m_new)]
        psum = [jnp.sum(x, axis=0, keepdims=True) for x in p]
        vt = vt_ref[pi, j]
        pv = jnp.concatenate([_nn(vt[0:hd], p[0].astype(BF16)), _nn(vt[hd:], p[1].astype(BF16))], axis=0)
        yield
        if own:
            return m_new, psum, pv
        corr = [jnp.exp2(m - mn) for m, mn in zip(prev[0], m_new)]
        l_new = [l * c + x for l, c, x in zip(prev[1], corr, psum)]
        return m_new, l_new, prev[2] * jnp.where(lo_q, corr[0], corr[1]) + pv

    def visit(j, carry, own):
        gens = [pair_block(i // nq, i % nq, j, None if own else carry[i], own) for i in range(npair * nq)]
        return tuple(_round_robin(gens))

    res = lax.fori_loop(0, n, lambda j, carry: visit(j, carry, False), visit(n, None, True))
    for pi in range(npair):
        tiles = []
        for qi in range(nq):
            _, l, acc = res[pi * nq + qi]
            tiles.append(acc / jnp.where(lo_q, l[0], l[1]))
        out_t = jnp.concatenate(tiles, axis=1)
        o_ref[:, pi * LANES:(pi + 1) * LANES] = out_t.T


MB_PAIRS_PER_STEP = 4
MB_QUERY_TILE = 256


def _moba(z, bsz, seq, col_q, col_k, col_v):
    t = z.shape[0]
    nb = seq // C_BLOCK
    npp = MB_PAIRS_PER_STEP
    ngrp = (C_HEADS // 2) // npp
    width = npp * LANES
    return pl.pallas_call(
        _moba_kernel,
        grid=(bsz, ngrp, nb),
        in_specs=[pl.BlockSpec((C_BLOCK, width), lambda b, p, n: (b * nb + n, col_q // npp + p)),
                  pl.BlockSpec((seq, width), lambda b, p, n: (b, col_k // npp + p)),
                  pl.BlockSpec((seq, width), lambda b, p, n: (b, col_v // npp + p))],
        out_specs=pl.BlockSpec((C_BLOCK, width), lambda b, p, n: (b * nb + n, p)),
        out_shape=jax.ShapeDtypeStruct((t, C_HEADS * C_HEAD_DIM), F32),
        scratch_shapes=[pltpu.VMEM((nb, width), F32),
                        pltpu.VMEM((nb, C_BLOCK, width), BF16),
                        pltpu.VMEM((npp, nb, LANES, C_BLOCK), BF16),
                        pltpu.VMEM((2 * npp, nb, C_BLOCK), F32)],
        compiler_params=_params(("parallel", "parallel", "arbitrary")),
        name="moba",
    )(z, z, z)


def _rglru_kernel(x_ref, cw_ref, vec_ref, wa_ref, wx_ref, o_ref, ext_ref, a_ref, b_ref, h_ref):
    rows = x_ref.shape[0]
    width = x_ref.shape[1]

    @pl.when(pl.program_id(1) == 0)
    def _():
        ext_ref[0:SUBLANES, :] = jnp.zeros((SUBLANES, width), F32)
        h_ref[...] = jnp.zeros_like(h_ref)

    x = x_ref[...].astype(F32)
    ext_ref[SUBLANES:SUBLANES + rows, :] = x
    xc = vec_ref[0:1, :] + x * cw_ref[D_CONV - 1:D_CONV, :]
    for d in range(1, D_CONV):
        xc = xc + ext_ref[SUBLANES - d:SUBLANES - d + rows, :] * cw_ref[D_CONV - 1 - d:D_CONV - d, :]
    ext_ref[0:SUBLANES, :] = x[rows - SUBLANES:rows, :]

    ba = vec_ref[1:2, :]
    bx = vec_ref[2:3, :]
    lam = vec_ref[3:4, :]
    sp = jnp.maximum(-lam, 0.0) + jnp.log(1.0 + jnp.exp(-jnp.abs(lam)))
    for nblk in range(D_BLOCKS):
        cs = slice(nblk * D_BLOCK_DIM, (nblk + 1) * D_BLOCK_DIM)
        xb = xc[:, cs]
        xbb = xb.astype(BF16)
        rg = _sigmoid(_nn(xbb, wa_ref[nblk]) + ba[:, cs])
        ig = _sigmoid(_nn(xbb, wx_ref[nblk]) + bx[:, cs])
        log_a = -LRU_C * rg * sp[:, cs]
        th = jnp.tanh(log_a)
        one_minus_a2 = -2.0 * th / (1.0 - th)
        a_ref[:, cs] = jnp.exp(log_a)
        b_ref[:, cs] = jnp.sqrt(one_minus_a2) * (ig * xb)

    rowi = lax.broadcasted_iota(jnp.int32, (SUBLANES, width), 0)

    def group(gi, carry):
        sl = pl.ds(pl.multiple_of(gi * SUBLANES, SUBLANES), SUBLANES)
        a = a_ref[sl, :]
        b = b_ref[sl, :]
        for d in (1, 2, 4):
            keep = rowi >= d
            a_sh = jnp.where(keep, pltpu.roll(a, d, 0), 1.0)
            b_sh = jnp.where(keep, pltpu.roll(b, d, 0), 0.0)
            b = a * b_sh + b
            a = a * a_sh
        h = a * carry + b
        o_ref[sl, :] = h
        return jnp.broadcast_to(h[SUBLANES - 1:SUBLANES, :], (SUBLANES, width))

    h_ref[...] = lax.fori_loop(0, rows // SUBLANES, group, h_ref[...])


def _rglru(z, conv_w, vecs, wa, wx, bsz, seq, col_x):
    t = z.shape[0]
    width = D_BLOCKS * D_BLOCK_DIM
    nr = seq // ROWS
    full = lambda shape: pl.BlockSpec(shape, lambda b, r: (0,) * len(shape))
    return pl.pallas_call(
        _rglru_kernel,
        grid=(bsz, nr),
        in_specs=[pl.BlockSpec((ROWS, width), lambda b, r: (b * nr + r, col_x)),
                  full((D_CONV, width)), full((SUBLANES, width)),
                  full((D_BLOCKS, D_BLOCK_DIM, D_BLOCK_DIM)), full((D_BLOCKS, D_BLOCK_DIM, D_BLOCK_DIM))],
        out_specs=pl.BlockSpec((ROWS, width), lambda b, r: (b * nr + r, 0)),
        out_shape=jax.ShapeDtypeStruct((t, width), F32),
        scratch_shapes=[pltpu.VMEM((SUBLANES + ROWS, width), F32),
                        pltpu.VMEM((ROWS, width), F32),
                        pltpu.VMEM((ROWS, width), F32),
                        pltpu.VMEM((SUBLANES, width), F32)],
        compiler_params=_params(("parallel", "arbitrary")),
        name="rglru",
    )(z, conv_w, vecs, wa, wx)


def _outproj_kernel(*refs, nbr, final):
    br = refs[:nbr]
    gt = refs[nbr:2 * nbr]
    ws = refs[2 * nbr:3 * nbr]
    x_ref, p_ref, pp_ref, pg_ref, pn_ref = refs[3 * nbr:3 * nbr + 5]
    rest = refs[3 * nbr + 5:]
    if final:
        fn_ref, o_ref = rest
    else:
        (o_ref,) = rest

    acc = x_ref[...]
    for b, g, w in zip(br, gt, ws):
        gv = g[...].astype(F32)
        y = b[...] * (gv * _sigmoid(gv))
        acc = acc + _nn(y.astype(BF16), w[...])
    ms = jnp.mean(acc * acc, axis=-1, keepdims=True)
    hn = (acc * lax.rsqrt(ms + EPS) * pn_ref[...]).astype(BF16)
    gate = _sigmoid(_nn(hn, pg_ref[...]))
    out = acc + _nn(p_ref[...].astype(BF16), pp_ref[...]) * gate
    if final:
        ms2 = jnp.mean(out * out, axis=-1, keepdims=True)
        out = out * lax.rsqrt(ms2 + EPS) * fn_ref[...]
    o_ref[...] = out


def _outproj(branches, z, gate_cols, w_parts, x, p_i, ple_proj, ple_gate, ple_norm, final_norm, tm):
    t, d = x.shape
    nbr = len(branches)
    in_specs, args = [], []
    for b in branches:
        in_specs.append(pl.BlockSpec((tm, b.shape[1]), lambda i: (i, 0)))
        args.append(b)
    for b, gc in zip(branches, gate_cols):
        in_specs.append(pl.BlockSpec((tm, b.shape[1]), lambda i, gc=gc: (i, gc)))
        args.append(z)
    for w in w_parts:
        in_specs.append(pl.BlockSpec(w.shape, lambda i: (0, 0)))
        args.append(w)
    in_specs += [pl.BlockSpec((tm, d), lambda i: (i, 0)),
                 pl.BlockSpec((tm, P_DIM), lambda i: (i, 0)),
                 pl.BlockSpec((P_DIM, d), lambda i: (0, 0)),
                 pl.BlockSpec((d, d), lambda i: (0, 0)),
                 pl.BlockSpec((1, d), lambda i: (0, 0))]
    args += [x, p_i, ple_proj, ple_gate, ple_norm.reshape(1, d)]
    final = final_norm is not None
    if final:
        in_specs.append(pl.BlockSpec((1, d), lambda i: (0, 0)))
        args.append(final_norm.reshape(1, d))
    return pl.pallas_call(
        functools.partial(_outproj_kernel, nbr=nbr, final=final),
        grid=(t // tm,),
        in_specs=in_specs,
        out_specs=pl.BlockSpec((tm, d), lambda i: (i, 0)),
        out_shape=jax.ShapeDtypeStruct((t, d), F32),
        compiler_params=_params(("parallel",)),
        name="outproj",
    )(*args)


def _even_layer(x, bsz, seq, norm_g, w_in, w_out, lb, hg_norm, mu, w0, w2, a0, a2, k_k, k_a, r_k, ln_g, ln_b,
                p_i, ple_proj, ple_gate, ple_norm, final_norm):
    aw = A_HEADS * A_HEAD_DIM
    bw = B_HEADS * B_HEAD_DIM
    rw0 = 3 * aw
    lr0 = rw0 + 3 * bw
    g0 = lr0 + 2 * B_RANK
    w_perm = jnp.concatenate([w_in[:, :lr0], w_in[:, g0:], w_in[:, lr0:g0]], axis=1).astype(BF16)
    z = _inproj(x, norm_g, w_perm, ntiles=3)

    zero = jnp.zeros_like(lb)
    hpar = jnp.stack([jnp.log(lb), jnp.log1p(-lb), 1.0 - lb, hg_norm, zero, zero, zero, zero])

    zrow = jnp.zeros((bw,), F32)
    rows = [mu[0:bw], mu[bw:2 * bw], mu[2 * bw:3 * bw], w0, a0, k_k, k_a, r_k, ln_g, ln_b]
    rpar = jnp.stack(rows + [zrow] * (_RW_NPAR - len(rows)))
    muwa = mu[3 * bw:].reshape(1, 2 * B_RANK)
    npair = B_HEADS // 2
    zpad = jnp.zeros((npair, B_RANK, LANES), F32)
    w2p = jnp.concatenate([w2.reshape(B_RANK, npair, LANES).transpose(1, 0, 2), zpad], axis=1).astype(BF16)
    a2p = jnp.concatenate([zpad, a2.reshape(B_RANK, npair, LANES).transpose(1, 0, 2)], axis=1).astype(BF16)
    oa, ob = _even_mixer(z, hpar, rpar, muwa, w2p, a2p, bsz, seq, rw0, lr0 + aw + bw)

    gate0 = lr0
    w_out_b = w_out.astype(BF16)
    return _outproj([oa, ob], z, [gate0 // aw, (gate0 + aw) // bw], [w_out_b[:aw], w_out_b[aw:]],
                    x, p_i, ple_proj.astype(BF16), ple_gate.astype(BF16), ple_norm, final_norm, tm=OUTPROJ_ROWS)


def _odd_layer(x, bsz, seq, norm_g, w_in, w_out, conv_w, conv_b, wa, ba, wx, bx, lam,
               p_i, ple_proj, ple_gate, ple_norm, final_norm):
    cw = C_HEADS * C_HEAD_DIM
    dw = D_BLOCKS * D_BLOCK_DIM
    x0 = 3 * cw
    g0 = x0 + dw
    w_perm = jnp.concatenate([w_in[:, :x0], w_in[:, g0:g0 + cw], w_in[:, x0:g0], w_in[:, g0 + cw:]],
                             axis=1).astype(BF16)
    z = _inproj(x, norm_g, w_perm, ntiles=2)

    cb = lambda c: c // LANES
    oc = _moba(z, bsz, seq, cb(0), cb(cw), cb(2 * cw))
    zero = jnp.zeros_like(lam)
    vecs = jnp.stack([conv_b, ba, bx, lam, zero, zero, zero, zero])
    od = _rglru(z, conv_w, vecs, wa.astype(BF16), wx.astype(BF16), bsz, seq, (x0 + cw) // dw)

    w_out_b = w_out.astype(BF16)
    return _outproj([oc, od], z, [x0 // cw, (x0 + cw + dw) // dw], [w_out_b[:cw], w_out_b[cw:]],
                    x, p_i, ple_proj.astype(BF16), ple_gate.astype(BF16), ple_norm, final_norm, tm=OUTPROJ_ROWS)


def kernel(x, p, ev_norm, ev_w_in, ev_w_out, hg_lb_logits, hg_norm, rw_mu, rw_w0, rw_w2, rw_a0, rw_a2, rw_k_k, rw_k_a, rw_r_k, rw_ln_g, rw_ln_b, od_norm, od_w_in, od_w_out, lru_conv_w, lru_conv_b, lru_wa, lru_ba, lru_wx, lru_bx, lru_lambda, ple_proj, ple_gate, ple_norm, final_norm):
    bsz, seq, d = x.shape
    depth = p.shape[0]
    s = jax.nn.softmax(hg_lb_logits.astype(F32), axis=0)
    lower_bounds = jnp.maximum(jnp.cumsum(s, axis=0) - s[0], 0.0)
    xf = x.reshape(bsz * seq, d)
    pf = p.reshape(depth, bsz * seq, p.shape[-1])
    for i in range(depth):
        j = i // 2
        fin = final_norm if i == depth - 1 else None
        if i % 2 == 0:
            xf = _even_layer(xf, bsz, seq, ev_norm[j], ev_w_in[j], ev_w_out[j], lower_bounds[j], hg_norm[j],
                             rw_mu[j], rw_w0[j], rw_w2[j], rw_a0[j], rw_a2[j], rw_k_k[j], rw_k_a[j], rw_r_k[j],
                             rw_ln_g[j], rw_ln_b[j], pf[i], ple_proj[i], ple_gate[i], ple_norm[i], fin)
        else:
            xf = _odd_layer(xf, bsz, seq, od_norm[j], od_w_in[j], od_w_out[j], lru_conv_w[j], lru_conv_b[j],
                            lru_wa[j], lru_ba[j], lru_wx[j], lru_bx[j], lru_lambda[j],
                            pf[i], ple_proj[i], ple_gate[i], ple_norm[i], fin)
    return xf.reshape(bsz, seq, d)
```

```python
import functools
import math

import jax
import jax.numpy as jnp
from jax import lax
from jax.experimental import pallas as pl
from jax.experimental.pallas import tpu as pltpu

F32 = jnp.float32
BF16 = jnp.bfloat16

EPS = 1e-6
LOG2E = math.log2(math.e)
LANES = 128
SUBLANES = 8
VMEM_LIMIT = 56 * 1024 * 1024

A_HEADS, A_HEAD_DIM = 8, 128
B_HEADS, B_HEAD_DIM = 16, 64
B_RANK = 64
B_LN_EPS = 64e-5
C_HEADS, C_HEAD_DIM, C_BLOCK, C_TOPK = 8, 64, 256, 3
D_BLOCKS, D_BLOCK_DIM, D_CONV = 8, 128, 4
LRU_C = 8.0
P_DIM = 256

OUTPROJ_ROWS = 512
CHUNK = 64
ROWS = 256
NEG_BIG = -16384.0


def _nt(a, b):
    return lax.dot_general(a, b, (((1,), (1,)), ((), ())), preferred_element_type=F32)


def _tn(a, b):
    return lax.dot_general(a, b, (((0,), (0,)), ((), ())), preferred_element_type=F32)


def _nn(a, b):
    return jnp.dot(a, b, preferred_element_type=F32)


def _split3(x):
    h = x.astype(BF16)
    r = x - h.astype(F32)
    m = r.astype(BF16)
    l = (r - m.astype(F32)).astype(BF16)
    return h, m, l


def _nn_exact_lhs(lhs_bf16, x):
    h, m, l = _split3(x)
    return _nn(lhs_bf16, h) + _nn(lhs_bf16, m) + _nn(lhs_bf16, l)


def _nn_exact_rhs(x, rhs_bf16):
    h, m, l = _split3(x)
    return _nn(h, rhs_bf16) + _nn(m, rhs_bf16) + _nn(l, rhs_bf16)


def _sigmoid(x):
    return 0.5 * jnp.tanh(0.5 * x) + 0.5


def _round_robin(gens, starts=None):
    starts = starts or [0] * len(gens)
    results = [None] * len(gens)
    done = [False] * len(gens)
    rnd = 0
    while not all(done):
        for i in range(len(gens)):
            if done[i] or rnd < starts[i]:
                continue
            try:
                next(gens[i])
            except StopIteration as stop:
                results[i] = stop.value
                done[i] = True
        rnd += 1
    return results


def _params(sem):
    return pltpu.CompilerParams(dimension_semantics=sem, vmem_limit_bytes=VMEM_LIMIT)


def _inproj_kernel(x_ref, g_ref, w_ref, o_ref, h_ref):
    @pl.when(pl.program_id(1) == 0)
    def _():
        x = x_ref[...]
        ms = jnp.mean(x * x, axis=-1, keepdims=True)
        h_ref[...] = (x * lax.rsqrt(ms + EPS) * g_ref[...]).astype(BF16)

    o_ref[...] = _nn(h_ref[...], w_ref[...]).astype(o_ref.dtype)


MXU_COLS = 256
INPROJ_ROWS = 1024


def _inproj(x, g, w, ntiles):
    t, d = x.shape
    tm = INPROJ_ROWS
    quantum = MXU_COLS * ntiles
    n = -(-w.shape[1] // quantum) * quantum
    w = jnp.pad(w, ((0, 0), (0, n - w.shape[1])))
    tn = n // ntiles
    return pl.pallas_call(
        _inproj_kernel,
        grid=(t // tm, n // tn),
        in_specs=[
            pl.BlockSpec((tm, d), lambda i, j: (i, 0)),
            pl.BlockSpec((1, d), lambda i, j: (0, 0)),
            pl.BlockSpec((d, tn), lambda i, j: (0, j)),
        ],
        out_specs=pl.BlockSpec((tm, tn), lambda i, j: (i, j)),
        out_shape=jax.ShapeDtypeStruct((t, n), BF16),
        scratch_shapes=[pltpu.VMEM((tm, d), BF16)],
        compiler_params=_params(("parallel", "arbitrary")),
        name="inproj",
    )(x, g.reshape(1, d), w)


_LEVELS = (1, 2, 4, 8, 16, 32)


def _boundary_bcast(g, m):
    c, w = g.shape
    if m >= SUBLANES:
        gr = g.reshape(c // (2 * m), 2 * m, w)
        b = jnp.broadcast_to(gr[:, m - 1:m, :], gr.shape)
        return b.reshape(c, w)
    gr = g.reshape(c // SUBLANES, SUBLANES, w)
    j = lax.broadcasted_iota(jnp.int32, gr.shape, 1)
    rows = [jnp.broadcast_to(gr[:, r:r + 1, :], gr.shape) for r in range(m - 1, SUBLANES, 2 * m)]
    b = rows[-1]
    for idx in range(len(rows) - 2, -1, -1):
        b = jnp.where(j < (idx + 1) * 2 * m, rows[idx], b)
    return b.reshape(c, w)


def _hgrn2_chains(zq_ref, zf_ref, zi_ref, par_ref, o_ref, st_ref, first_block):
    c = CHUNK

    @pl.when(first_block)
    def _():
        st_ref[...] = jnp.zeros_like(st_ref)

    row = lax.broadcasted_iota(jnp.int32, (c, c), 0)
    col = lax.broadcasted_iota(jnp.int32, (c, c), 1)
    ltri = (col <= row).astype(BF16)
    ltri2 = jnp.concatenate([ltri, ltri], axis=1)
    eye = col == row
    lvl_masks = []
    for m in _LEVELS:
        sh = int(math.log2(2 * m))
        same = (row >> sh) == (col >> sh)
        lvl_masks.append(same & ((row & (2 * m - 1)) >= m) & ((col & (2 * m - 1)) < m))
    rowi = lax.broadcasted_iota(jnp.int32, (c, LANES), 0)

    nseq, nheads = st_ref.shape[0], st_ref.shape[1]
    par_all = par_ref[...]

    def head_chunk(hi, zq, zf, v, st):
        cs = slice(hi * LANES, (hi + 1) * LANES)
        log_lb = par_all[0:1, cs]
        log_omlb = par_all[1:2, cs]
        omlb = par_all[2:3, cs]
        gain = par_all[3:4, cs]

        e = jnp.exp(-jnp.abs(zf))
        log_sig = jnp.minimum(zf, 0.0) - jnp.log(1.0 + e)
        sig_neg = _sigmoid(-zf)
        bterm = log_omlb + log_sig
        mx = jnp.maximum(log_lb, bterm)
        log_f = mx + jnp.log(1.0 + jnp.exp(-jnp.abs(log_lb - bterm)))
        k = omlb * sig_neg
        q = zq * _sigmoid(zq)

        lf2 = log_f * LOG2E
        lf_h = lf2.astype(BF16)
        lf_l = (lf2 - lf_h.astype(F32)).astype(BF16)
        g = _nn(ltri2, jnp.concatenate([lf_h, lf_l], axis=0))
        yield
        g_last = g[c - 1:c, :]

        qb = q.astype(BF16)
        kb = k.astype(BF16)
        vb = v.astype(BF16)
        attn = jnp.where(eye, _nt(qb, kb), 0.0)
        for m, msk in zip(_LEVELS, lvl_masks):
            second = (rowi & (2 * m - 1)) >= m
            if m == 1:
                fac = jnp.where(second, jnp.exp2(lf2), 1.0)
            else:
                fac = jnp.exp2(-jnp.abs(g - _boundary_bcast(g, m)))
            x = (jnp.where(second, q, k) * fac).astype(BF16)
            attn = attn + jnp.where(msk, _nt(x, x), 0.0)
        yield

        o = _nn(attn.astype(BF16), vb) + _nt((q * jnp.exp2(g)).astype(BF16), st.astype(BF16))
        khat = (k * jnp.exp2(g_last - g)).astype(BF16)
        st_new = st * jnp.exp2(g_last) + _tn(vb, khat)
        yield

        ms = jnp.mean(o * o, axis=-1, keepdims=True)
        return o * lax.rsqrt(ms + EPS) * gain, st_new

    def chains(ci):
        sl = pl.ds(pl.multiple_of(ci * c, c), c)
        out = []
        for bi in range(nseq):
            for hi in range(nheads):
                cs = slice(hi * LANES, (hi + 1) * LANES)

                def store(res, bi=bi, hi=hi, cs=cs):
                    o_ref[bi, sl, cs] = res[0]
                    st_ref[bi, hi] = res[1]

                load = lambda ref: ref[bi, sl, cs].astype(F32)
                out.append((head_chunk(hi, load(zq_ref), load(zf_ref), load(zi_ref), st_ref[bi, hi]), store))
        return out

    return chains


_EXP_M05 = math.exp(-0.5)
(_P_MU_R, _P_MU_K, _P_MU_V, _P_W0, _P_A0, _P_KK, _P_KA, _P_RK, _P_LNG, _P_LNB) = range(10)
_RW_NPAR = 16


def _rwkv_chains(zr_ref, zk_ref, zv_ref, zwa_ref, par_ref, muwa_ref, w2_ref, a2_ref, o_ref,
                 st_ref, ext_ref, extwa_ref, first_block):
    c = CHUNK
    hd = B_HEAD_DIM
    nbat, npair = st_ref.shape[0], st_ref.shape[1]

    @pl.when(first_block)
    def _():
        st_ref[...] = jnp.zeros_like(st_ref)
        ext_ref[:, :, 0:SUBLANES, :] = jnp.zeros((nbat, 3, SUBLANES, npair * LANES), F32)
        extwa_ref[:, 0:SUBLANES, :] = jnp.zeros((nbat, SUBLANES, LANES), F32)

    par_all = par_ref[...]

    @pl.when(jnp.logical_not(first_block))
    def _():
        ext_ref[:, :, 0:SUBLANES, :] = ext_ref[:, :, ROWS:SUBLANES + ROWS, :]
        extwa_ref[:, 0:SUBLANES, :] = extwa_ref[:, ROWS:SUBLANES + ROWS, :]

    for bi in range(nbat):
        ext_ref[bi, 0, SUBLANES:SUBLANES + ROWS, :] = zr_ref[bi].astype(F32)
        ext_ref[bi, 1, SUBLANES:SUBLANES + ROWS, :] = zk_ref[bi].astype(F32)
        ext_ref[bi, 2, SUBLANES:SUBLANES + ROWS, :] = zv_ref[bi].astype(F32)
        extwa_ref[bi, SUBLANES:SUBLANES + ROWS, :] = zwa_ref[bi].astype(F32)

    first_row = lax.broadcasted_iota(jnp.int32, (c, LANES), 0) == 0

    def mixed(ext, ci, cs, mu):
        start = pl.multiple_of(ci * c, c) + SUBLANES
        cur = ext[pl.ds(start, c), cs]
        above = ext[pl.ds(start - SUBLANES, SUBLANES), cs][SUBLANES - 1:SUBLANES]
        prev = jnp.where(first_row, above, pltpu.roll(cur, 1, 0))
        return cur + (prev - cur) * mu

    lane = lax.broadcasted_iota(jnp.int32, (c, LANES), 1)
    lo_l = lane < hd
    lane2 = lax.broadcasted_iota(jnp.int32, (2 * c, LANES), 1)
    row2 = lax.broadcasted_iota(jnp.int32, (2 * c, LANES), 0)
    s_idx = lane2 & (c - 1)
    t_idx = row2 & (c - 1)
    tri = (s_idx < t_idx) | ((row2 >= c) & (s_idx == t_idx))
    eye_s = ((lane & (c - 1)) == lax.broadcasted_iota(jnp.int32, (c, LANES), 0)).astype(F32)
    ltri = (lax.broadcasted_iota(jnp.int32, (c, c), 1)
            <= lax.broadcasted_iota(jnp.int32, (c, c), 0)).astype(BF16)
    ltri2 = jnp.concatenate([ltri, ltri], axis=1)
    seg = ((lax.broadcasted_iota(jnp.int32, (LANES, LANES), 0) >> int(math.log2(hd)))
           == (lax.broadcasted_iota(jnp.int32, (LANES, LANES), 1) >> int(math.log2(hd))))
    seg_b = seg.astype(BF16)
    seg2 = jnp.concatenate([seg_b, seg_b], axis=0)

    def split2(x):
        h = x.astype(BF16)
        return h, (x - h.astype(F32)).astype(BF16)

    def segsum(x):
        return _nn(jnp.concatenate(split2(x), axis=1), seg2)

    def by_head(x):
        zero = jnp.zeros_like(x)
        return jnp.concatenate([jnp.where(lo_l, x, zero), jnp.where(lo_l, zero, x)], axis=0)

    def pair_chunk(pi, r, kraw, v, st, w2, a2, wab, twab):
        cs = slice(pi * LANES, (pi + 1) * LANES)
        prow = lambda i: par_all[i:i + 1, cs]

        xw = prow(_P_W0) + _nn(twab, w2)
        g = (-_EXP_M05 * LOG2E) * _sigmoid(xw)
        alpha = _sigmoid(prow(_P_A0) + _nn(wab, a2))
        yield
        kk = kraw * prow(_P_KK)
        k = kraw * (1.0 + (alpha - 1.0) * prow(_P_KA))
        sums = segsum(jnp.concatenate([kk * kk, r * k * prow(_P_RK)], axis=0))
        gc = _nn(ltri2, jnp.concatenate(split2(g), axis=0))
        yield
        kk = kk / jnp.maximum(jnp.sqrt(sums[0:c]), 1e-12)
        rk_sum = sums[c:2 * c]
        bv = kk * alpha
        g_last = gc[c - 1:c, :]
        e_pos = jnp.exp2(gc)
        e_neg = jnp.exp2(-gc)
        e_last = jnp.exp2(g_last)
        at = -kk * jnp.exp2(gc - g)
        rt = r * e_pos
        bt = bv * e_neg
        kt = k * e_neg
        bh = bt * e_last
        kh = kt * e_last

        atb = at.astype(BF16)
        rtb = rt.astype(BF16)
        lhs = jnp.concatenate([atb, rtb], axis=0)
        ab_rb = jnp.where(tri, _nt(lhs, by_head(bt.astype(BF16))), 0.0)
        ak_rk = jnp.where(tri, _nt(lhs, by_head(kt.astype(BF16))), 0.0)
        yield

        stb = st.astype(BF16)
        vh = by_head(v.astype(BF16))
        w = _nt(atb, stb) + _nn(ak_rk[0:c].astype(BF16), vh)
        pw = ab_rb[0:c]
        t_inv = eye_s + pw
        for _ in range(int(math.log2(c)) - 1):
            pwb = pw.astype(BF16)
            pw = _nn(pwb, by_head(pwb))
            yield
            t_inv = t_inv + _nn(pw.astype(BF16), by_head(t_inv.astype(BF16)))
        yield

        u = _nn(t_inv.astype(BF16), by_head(w.astype(BF16)))
        yield
        ub = u.astype(BF16)
        a_r = jnp.concatenate([ab_rb[c:2 * c], ak_rk[c:2 * c]], axis=1).astype(BF16)
        y = _nt(rtb, stb) + _nn(a_r, jnp.concatenate([by_head(ub), vh], axis=0))

        st_new = st * e_last + _tn(jnp.concatenate([ub, v.astype(BF16)], axis=0),
                                   jnp.concatenate([bh.astype(BF16), kh.astype(BF16)], axis=0))
        st_new = jnp.where(seg, st_new, 0.0)
        yield

        inv_n = 1.0 / hd
        moments = segsum(jnp.concatenate([y, y * y], axis=0)) * inv_n
        yield
        mu = moments[0:c]
        var = moments[c:2 * c] - mu * mu
        yn = (y - mu) * lax.rsqrt(var + B_LN_EPS) * prow(_P_LNG) + prow(_P_LNB)
        yn = yn + rk_sum * v
        return yn, st_new

    def chains(ci):
        out = []
        for bi in range(nbat):
            wa = mixed(extwa_ref.at[bi], ci, slice(None), muwa_ref[...])
            wab = wa.astype(BF16)
            twab = jnp.tanh(wa).astype(BF16)
            for pi in range(npair):
                cs = slice(pi * LANES, (pi + 1) * LANES)
                mu_of = lambda i: par_all[i:i + 1, cs]

                def store(res, bi=bi, pi=pi, cs=cs):
                    o_ref[bi, pl.ds(pl.multiple_of(ci * c, c), c), cs] = res[0]
                    st_ref[bi, pi] = res[1]

                out.append((pair_chunk(pi, mixed(ext_ref.at[bi, 0], ci, cs, mu_of(_P_MU_R)),
                                       mixed(ext_ref.at[bi, 1], ci, cs, mu_of(_P_MU_K)),
                                       mixed(ext_ref.at[bi, 2], ci, cs, mu_of(_P_MU_V)),
                                       st_ref[bi, pi], w2_ref[pi], a2_ref[pi], wab, twab), store))
        return out

    return chains


EV_SEQS_PER_STEP = 2
EV_GROUPS = 1
HG_STAGES = 3
RW_STAGES = 13


def _even_mixer_kernel(zq_ref, zf_ref, zi_ref, zr_ref, zk_ref, zv_ref, zwa_ref, hpar_ref, rpar_ref, muwa_ref,
                       w2_ref, a2_ref, oa_ref, ob_ref, hst_ref, rst_ref, ext_ref, extwa_ref):
    first = pl.program_id(1) == 0
    hg = _hgrn2_chains(zq_ref, zf_ref, zi_ref, hpar_ref, oa_ref, hst_ref, first)
    rw = _rwkv_chains(zr_ref, zk_ref, zv_ref, zwa_ref, rpar_ref, muwa_ref, w2_ref, a2_ref, ob_ref,
                      rst_ref, ext_ref, extwa_ref, first)

    def chunk(ci, carry):
        hg_chains = hg(ci)
        rw_chains = rw(ci)
        done = []
        for grp in range(EV_GROUPS):
            hgs = hg_chains[grp::EV_GROUPS]
            rws = rw_chains[grp::EV_GROUPS]
            span = max(RW_STAGES - HG_STAGES, 0)
            starts = [(i * span) // max(len(hgs) - 1, 1) for i in range(len(hgs))] + [0] * len(rws)
            both = hgs + rws
            done += list(zip(both, _round_robin([g for g, _ in both], starts)))
        for (_, store), res in done:
            store(res)
        return carry

    lax.fori_loop(0, ROWS // CHUNK, chunk, 0)


def _even_mixer(z, hpar, rpar, muwa, w2p, a2p, bsz, seq, col_rkv, col_wa):
    t, n = z.shape
    nr = seq // ROWS
    nsq = EV_SEQS_PER_STEP
    aw = A_HEADS * A_HEAD_DIM
    bw = B_HEADS * B_HEAD_DIM
    npair = B_HEADS // 2
    z3 = z.reshape(bsz, seq, n)
    wide = lambda width, col: pl.BlockSpec((nsq, ROWS, width), lambda b, r: (b, r, col // width))
    const = lambda shape: pl.BlockSpec(shape, lambda b, r: (0,) * len(shape))
    oa, ob = pl.pallas_call(
        _even_mixer_kernel,
        grid=(bsz // nsq, nr),
        in_specs=[wide(aw, 0), wide(aw, aw), wide(aw, 2 * aw),
                  wide(bw, col_rkv), wide(bw, col_rkv + bw), wide(bw, col_rkv + 2 * bw),
                  wide(LANES, col_wa),
                  const((SUBLANES, aw)), const((_RW_NPAR, bw)), const((1, LANES)),
                  const((npair, LANES, LANES)), const((npair, LANES, LANES))],
        out_specs=[pl.BlockSpec((nsq, ROWS, aw), lambda b, r: (b, r, 0)),
                   pl.BlockSpec((nsq, ROWS, bw), lambda b, r: (b, r, 0))],
        out_shape=[jax.ShapeDtypeStruct((bsz, seq, aw), F32), jax.ShapeDtypeStruct((bsz, seq, bw), F32)],
        scratch_shapes=[pltpu.VMEM((nsq, A_HEADS, A_HEAD_DIM, A_HEAD_DIM), F32),
                        pltpu.VMEM((nsq, npair, LANES, LANES), F32),
                        pltpu.VMEM((nsq, 3, SUBLANES + ROWS, bw), F32),
                        pltpu.VMEM((nsq, SUBLANES + ROWS, LANES), F32)],
        compiler_params=_params(("parallel", "arbitrary")),
        name="even_mixer",
    )(z3, z3, z3, z3, z3, z3, z3, hpar, rpar, muwa, w2p, a2p)
    return oa.reshape(t, aw), ob.reshape(t, bw)


def _moba_kernel(q_ref, k_ref, v_ref, o_ref, kmean_ref, kb_ref, vt_ref, bias_ref):
    nb = kmean_ref.shape[0]
    blk = C_BLOCK
    hd = C_HEAD_DIM
    n = pl.program_id(2)
    npair = q_ref.shape[1] // LANES

    @pl.when(n == 0)
    def _():
        for j in range(nb):
            kj = k_ref[j * blk:(j + 1) * blk, :]
            kmean_ref[j:j + 1, :] = jnp.mean(kj.astype(F32), axis=0, keepdims=True)
            kb_ref[j] = kj
            for pi in range(npair):
                vj = v_ref[j * blk:(j + 1) * blk, pi * LANES:(pi + 1) * LANES]
                vt_ref[pi, j] = vj.astype(F32).T.astype(BF16)

    lane_m = lax.broadcasted_iota(jnp.int32, (nb, LANES), 1)
    jidx = lax.broadcasted_iota(jnp.int32, (nb, blk), 0)
    valid = jidx < n
    row_d = lax.broadcasted_iota(jnp.int32, (LANES, blk), 0)
    lo_d = row_d < hd
    key_i = lax.broadcasted_iota(jnp.int32, (blk, blk), 0)
    qry_i = lax.broadcasted_iota(jnp.int32, (blk, blk), 1)
    causal = key_i <= qry_i
    qscale = (hd ** -0.5) * math.log2(math.e)

    def select_bias(km, q_t):
        gate = lax.dot_general(km, q_t, (((1,), (0,)), ((), ())), precision=lax.Precision.HIGHEST,
                               preferred_element_type=F32)
        gm = jnp.where(valid, gate, -jnp.inf)
        cnt = jnp.zeros((nb, blk), F32)
        for j2 in range(nb):
            gj = gm[j2:j2 + 1, :]
            better = (gj > gm) | ((gj == gm) & (j2 < jidx))
            cnt = cnt + jnp.where(better, 1.0, 0.0)
        sel = valid & (cnt < float(C_TOPK))
        return jnp.where(sel, 0.0, NEG_BIG)

    qt_heads = []
    for pi in range(npair):
        cs = slice(pi * LANES, (pi + 1) * LANES)
        q_t = q_ref[:, cs].astype(F32).T
        kmean = kmean_ref[:, cs]
        bias_ref[2 * pi] = select_bias(jnp.where(lane_m < hd, kmean, 0.0), q_t)
        bias_ref[2 * pi + 1] = select_bias(jnp.where(lane_m >= hd, kmean, 0.0), q_t)
        qs = q_t * qscale
        qt_heads.append((jnp.where(lo_d, qs, 0.0).astype(BF16), jnp.where(lo_d, 0.0, qs).astype(BF16)))

    nq = blk // MB_QUERY_TILE
    lo_q = lo_d[:, 0:MB_QUERY_TILE]

    def pair_block(pi, qi, j, nkb, prev, own):
        qs = slice(qi * MB_QUERY_TILE, (qi + 1) * MB_QUERY_TILE)
        kj = kb_ref[pl.ds(j, nkb), :, pi * LANES:(pi + 1) * LANES].reshape(nkb * blk, LANES)
        s = [_nn(kj, qt[:, qs]) for qt in qt_heads[pi]]
        yield

        if own:
            s = [jnp.where(causal[:, qs], x, -jnp.inf) for x in s]
        else:
            s = [jnp.concatenate([x[i * blk:(i + 1) * blk] + bias_ref[2 * pi + h, pl.ds(j + i, 1), :][:, qs]
                                  for i in range(nkb)], axis=0) for h, x in enumerate(s)]
        smax = [jnp.max(x, axis=0, keepdims=True) for x in s]
        m_new = smax if own else [jnp.maximum(m, x) for m, x in zip(prev[0], smax)]
        p = [jnp.exp2(x - m) for x, m in zip(s, m_new)]
        psum = [jnp.sum(x, axis=0, keepdims=True) for x in p]
        vt = jnp.concatenate([vt_ref[pi, j + i] for i in range(nkb)], axis=1)
        pv = jnp.concatenate([_nn(vt[0:hd], p[0].astype(BF16)), _nn(vt[hd:], p[1].astype(BF16))], axis=0)
        yield
        if own:
            return m_new, psum, pv
        corr = [jnp.exp2(m - mn) for m, mn in zip(prev[0], m_new)]
        l_new = [l * c + x for l, c, x in zip(prev[1], corr, psum)]
        return m_new, l_new, prev[2] * jnp.where(lo_q, corr[0], corr[1]) + pv

    def visit(j, nkb, carry, own):
        gens = [pair_block(i // nq, i % nq, j, nkb, None if own else carry[i], own) for i in range(npair * nq)]
        return tuple(_round_robin(gens))

    res = visit(n, 1, None, True)
    start = 0
    for size in MB_KEY_TILES:
        count = (n - start) // size
        res = lax.fori_loop(0, count, lambda g, carry, s0=start, sz=size: visit(s0 + g * sz, sz, carry, False), res)
        start = start + count * size
    for pi in range(npair):
        tiles = []
        for qi in range(nq):
            _, l, acc = res[pi * nq + qi]
            tiles.append(acc / jnp.where(lo_q, l[0], l[1]))
        out_t = jnp.concatenate(tiles, axis=1)
        o_ref[:, pi * LANES:(pi + 1) * LANES] = out_t.T


MB_PAIRS_PER_STEP = 4
MB_QUERY_TILE = 256
MB_KEY_TILES = (2, 1)


def _moba(z, bsz, seq, col_q, col_k, col_v):
    t = z.shape[0]
    nb = seq // C_BLOCK
    npp = MB_PAIRS_PER_STEP
    ngrp = (C_HEADS // 2) // npp
    width = npp * LANES
    return pl.pallas_call(
        _moba_kernel,
        grid=(bsz, ngrp, nb),
        in_specs=[pl.BlockSpec((C_BLOCK, width), lambda b, p, n: (b * nb + n, col_q // npp + p)),
                  pl.BlockSpec((seq, width), lambda b, p, n: (b, col_k // npp + p)),
                  pl.BlockSpec((seq, width), lambda b, p, n: (b, col_v // npp + p))],
        out_specs=pl.BlockSpec((C_BLOCK, width), lambda b, p, n: (b * nb + n, p)),
        out_shape=jax.ShapeDtypeStruct((t, C_HEADS * C_HEAD_DIM), F32),
        scratch_shapes=[pltpu.VMEM((nb, width), F32),
                        pltpu.VMEM((nb, C_BLOCK, width), BF16),
                        pltpu.VMEM((npp, nb, LANES, C_BLOCK), BF16),
                        pltpu.VMEM((2 * npp, nb, C_BLOCK), F32)],
        compiler_params=_params(("parallel", "parallel", "arbitrary")),
        name="moba",
    )(z, z, z)


def _rglru_kernel(x_ref, cw_ref, vec_ref, wa_ref, wx_ref, o_ref, ext_ref, a_ref, b_ref, h_ref):
    rows = x_ref.shape[0]
    width = x_ref.shape[1]

    @pl.when(pl.program_id(1) == 0)
    def _():
        ext_ref[0:SUBLANES, :] = jnp.zeros((SUBLANES, width), F32)
        h_ref[...] = jnp.zeros_like(h_ref)

    x = x_ref[...].astype(F32)
    ext_ref[SUBLANES:SUBLANES + rows, :] = x
    xc = vec_ref[0:1, :] + x * cw_ref[D_CONV - 1:D_CONV, :]
    for d in range(1, D_CONV):
        xc = xc + ext_ref[SUBLANES - d:SUBLANES - d + rows, :] * cw_ref[D_CONV - 1 - d:D_CONV - d, :]
    ext_ref[0:SUBLANES, :] = x[rows - SUBLANES:rows, :]

    ba = vec_ref[1:2, :]
    bx = vec_ref[2:3, :]
    lam = vec_ref[3:4, :]
    sp = jnp.maximum(-lam, 0.0) + jnp.log(1.0 + jnp.exp(-jnp.abs(lam)))
    for nblk in range(D_BLOCKS):
        cs = slice(nblk * D_BLOCK_DIM, (nblk + 1) * D_BLOCK_DIM)
        xb = xc[:, cs]
        xbb = xb.astype(BF16)
        rg = _sigmoid(_nn(xbb, wa_ref[nblk]) + ba[:, cs])
        ig = _sigmoid(_nn(xbb, wx_ref[nblk]) + bx[:, cs])
        log_a = -LRU_C * rg * sp[:, cs]
        th = jnp.tanh(log_a)
        one_minus_a2 = -2.0 * th / (1.0 - th)
        a_ref[:, cs] = jnp.exp(log_a)
        b_ref[:, cs] = jnp.sqrt(one_minus_a2) * (ig * xb)

    rowi = lax.broadcasted_iota(jnp.int32, (SUBLANES, width), 0)

    def group(gi, carry):
        sl = pl.ds(pl.multiple_of(gi * SUBLANES, SUBLANES), SUBLANES)
        a = a_ref[sl, :]
        b = b_ref[sl, :]
        for d in (1, 2, 4):
            keep = rowi >= d
            a_sh = jnp.where(keep, pltpu.roll(a, d, 0), 1.0)
            b_sh = jnp.where(keep, pltpu.roll(b, d, 0), 0.0)
            b = a * b_sh + b
            a = a * a_sh
        h = a * carry + b
        o_ref[sl, :] = h
        return jnp.broadcast_to(h[SUBLANES - 1:SUBLANES, :], (SUBLANES, width))

    h_ref[...] = lax.fori_loop(0, rows // SUBLANES, group, h_ref[...])


def _rglru(z, conv_w, vecs, wa, wx, bsz, seq, col_x):
    t = z.shape[0]
    width = D_BLOCKS * D_BLOCK_DIM
    nr = seq // ROWS
    full = lambda shape: pl.BlockSpec(shape, lambda b, r: (0,) * len(shape))
    return pl.pallas_call(
        _rglru_kernel,
        grid=(bsz, nr),
        in_specs=[pl.BlockSpec((ROWS, width), lambda b, r: (b * nr + r, col_x)),
                  full((D_CONV, width)), full((SUBLANES, width)),
                  full((D_BLOCKS, D_BLOCK_DIM, D_BLOCK_DIM)), full((D_BLOCKS, D_BLOCK_DIM, D_BLOCK_DIM))],
        out_specs=pl.BlockSpec((ROWS, width), lambda b, r: (b * nr + r, 0)),
        out_shape=jax.ShapeDtypeStruct((t, width), F32),
        scratch_shapes=[pltpu.VMEM((SUBLANES + ROWS, width), F32),
                        pltpu.VMEM((ROWS, width), F32),
                        pltpu.VMEM((ROWS, width), F32),
                        pltpu.VMEM((SUBLANES, width), F32)],
        compiler_params=_params(("parallel", "arbitrary")),
        name="rglru",
    )(z, conv_w, vecs, wa, wx)


def _outproj_kernel(*refs, nbr, final):
    br = refs[:nbr]
    gt = refs[nbr:2 * nbr]
    ws = refs[2 * nbr:3 * nbr]
    x_ref, p_ref, pp_ref, pg_ref, pn_ref = refs[3 * nbr:3 * nbr + 5]
    rest = refs[3 * nbr + 5:]
    if final:
        fn_ref, o_ref = rest
    else:
        (o_ref,) = rest

    acc = x_ref[...]
    for b, g, w in zip(br, gt, ws):
        gv = g[...].astype(F32)
        y = b[...] * (gv * _sigmoid(gv))
        acc = acc + _nn(y.astype(BF16), w[...])
    ms = jnp.mean(acc * acc, axis=-1, keepdims=True)
    hn = (acc * lax.rsqrt(ms + EPS) * pn_ref[...]).astype(BF16)
    gate = _sigmoid(_nn(hn, pg_ref[...]))
    out = acc + _nn(p_ref[0].astype(BF16), pp_ref[...]) * gate
    if final:
        ms2 = jnp.mean(out * out, axis=-1, keepdims=True)
        out = out * lax.rsqrt(ms2 + EPS) * fn_ref[...]
    o_ref[...] = out


def _outproj(branches, z, gate_cols, w_parts, x, p_i, ple_proj, ple_gate, ple_norm, final_norm, tm):
    t, d = x.shape
    p_all, layer = p_i
    nbr = len(branches)
    in_specs, args = [], []
    for b in branches:
        in_specs.append(pl.BlockSpec((tm, b.shape[1]), lambda i: (i, 0)))
        args.append(b)
    for b, gc in zip(branches, gate_cols):
        in_specs.append(pl.BlockSpec((tm, b.shape[1]), lambda i, gc=gc: (i, gc)))
        args.append(z)
    for w in w_parts:
        in_specs.append(pl.BlockSpec(w.shape, lambda i: (0, 0)))
        args.append(w)
    in_specs += [pl.BlockSpec((tm, d), lambda i: (i, 0)),
                 pl.BlockSpec((1, tm, P_DIM), lambda i: (layer, i, 0)),
                 pl.BlockSpec((P_DIM, d), lambda i: (0, 0)),
                 pl.BlockSpec((d, d), lambda i: (0, 0)),
                 pl.BlockSpec((1, d), lambda i: (0, 0))]
    args += [x, p_all, ple_proj, ple_gate, ple_norm.reshape(1, d)]
    final = final_norm is not None
    if final:
        in_specs.append(pl.BlockSpec((1, d), lambda i: (0, 0)))
        args.append(final_norm.reshape(1, d))
    return pl.pallas_call(
        functools.partial(_outproj_kernel, nbr=nbr, final=final),
        grid=(t // tm,),
        in_specs=in_specs,
        out_specs=pl.BlockSpec((tm, d), lambda i: (i, 0)),
        out_shape=jax.ShapeDtypeStruct((t, d), F32),
        compiler_params=_params(("parallel",)),
        name="outproj",
    )(*args)


def _even_layer(x, bsz, seq, norm_g, w_in, w_out, lb, hg_norm, mu, w0, w2, a0, a2, k_k, k_a, r_k, ln_g, ln_b,
                p_i, ple_proj, ple_gate, ple_norm, final_norm):
    aw = A_HEADS * A_HEAD_DIM
    bw = B_HEADS * B_HEAD_DIM
    rw0 = 3 * aw
    lr0 = rw0 + 3 * bw
    g0 = lr0 + 2 * B_RANK
    w_perm = jnp.concatenate([w_in[:, :lr0], w_in[:, g0:], w_in[:, lr0:g0]], axis=1).astype(BF16)
    z = _inproj(x, norm_g, w_perm, ntiles=3)

    zero = jnp.zeros_like(lb)
    hpar = jnp.stack([jnp.log(lb), jnp.log1p(-lb), 1.0 - lb, hg_norm, zero, zero, zero, zero])

    zrow = jnp.zeros((bw,), F32)
    rows = [mu[0:bw], mu[bw:2 * bw], mu[2 * bw:3 * bw], w0, a0, k_k, k_a, r_k, ln_g, ln_b]
    rpar = jnp.stack(rows + [zrow] * (_RW_NPAR - len(rows)))
    muwa = mu[3 * bw:].reshape(1, 2 * B_RANK)
    npair = B_HEADS // 2
    zpad = jnp.zeros((npair, B_RANK, LANES), F32)
    w2p = jnp.concatenate([w2.reshape(B_RANK, npair, LANES).transpose(1, 0, 2), zpad], axis=1).astype(BF16)
    a2p = jnp.concatenate([zpad, a2.reshape(B_RANK, npair, LANES).transpose(1, 0, 2)], axis=1).astype(BF16)
    oa, ob = _even_mixer(z, hpar, rpar, muwa, w2p, a2p, bsz, seq, rw0, lr0 + aw + bw)

    gate0 = lr0
    w_out_b = w_out.astype(BF16)
    return _outproj([oa, ob], z, [gate0 // aw, (gate0 + aw) // bw], [w_out_b[:aw], w_out_b[aw:]],
                    x, p_i, ple_proj.astype(BF16), ple_gate.astype(BF16), ple_norm, final_norm, tm=OUTPROJ_ROWS)


def _odd_layer(x, bsz, seq, norm_g, w_in, w_out, conv_w, conv_b, wa, ba, wx, bx, lam,
               p_i, ple_proj, ple_gate, ple_norm, final_norm):
    cw = C_HEADS * C_HEAD_DIM
    dw = D_BLOCKS * D_BLOCK_DIM
    x0 = 3 * cw
    g0 = x0 + dw
    w_perm = jnp.concatenate([w_in[:, :x0], w_in[:, g0:g0 + cw], w_in[:, x0:g0], w_in[:, g0 + cw:]],
                             axis=1).astype(BF16)
    z = _inproj(x, norm_g, w_perm, ntiles=2)

    cb = lambda c: c // LANES
    oc = _moba(z, bsz, seq, cb(0), cb(cw), cb(2 * cw))
    zero = jnp.zeros_like(lam)
    vecs = jnp.stack([conv_b, ba, bx, lam, zero, zero, zero, zero])
    od = _rglru(z, conv_w, vecs, wa.astype(BF16), wx.astype(BF16), bsz, seq, (x0 + cw) // dw)

    w_out_b = w_out.astype(BF16)
    return _outproj([oc, od], z, [x0 // cw, (x0 + cw + dw) // dw], [w_out_b[:cw], w_out_b[cw:]],
                    x, p_i, ple_proj.astype(BF16), ple_gate.astype(BF16), ple_norm, final_norm, tm=OUTPROJ_ROWS)


def kernel(x, p, ev_norm, ev_w_in, ev_w_out, hg_lb_logits, hg_norm, rw_mu, rw_w0, rw_w2, rw_a0, rw_a2, rw_k_k, rw_k_a, rw_r_k, rw_ln_g, rw_ln_b, od_norm, od_w_in, od_w_out, lru_conv_w, lru_conv_b, lru_wa, lru_ba, lru_wx, lru_bx, lru_lambda, ple_proj, ple_gate, ple_norm, final_norm):
    bsz, seq, d = x.shape
    depth = p.shape[0]
    s = jax.nn.softmax(hg_lb_logits.astype(F32), axis=0)
    lower_bounds = jnp.maximum(jnp.cumsum(s, axis=0) - s[0], 0.0)
    xf = x.reshape(bsz * seq, d)
    pf = p.reshape(depth, bsz * seq, p.shape[-1])
    for i in range(depth):
        j = i // 2
        fin = final_norm if i == depth - 1 else None
        if i % 2 == 0:
            xf = _even_layer(xf, bsz, seq, ev_norm[j], ev_w_in[j], ev_w_out[j], lower_bounds[j], hg_norm[j],
                             rw_mu[j], rw_w0[j], rw_w2[j], rw_a0[j], rw_a2[j], rw_k_k[j], rw_k_a[j], rw_r_k[j],
                             rw_ln_g[j], rw_ln_b[j], (pf, i), ple_proj[i], ple_gate[i], ple_norm[i], fin)
        else:
            xf = _odd_layer(xf, bsz, seq, od_norm[j], od_w_in[j], od_w_out[j], lru_conv_w[j], lru_conv_b[j],
                            lru_wa[j], lru_ba[j], lru_wx[j], lru_bx[j], lru_lambda[j],
                            (pf, i), ple_proj[i], ple_gate[i], ple_norm[i], fin)
    return xf.reshape(bsz, seq, d)
```

```python
import functools
import math

import jax
import jax.numpy as jnp
from jax import lax
from jax.experimental import pallas as pl
from jax.experimental.pallas import tpu as pltpu

F32 = jnp.float32
BF16 = jnp.bfloat16

EPS = 1e-6
LOG2E = math.log2(math.e)
LANES = 128
SUBLANES = 8
VMEM_LIMIT = 56 * 1024 * 1024

A_HEADS, A_HEAD_DIM = 8, 128
B_HEADS, B_HEAD_DIM = 16, 64
B_RANK = 64
B_LN_EPS = 64e-5
C_HEADS, C_HEAD_DIM, C_BLOCK, C_TOPK = 8, 64, 256, 3
D_BLOCKS, D_BLOCK_DIM, D_CONV = 8, 128, 4
LRU_C = 8.0
P_DIM = 256

OUTPROJ_ROWS = 512
CHUNK = 64
ROWS = 256
NEG_BIG = -16384.0


def _nt(a, b):
    return lax.dot_general(a, b, (((1,), (1,)), ((), ())), preferred_element_type=F32)


def _tn(a, b):
    return lax.dot_general(a, b, (((0,), (0,)), ((), ())), preferred_element_type=F32)


def _nn(a, b):
    return jnp.dot(a, b, preferred_element_type=F32)


def _split3(x):
    h = x.astype(BF16)
    r = x - h.astype(F32)
    m = r.astype(BF16)
    l = (r - m.astype(F32)).astype(BF16)
    return h, m, l


def _nn_exact_lhs(lhs_bf16, x):
    h, m, l = _split3(x)
    return _nn(lhs_bf16, h) + _nn(lhs_bf16, m) + _nn(lhs_bf16, l)


def _nn_exact_rhs(x, rhs_bf16):
    h, m, l = _split3(x)
    return _nn(h, rhs_bf16) + _nn(m, rhs_bf16) + _nn(l, rhs_bf16)


def _sigmoid(x):
    return 0.5 * jnp.tanh(0.5 * x) + 0.5


def _round_robin(gens, starts=None):
    starts = starts or [0] * len(gens)
    results = [None] * len(gens)
    done = [False] * len(gens)
    rnd = 0
    while not all(done):
        for i in range(len(gens)):
            if done[i] or rnd < starts[i]:
                continue
            try:
                next(gens[i])
            except StopIteration as stop:
                results[i] = stop.value
                done[i] = True
        rnd += 1
    return results


def _params(sem):
    return pltpu.CompilerParams(dimension_semantics=sem, vmem_limit_bytes=VMEM_LIMIT)


def _inproj_kernel(x_ref, g_ref, w_ref, o_ref, h_ref):
    @pl.when(pl.program_id(1) == 0)
    def _():
        x = x_ref[...]
        ms = jnp.mean(x * x, axis=-1, keepdims=True)
        h_ref[...] = (x * lax.rsqrt(ms + EPS) * g_ref[...]).astype(BF16)

    o_ref[...] = _nn(h_ref[...], w_ref[...]).astype(o_ref.dtype)


MXU_COLS = 256
INPROJ_ROWS = 1024


def _inproj(x, g, w, ntiles):
    t, d = x.shape
    tm = INPROJ_ROWS
    quantum = MXU_COLS * ntiles
    n = -(-w.shape[1] // quantum) * quantum
    w = jnp.pad(w, ((0, 0), (0, n - w.shape[1])))
    tn = n // ntiles
    return pl.pallas_call(
        _inproj_kernel,
        grid=(t // tm, n // tn),
        in_specs=[
            pl.BlockSpec((tm, d), lambda i, j: (i, 0)),
            pl.BlockSpec((1, d), lambda i, j: (0, 0)),
            pl.BlockSpec((d, tn), lambda i, j: (0, j)),
        ],
        out_specs=pl.BlockSpec((tm, tn), lambda i, j: (i, j)),
        out_shape=jax.ShapeDtypeStruct((t, n), BF16),
        scratch_shapes=[pltpu.VMEM((tm, d), BF16)],
        compiler_params=_params(("parallel", "arbitrary")),
        name="inproj",
    )(x, g.reshape(1, d), w)


_LEVELS = (1, 2, 4, 8, 16, 32)


def _boundary_bcast(g, m):
    c, w = g.shape
    if m >= SUBLANES:
        gr = g.reshape(c // (2 * m), 2 * m, w)
        b = jnp.broadcast_to(gr[:, m - 1:m, :], gr.shape)
        return b.reshape(c, w)
    gr = g.reshape(c // SUBLANES, SUBLANES, w)
    j = lax.broadcasted_iota(jnp.int32, gr.shape, 1)
    rows = [jnp.broadcast_to(gr[:, r:r + 1, :], gr.shape) for r in range(m - 1, SUBLANES, 2 * m)]
    b = rows[-1]
    for idx in range(len(rows) - 2, -1, -1):
        b = jnp.where(j < (idx + 1) * 2 * m, rows[idx], b)
    return b.reshape(c, w)


def _hgrn2_chains(zq_ref, zf_ref, zi_ref, par_ref, o_ref, st_ref, first_block):
    c = CHUNK

    @pl.when(first_block)
    def _():
        st_ref[...] = jnp.zeros_like(st_ref)

    row = lax.broadcasted_iota(jnp.int32, (c, c), 0)
    col = lax.broadcasted_iota(jnp.int32, (c, c), 1)
    ltri = (col <= row).astype(BF16)
    ltri2 = jnp.concatenate([ltri, ltri], axis=1)
    eye = col == row
    lvl_masks = []
    for m in _LEVELS:
        sh = int(math.log2(2 * m))
        same = (row >> sh) == (col >> sh)
        lvl_masks.append(same & ((row & (2 * m - 1)) >= m) & ((col & (2 * m - 1)) < m))
    rowi = lax.broadcasted_iota(jnp.int32, (c, LANES), 0)

    nseq, nheads = st_ref.shape[0], st_ref.shape[1]
    par_all = par_ref[...]

    def head_chunk(hi, zq, zf, v, st):
        cs = slice(hi * LANES, (hi + 1) * LANES)
        log_lb = par_all[0:1, cs]
        log_omlb = par_all[1:2, cs]
        omlb = par_all[2:3, cs]
        gain = par_all[3:4, cs]

        e = jnp.exp(-jnp.abs(zf))
        log_sig = jnp.minimum(zf, 0.0) - jnp.log(1.0 + e)
        sig_neg = _sigmoid(-zf)
        bterm = log_omlb + log_sig
        mx = jnp.maximum(log_lb, bterm)
        log_f = mx + jnp.log(1.0 + jnp.exp(-jnp.abs(log_lb - bterm)))
        k = omlb * sig_neg
        q = zq * _sigmoid(zq)

        lf2 = log_f * LOG2E
        lf_h = lf2.astype(BF16)
        lf_l = (lf2 - lf_h.astype(F32)).astype(BF16)
        g = _nn(ltri2, jnp.concatenate([lf_h, lf_l], axis=0))
        yield
        g_last = g[c - 1:c, :]

        qb = q.astype(BF16)
        kb = k.astype(BF16)
        vb = v.astype(BF16)
        attn = jnp.where(eye, _nt(qb, kb), 0.0)
        for m, msk in zip(_LEVELS, lvl_masks):
            second = (rowi & (2 * m - 1)) >= m
            if m == 1:
                fac = jnp.where(second, jnp.exp2(lf2), 1.0)
            else:
                fac = jnp.exp2(-jnp.abs(g - _boundary_bcast(g, m)))
            x = (jnp.where(second, q, k) * fac).astype(BF16)
            attn = attn + jnp.where(msk, _nt(x, x), 0.0)
        yield

        o = _nn(attn.astype(BF16), vb) + _nt((q * jnp.exp2(g)).astype(BF16), st.astype(BF16))
        khat = (k * jnp.exp2(g_last - g)).astype(BF16)
        st_new = st * jnp.exp2(g_last) + _tn(vb, khat)
        yield

        ms = jnp.mean(o * o, axis=-1, keepdims=True)
        return o * lax.rsqrt(ms + EPS) * gain, st_new

    def chains(ci):
        sl = pl.ds(pl.multiple_of(ci * c, c), c)
        out = []
        for bi in range(nseq):
            for hi in range(nheads):
                cs = slice(hi * LANES, (hi + 1) * LANES)

                def store(res, bi=bi, hi=hi, cs=cs):
                    o_ref[bi, sl, cs] = res[0]
                    st_ref[bi, hi] = res[1]

                load = lambda ref: ref[bi, sl, cs].astype(F32)
                out.append((head_chunk(hi, load(zq_ref), load(zf_ref), load(zi_ref), st_ref[bi, hi]), store))
        return out

    return chains


_EXP_M05 = math.exp(-0.5)
(_P_MU_R, _P_MU_K, _P_MU_V, _P_W0, _P_A0, _P_KK, _P_KA, _P_RK, _P_LNG, _P_LNB) = range(10)
_RW_NPAR = 16


def _rwkv_chains(zr_ref, zk_ref, zv_ref, zwa_ref, par_ref, muwa_ref, w2_ref, a2_ref, o_ref,
                 st_ref, ext_ref, extwa_ref, first_block):
    c = CHUNK
    hd = B_HEAD_DIM
    nbat, npair = st_ref.shape[0], st_ref.shape[1]

    @pl.when(first_block)
    def _():
        st_ref[...] = jnp.zeros_like(st_ref)
        ext_ref[:, :, 0:SUBLANES, :] = jnp.zeros((nbat, 3, SUBLANES, npair * LANES), F32)
        extwa_ref[:, 0:SUBLANES, :] = jnp.zeros((nbat, SUBLANES, LANES), F32)

    par_all = par_ref[...]

    @pl.when(jnp.logical_not(first_block))
    def _():
        ext_ref[:, :, 0:SUBLANES, :] = ext_ref[:, :, ROWS:SUBLANES + ROWS, :]
        extwa_ref[:, 0:SUBLANES, :] = extwa_ref[:, ROWS:SUBLANES + ROWS, :]

    for bi in range(nbat):
        ext_ref[bi, 0, SUBLANES:SUBLANES + ROWS, :] = zr_ref[bi].astype(F32)
        ext_ref[bi, 1, SUBLANES:SUBLANES + ROWS, :] = zk_ref[bi].astype(F32)
        ext_ref[bi, 2, SUBLANES:SUBLANES + ROWS, :] = zv_ref[bi].astype(F32)
        extwa_ref[bi, SUBLANES:SUBLANES + ROWS, :] = zwa_ref[bi].astype(F32)

    first_row = lax.broadcasted_iota(jnp.int32, (c, LANES), 0) == 0

    def mixed(ext, ci, cs, mu):
        start = pl.multiple_of(ci * c, c) + SUBLANES
        cur = ext[pl.ds(start, c), cs]
        above = ext[pl.ds(start - SUBLANES, SUBLANES), cs][SUBLANES - 1:SUBLANES]
        prev = jnp.where(first_row, above, pltpu.roll(cur, 1, 0))
        return cur + (prev - cur) * mu

    lane = lax.broadcasted_iota(jnp.int32, (c, LANES), 1)
    lo_l = lane < hd
    lane2 = lax.broadcasted_iota(jnp.int32, (2 * c, LANES), 1)
    row2 = lax.broadcasted_iota(jnp.int32, (2 * c, LANES), 0)
    s_idx = lane2 & (c - 1)
    t_idx = row2 & (c - 1)
    tri = (s_idx < t_idx) | ((row2 >= c) & (s_idx == t_idx))
    eye_s = ((lane & (c - 1)) == lax.broadcasted_iota(jnp.int32, (c, LANES), 0)).astype(F32)
    ltri = (lax.broadcasted_iota(jnp.int32, (c, c), 1)
            <= lax.broadcasted_iota(jnp.int32, (c, c), 0)).astype(BF16)
    ltri2 = jnp.concatenate([ltri, ltri], axis=1)
    seg = ((lax.broadcasted_iota(jnp.int32, (LANES, LANES), 0) >> int(math.log2(hd)))
           == (lax.broadcasted_iota(jnp.int32, (LANES, LANES), 1) >> int(math.log2(hd))))
    seg_b = seg.astype(BF16)
    seg2 = jnp.concatenate([seg_b, seg_b], axis=0)

    def split2(x):
        h = x.astype(BF16)
        return h, (x - h.astype(F32)).astype(BF16)

    def segsum(x):
        return _nn(jnp.concatenate(split2(x), axis=1), seg2)

    def by_head(x):
        zero = jnp.zeros_like(x)
        return jnp.concatenate([jnp.where(lo_l, x, zero), jnp.where(lo_l, zero, x)], axis=0)

    def pair_chunk(pi, r, kraw, v, st, w2, a2, wab, twab):
        cs = slice(pi * LANES, (pi + 1) * LANES)
        prow = lambda i: par_all[i:i + 1, cs]

        xw = prow(_P_W0) + _nn(twab, w2)
        g = (-_EXP_M05 * LOG2E) * _sigmoid(xw)
        alpha = _sigmoid(prow(_P_A0) + _nn(wab, a2))
        yield
        kk = kraw * prow(_P_KK)
        k = kraw * (1.0 + (alpha - 1.0) * prow(_P_KA))
        sums = segsum(jnp.concatenate([kk * kk, r * k * prow(_P_RK)], axis=0))
        gc = _nn(ltri2, jnp.concatenate(split2(g), axis=0))
        yield
        kk = kk / jnp.maximum(jnp.sqrt(sums[0:c]), 1e-12)
        rk_sum = sums[c:2 * c]
        bv = kk * alpha
        g_last = gc[c - 1:c, :]
        e_pos = jnp.exp2(gc)
        e_neg = jnp.exp2(-gc)
        e_last = jnp.exp2(g_last)
        at = -kk * jnp.exp2(gc - g)
        rt = r * e_pos
        bt = bv * e_neg
        kt = k * e_neg
        bh = bt * e_last
        kh = kt * e_last

        atb = at.astype(BF16)
        rtb = rt.astype(BF16)
        lhs = jnp.concatenate([atb, rtb], axis=0)
        ab_rb = jnp.where(tri, _nt(lhs, by_head(bt.astype(BF16))), 0.0)
        ak_rk = jnp.where(tri, _nt(lhs, by_head(kt.astype(BF16))), 0.0)
        yield

        stb = st.astype(BF16)
        vh = by_head(v.astype(BF16))
        w = _nt(atb, stb) + _nn(ak_rk[0:c].astype(BF16), vh)
        pw = ab_rb[0:c]
        t_inv = eye_s + pw
        for _ in range(int(math.log2(c)) - 1):
            pwb = pw.astype(BF16)
            pw = _nn(pwb, by_head(pwb))
            yield
            t_inv = t_inv + _nn(pw.astype(BF16), by_head(t_inv.astype(BF16)))
        yield

        u = _nn(t_inv.astype(BF16), by_head(w.astype(BF16)))
        yield
        ub = u.astype(BF16)
        a_r = jnp.concatenate([ab_rb[c:2 * c], ak_rk[c:2 * c]], axis=1).astype(BF16)
        y = _nt(rtb, stb) + _nn(a_r, jnp.concatenate([by_head(ub), vh], axis=0))

        st_new = st * e_last + _tn(jnp.concatenate([ub, v.astype(BF16)], axis=0),
                                   jnp.concatenate([bh.astype(BF16), kh.astype(BF16)], axis=0))
        st_new = jnp.where(seg, st_new, 0.0)
        yield

        inv_n = 1.0 / hd
        moments = segsum(jnp.concatenate([y, y * y], axis=0)) * inv_n
        yield
        mu = moments[0:c]
        var = moments[c:2 * c] - mu * mu
        yn = (y - mu) * lax.rsqrt(var + B_LN_EPS) * prow(_P_LNG) + prow(_P_LNB)
        yn = yn + rk_sum * v
        return yn, st_new

    def chains(ci):
        out = []
        for bi in range(nbat):
            wa = mixed(extwa_ref.at[bi], ci, slice(None), muwa_ref[...])
            wab = wa.astype(BF16)
            twab = jnp.tanh(wa).astype(BF16)
            for pi in range(npair):
                cs = slice(pi * LANES, (pi + 1) * LANES)
                mu_of = lambda i: par_all[i:i + 1, cs]

                def store(res, bi=bi, pi=pi, cs=cs):
                    o_ref[bi, pl.ds(pl.multiple_of(ci * c, c), c), cs] = res[0]
                    st_ref[bi, pi] = res[1]

                out.append((pair_chunk(pi, mixed(ext_ref.at[bi, 0], ci, cs, mu_of(_P_MU_R)),
                                       mixed(ext_ref.at[bi, 1], ci, cs, mu_of(_P_MU_K)),
                                       mixed(ext_ref.at[bi, 2], ci, cs, mu_of(_P_MU_V)),
                                       st_ref[bi, pi], w2_ref[pi], a2_ref[pi], wab, twab), store))
        return out

    return chains


EV_SEQS_PER_STEP = 2
EV_GROUPS = 1
HG_STAGES = 3
RW_STAGES = 13


def _even_mixer_kernel(zq_ref, zf_ref, zi_ref, zr_ref, zk_ref, zv_ref, zwa_ref, hpar_ref, rpar_ref, muwa_ref,
                       w2_ref, a2_ref, oa_ref, ob_ref, hst_ref, rst_ref, ext_ref, extwa_ref):
    first = pl.program_id(1) == 0
    hg = _hgrn2_chains(zq_ref, zf_ref, zi_ref, hpar_ref, oa_ref, hst_ref, first)
    rw = _rwkv_chains(zr_ref, zk_ref, zv_ref, zwa_ref, rpar_ref, muwa_ref, w2_ref, a2_ref, ob_ref,
                      rst_ref, ext_ref, extwa_ref, first)

    def chunk(ci, carry):
        hg_chains = hg(ci)
        rw_chains = rw(ci)
        done = []
        for grp in range(EV_GROUPS):
            hgs = hg_chains[grp::EV_GROUPS]
            rws = rw_chains[grp::EV_GROUPS]
            span = max(RW_STAGES - HG_STAGES, 0)
            starts = [(i * span) // max(len(hgs) - 1, 1) for i in range(len(hgs))] + [0] * len(rws)
            both = hgs + rws
            done += list(zip(both, _round_robin([g for g, _ in both], starts)))
        for (_, store), res in done:
            store(res)
        return carry

    lax.fori_loop(0, ROWS // CHUNK, chunk, 0)


def _even_mixer(z, hpar, rpar, muwa, w2p, a2p, bsz, seq, col_rkv, col_wa):
    t, n = z.shape
    nr = seq // ROWS
    nsq = EV_SEQS_PER_STEP
    aw = A_HEADS * A_HEAD_DIM
    bw = B_HEADS * B_HEAD_DIM
    npair = B_HEADS // 2
    z3 = z.reshape(bsz, seq, n)
    wide = lambda width, col: pl.BlockSpec((nsq, ROWS, width), lambda b, r: (b, r, col // width))
    const = lambda shape: pl.BlockSpec(shape, lambda b, r: (0,) * len(shape))
    oa, ob = pl.pallas_call(
        _even_mixer_kernel,
        grid=(bsz // nsq, nr),
        in_specs=[wide(aw, 0), wide(aw, aw), wide(aw, 2 * aw),
                  wide(bw, col_rkv), wide(bw, col_rkv + bw), wide(bw, col_rkv + 2 * bw),
                  wide(LANES, col_wa),
                  const((SUBLANES, aw)), const((_RW_NPAR, bw)), const((1, LANES)),
                  const((npair, LANES, LANES)), const((npair, LANES, LANES))],
        out_specs=[pl.BlockSpec((nsq, ROWS, aw), lambda b, r: (b, r, 0)),
                   pl.BlockSpec((nsq, ROWS, bw), lambda b, r: (b, r, 0))],
        out_shape=[jax.ShapeDtypeStruct((bsz, seq, aw), F32), jax.ShapeDtypeStruct((bsz, seq, bw), F32)],
        scratch_shapes=[pltpu.VMEM((nsq, A_HEADS, A_HEAD_DIM, A_HEAD_DIM), F32),
                        pltpu.VMEM((nsq, npair, LANES, LANES), F32),
                        pltpu.VMEM((nsq, 3, SUBLANES + ROWS, bw), F32),
                        pltpu.VMEM((nsq, SUBLANES + ROWS, LANES), F32)],
        compiler_params=_params(("parallel", "arbitrary")),
        name="even_mixer",
    )(z3, z3, z3, z3, z3, z3, z3, hpar, rpar, muwa, w2p, a2p)
    return oa.reshape(t, aw), ob.reshape(t, bw)


def _moba_body(q_ref, k_ref, v_ref, o_ref, kmean_ref, kb_ref, vt_ref, bias_ref, n, straight_line_extra):
    nb = kmean_ref.shape[0]
    blk = C_BLOCK
    hd = C_HEAD_DIM
    npair = q_ref.shape[1] // LANES

    @pl.when(n == 0)
    def _():
        for j in range(nb):
            kj = k_ref[j * blk:(j + 1) * blk, :]
            kmean_ref[j:j + 1, :] = jnp.mean(kj.astype(F32), axis=0, keepdims=True)
            kb_ref[j] = kj
            for pi in range(npair):
                vj = v_ref[j * blk:(j + 1) * blk, pi * LANES:(pi + 1) * LANES]
                vt_ref[pi, j] = vj.astype(F32).T.astype(BF16)

    straight_line_extra()

    lane_m = lax.broadcasted_iota(jnp.int32, (nb, LANES), 1)
    jidx = lax.broadcasted_iota(jnp.int32, (nb, blk), 0)
    valid = jidx < n
    row_d = lax.broadcasted_iota(jnp.int32, (LANES, blk), 0)
    lo_d = row_d < hd
    key_i = lax.broadcasted_iota(jnp.int32, (blk, blk), 0)
    qry_i = lax.broadcasted_iota(jnp.int32, (blk, blk), 1)
    causal = key_i <= qry_i
    qscale = (hd ** -0.5) * math.log2(math.e)

    def select_bias(km, q_t):
        gate = lax.dot_general(km, q_t, (((1,), (0,)), ((), ())), precision=lax.Precision.HIGHEST,
                               preferred_element_type=F32)
        gm = jnp.where(valid, gate, -jnp.inf)
        cnt = jnp.zeros((nb, blk), F32)
        for j2 in range(nb):
            gj = gm[j2:j2 + 1, :]
            better = (gj > gm) | ((gj == gm) & (j2 < jidx))
            cnt = cnt + jnp.where(better, 1.0, 0.0)
        sel = valid & (cnt < float(C_TOPK))
        return jnp.where(sel, 0.0, NEG_BIG)

    qt_heads = []
    for pi in range(npair):
        cs = slice(pi * LANES, (pi + 1) * LANES)
        q_t = q_ref[:, cs].astype(F32).T
        kmean = kmean_ref[:, cs]
        bias_ref[2 * pi] = select_bias(jnp.where(lane_m < hd, kmean, 0.0), q_t)
        bias_ref[2 * pi + 1] = select_bias(jnp.where(lane_m >= hd, kmean, 0.0), q_t)
        qs = q_t * qscale
        qt_heads.append((jnp.where(lo_d, qs, 0.0).astype(BF16), jnp.where(lo_d, 0.0, qs).astype(BF16)))

    nq = blk // MB_QUERY_TILE
    lo_q = lo_d[:, 0:MB_QUERY_TILE]

    def pair_block(pi, qi, blocks, prev, own_first):
        qs = slice(qi * MB_QUERY_TILE, (qi + 1) * MB_QUERY_TILE)
        kj = jnp.concatenate([kb_ref[j, :, pi * LANES:(pi + 1) * LANES] for j in blocks], axis=0)
        s = [_nn(kj, qt[:, qs]) for qt in qt_heads[pi]]
        yield

        def masked(x, h, i):
            part = x[i * blk:(i + 1) * blk]
            if own_first and i == 0:
                return jnp.where(causal[:, qs], part, -jnp.inf)
            return part + bias_ref[2 * pi + h, pl.ds(blocks[i], 1), :][:, qs]

        s = [jnp.concatenate([masked(x, h, i) for i in range(len(blocks))], axis=0) for h, x in enumerate(s)]
        smax = [jnp.max(x, axis=0, keepdims=True) for x in s]
        m_new = smax if own_first else [jnp.maximum(m, x) for m, x in zip(prev[0], smax)]
        p = [jnp.exp2(x - m) for x, m in zip(s, m_new)]
        psum = [jnp.sum(x, axis=0, keepdims=True) for x in p]
        vt = jnp.concatenate([vt_ref[pi, j] for j in blocks], axis=1)
        pv = jnp.concatenate([_nn(vt[0:hd], p[0].astype(BF16)), _nn(vt[hd:], p[1].astype(BF16))], axis=0)
        yield
        if own_first:
            return m_new, psum, pv
        corr = [jnp.exp2(m - mn) for m, mn in zip(prev[0], m_new)]
        l_new = [l * c + x for l, c, x in zip(prev[1], corr, psum)]
        return m_new, l_new, prev[2] * jnp.where(lo_q, corr[0], corr[1]) + pv

    def visit(blocks, carry, own_first):
        gens = [pair_block(i // nq, i % nq, blocks, None if own_first else carry[i], own_first)
                for i in range(npair * nq)]
        return tuple(_round_robin(gens))

    rest = jnp.maximum(n - 1, 0)
    res = visit([n, rest], None, True)
    start = 0
    for size in MB_KEY_TILES:
        count = (rest - start) // size
        res = lax.fori_loop(
            0, count, lambda g, carry, s0=start, sz=size: visit([s0 + g * sz + i for i in range(sz)], carry, False), res)
        start = start + count * size
    for pi in range(npair):
        tiles = []
        for qi in range(nq):
            _, l, acc = res[pi * nq + qi]
            tiles.append(acc / jnp.where(lo_q, l[0], l[1]))
        out_t = jnp.concatenate(tiles, axis=1)
        o_ref[:, pi * LANES:(pi + 1) * LANES] = out_t.T


MB_QUERY_TILE = 256
MB_KEY_TILES = (2, 1)


def _rglru_init(ext_ref, h_ref, first_block):
    @pl.when(first_block)
    def _():
        ext_ref[0:SUBLANES, :] = jnp.zeros((SUBLANES, ext_ref.shape[1]), F32)
        h_ref[...] = jnp.zeros_like(h_ref)


def _rglru_gates(x_ref, cw_ref, vec_ref, wa_ref, wx_ref, ext_ref, a_ref, b_ref):
    rows = x_ref.shape[0]

    x = x_ref[...].astype(F32)
    ext_ref[SUBLANES:SUBLANES + rows, :] = x
    ext = ext_ref[...]
    acc = ext * cw_ref[0:1, :]
    for i in range(1, D_CONV):
        acc = ext * cw_ref[i:i + 1, :] + pltpu.roll(acc, 1, 0)
    xc = acc[SUBLANES:SUBLANES + rows] + vec_ref[0:1, :]
    ext_ref[0:SUBLANES, :] = x[rows - SUBLANES:rows, :]

    ba = vec_ref[1:2, :]
    bx = vec_ref[2:3, :]
    lam = vec_ref[3:4, :]
    sp = jnp.maximum(-lam, 0.0) + jnp.log(1.0 + jnp.exp(-jnp.abs(lam)))
    for nblk in range(D_BLOCKS):
        cs = slice(nblk * D_BLOCK_DIM, (nblk + 1) * D_BLOCK_DIM)
        xb = xc[:, cs]
        xbb = xb.astype(BF16)
        rg = _sigmoid(_nn(xbb, wa_ref[nblk]) + ba[:, cs])
        ig = _sigmoid(_nn(xbb, wx_ref[nblk]) + bx[:, cs])
        log_a = -LRU_C * rg * sp[:, cs]
        th = jnp.tanh(log_a)
        one_minus_a2 = -2.0 * th / (1.0 - th)
        a_ref[:, cs] = jnp.exp(log_a)
        b_ref[:, cs] = jnp.sqrt(one_minus_a2) * (ig * xb)


def _rglru_scan(o_ref, a_ref, b_ref, h_ref):
    rows, width = a_ref.shape
    rowi = lax.broadcasted_iota(jnp.int32, (SUBLANES, width), 0)

    def group(gi, carry):
        sl = pl.ds(pl.multiple_of(gi * SUBLANES, SUBLANES), SUBLANES)
        a = a_ref[sl, :]
        b = b_ref[sl, :]
        for d in (1, 2, 4):
            keep = rowi >= d
            a_sh = jnp.where(keep, pltpu.roll(a, d, 0), 1.0)
            b_sh = jnp.where(keep, pltpu.roll(b, d, 0), 0.0)
            b = a * b_sh + b
            a = a * a_sh
        h = a * carry + b
        o_ref[sl, :] = h
        return jnp.broadcast_to(h[SUBLANES - 1:SUBLANES, :], (SUBLANES, width))

    h_ref[...] = lax.fori_loop(0, rows // SUBLANES, group, h_ref[...], unroll=4)


def _odd_mixer_kernel(q_ref, k_ref, v_ref, x_ref, cw_ref, vec_ref, wa_ref, wx_ref, oc_ref, od_ref,
                      kmean_ref, kb_ref, vt_ref, bias_ref, ext_ref, a_ref, b_ref, h_ref):
    n = pl.program_id(1)
    _rglru_init(ext_ref, h_ref, n == 0)
    _moba_body(q_ref, k_ref, v_ref, oc_ref, kmean_ref, kb_ref, vt_ref, bias_ref, n,
               functools.partial(_rglru_gates, x_ref, cw_ref, vec_ref, wa_ref, wx_ref, ext_ref, a_ref, b_ref))
    _rglru_scan(od_ref, a_ref, b_ref, h_ref)


def _odd_mixer(z, conv_w, vecs, wa, wx, bsz, seq, col_q, col_k, col_v, col_x):
    assert C_BLOCK == ROWS and seq // C_BLOCK >= 2
    t = z.shape[0]
    nb = seq // C_BLOCK
    npp = C_HEADS // 2
    cw = C_HEADS * C_HEAD_DIM
    dw = D_BLOCKS * D_BLOCK_DIM
    full = lambda shape: pl.BlockSpec(shape, lambda b, n: (0,) * len(shape))
    return pl.pallas_call(
        _odd_mixer_kernel,
        grid=(bsz, nb),
        in_specs=[pl.BlockSpec((C_BLOCK, cw), lambda b, n: (b * nb + n, col_q // cw)),
                  pl.BlockSpec((seq, cw), lambda b, n: (b, col_k // cw)),
                  pl.BlockSpec((seq, cw), lambda b, n: (b, col_v // cw)),
                  pl.BlockSpec((ROWS, dw), lambda b, n: (b * nb + n, col_x // dw)),
                  full((D_CONV, dw)), full((SUBLANES, dw)),
                  full((D_BLOCKS, D_BLOCK_DIM, D_BLOCK_DIM)), full((D_BLOCKS, D_BLOCK_DIM, D_BLOCK_DIM))],
        out_specs=[pl.BlockSpec((C_BLOCK, cw), lambda b, n: (b * nb + n, 0)),
                   pl.BlockSpec((ROWS, dw), lambda b, n: (b * nb + n, 0))],
        out_shape=[jax.ShapeDtypeStruct((t, cw), F32), jax.ShapeDtypeStruct((t, dw), F32)],
        scratch_shapes=[pltpu.VMEM((nb, cw), F32),
                        pltpu.VMEM((nb, C_BLOCK, cw), BF16),
                        pltpu.VMEM((npp, nb, LANES, C_BLOCK), BF16),
                        pltpu.VMEM((2 * npp, nb, C_BLOCK), F32),
                        pltpu.VMEM((SUBLANES + ROWS, dw), F32),
                        pltpu.VMEM((ROWS, dw), F32),
                        pltpu.VMEM((ROWS, dw), F32),
                        pltpu.VMEM((SUBLANES, dw), F32)],
        compiler_params=_params(("parallel", "arbitrary")),
        name="odd_mixer",
    )(z, z, z, z, conv_w, vecs, wa, wx)


def _outproj_kernel(*refs, nbr, final):
    br = refs[:nbr]
    gt = refs[nbr:2 * nbr]
    ws = refs[2 * nbr:3 * nbr]
    x_ref, p_ref, pp_ref, pg_ref, pn_ref = refs[3 * nbr:3 * nbr + 5]
    rest = refs[3 * nbr + 5:]
    if final:
        fn_ref, o_ref = rest
    else:
        (o_ref,) = rest

    acc = x_ref[...]
    for b, g, w in zip(br, gt, ws):
        gv = g[...].astype(F32)
        y = b[...] * (gv * _sigmoid(gv))
        acc = acc + _nn(y.astype(BF16), w[...])
    ms = jnp.mean(acc * acc, axis=-1, keepdims=True)
    hn = (acc * lax.rsqrt(ms + EPS) * pn_ref[...]).astype(BF16)
    gate = _sigmoid(_nn(hn, pg_ref[...]))
    out = acc + _nn(p_ref[0].astype(BF16), pp_ref[...]) * gate
    if final:
        ms2 = jnp.mean(out * out, axis=-1, keepdims=True)
        out = out * lax.rsqrt(ms2 + EPS) * fn_ref[...]
    o_ref[...] = out


def _outproj(branches, z, gate_cols, w_parts, x, p_i, ple_proj, ple_gate, ple_norm, final_norm, tm):
    t, d = x.shape
    p_all, layer = p_i
    nbr = len(branches)
    in_specs, args = [], []
    for b in branches:
        in_specs.append(pl.BlockSpec((tm, b.shape[1]), lambda i: (i, 0)))
        args.append(b)
    for b, gc in zip(branches, gate_cols):
        in_specs.append(pl.BlockSpec((tm, b.shape[1]), lambda i, gc=gc: (i, gc)))
        args.append(z)
    for w in w_parts:
        in_specs.append(pl.BlockSpec(w.shape, lambda i: (0, 0)))
        args.append(w)
    in_specs += [pl.BlockSpec((tm, d), lambda i: (i, 0)),
                 pl.BlockSpec((1, tm, P_DIM), lambda i: (layer, i, 0)),
                 pl.BlockSpec((P_DIM, d), lambda i: (0, 0)),
                 pl.BlockSpec((d, d), lambda i: (0, 0)),
                 pl.BlockSpec((1, d), lambda i: (0, 0))]
    args += [x, p_all, ple_proj, ple_gate, ple_norm.reshape(1, d)]
    final = final_norm is not None
    if final:
        in_specs.append(pl.BlockSpec((1, d), lambda i: (0, 0)))
        args.append(final_norm.reshape(1, d))
    return pl.pallas_call(
        functools.partial(_outproj_kernel, nbr=nbr, final=final),
        grid=(t // tm,),
        in_specs=in_specs,
        out_specs=pl.BlockSpec((tm, d), lambda i: (i, 0)),
        out_shape=jax.ShapeDtypeStruct((t, d), F32),
        compiler_params=_params(("parallel",)),
        name="outproj",
    )(*args)


def _even_layer(x, bsz, seq, norm_g, w_in, w_out, lb, hg_norm, mu, w0, w2, a0, a2, k_k, k_a, r_k, ln_g, ln_b,
                p_i, ple_proj, ple_gate, ple_norm, final_norm):
    aw = A_HEADS * A_HEAD_DIM
    bw = B_HEADS * B_HEAD_DIM
    rw0 = 3 * aw
    lr0 = rw0 + 3 * bw
    g0 = lr0 + 2 * B_RANK
    w_perm = jnp.concatenate([w_in[:, :lr0], w_in[:, g0:], w_in[:, lr0:g0]], axis=1).astype(BF16)
    z = _inproj(x, norm_g, w_perm, ntiles=3)

    zero = jnp.zeros_like(lb)
    hpar = jnp.stack([jnp.log(lb), jnp.log1p(-lb), 1.0 - lb, hg_norm, zero, zero, zero, zero])

    zrow = jnp.zeros((bw,), F32)
    rows = [mu[0:bw], mu[bw:2 * bw], mu[2 * bw:3 * bw], w0, a0, k_k, k_a, r_k, ln_g, ln_b]
    rpar = jnp.stack(rows + [zrow] * (_RW_NPAR - len(rows)))
    muwa = mu[3 * bw:].reshape(1, 2 * B_RANK)
    npair = B_HEADS // 2
    zpad = jnp.zeros((npair, B_RANK, LANES), F32)
    w2p = jnp.concatenate([w2.reshape(B_RANK, npair, LANES).transpose(1, 0, 2), zpad], axis=1).astype(BF16)
    a2p = jnp.concatenate([zpad, a2.reshape(B_RANK, npair, LANES).transpose(1, 0, 2)], axis=1).astype(BF16)
    oa, ob = _even_mixer(z, hpar, rpar, muwa, w2p, a2p, bsz, seq, rw0, lr0 + aw + bw)

    gate0 = lr0
    w_out_b = w_out.astype(BF16)
    return _outproj([oa, ob], z, [gate0 // aw, (gate0 + aw) // bw], [w_out_b[:aw], w_out_b[aw:]],
                    x, p_i, ple_proj.astype(BF16), ple_gate.astype(BF16), ple_norm, final_norm, tm=OUTPROJ_ROWS)


def _odd_layer(x, bsz, seq, norm_g, w_in, w_out, conv_w, conv_b, wa, ba, wx, bx, lam,
               p_i, ple_proj, ple_gate, ple_norm, final_norm):
    cw = C_HEADS * C_HEAD_DIM
    dw = D_BLOCKS * D_BLOCK_DIM
    x0 = 3 * cw
    g0 = x0 + dw
    w_perm = jnp.concatenate([w_in[:, :x0], w_in[:, g0:g0 + cw], w_in[:, x0:g0], w_in[:, g0 + cw:]],
                             axis=1).astype(BF16)
    z = _inproj(x, norm_g, w_perm, ntiles=2)

    zero = jnp.zeros_like(lam)
    vecs = jnp.stack([conv_b, ba, bx, lam, zero, zero, zero, zero])
    oc, od = _odd_mixer(z, conv_w, vecs, wa.astype(BF16), wx.astype(BF16), bsz, seq, 0, cw, 2 * cw, x0 + cw)

    w_out_b = w_out.astype(BF16)
    return _outproj([oc, od], z, [x0 // cw, (x0 + cw + dw) // dw], [w_out_b[:cw], w_out_b[cw:]],
                    x, p_i, ple_proj.astype(BF16), ple_gate.astype(BF16), ple_norm, final_norm, tm=OUTPROJ_ROWS)


def kernel(x, p, ev_norm, ev_w_in, ev_w_out, hg_lb_logits, hg_norm, rw_mu, rw_w0, rw_w2, rw_a0, rw_a2, rw_k_k, rw_k_a, rw_r_k, rw_ln_g, rw_ln_b, od_norm, od_w_in, od_w_out, lru_conv_w, lru_conv_b, lru_wa, lru_ba, lru_wx, lru_bx, lru_lambda, ple_proj, ple_gate, ple_norm, final_norm):
    bsz, seq, d = x.shape
    depth = p.shape[0]
    s = jax.nn.softmax(hg_lb_logits.astype(F32), axis=0)
    lower_bounds = jnp.maximum(jnp.cumsum(s, axis=0) - s[0], 0.0)
    xf = x.reshape(bsz * seq, d)
    pf = p.reshape(depth, bsz * seq, p.shape[-1])
    for i in range(depth):
        j = i // 2
        fin = final_norm if i == depth - 1 else None
        if i % 2 == 0:
            xf = _even_layer(xf, bsz, seq, ev_norm[j], ev_w_in[j], ev_w_out[j], lower_bounds[j], hg_norm[j],
                             rw_mu[j], rw_w0[j], rw_w2[j], rw_a0[j], rw_a2[j], rw_k_k[j], rw_k_a[j], rw_r_k[j],
                             rw_ln_g[j], rw_ln_b[j], (pf, i), ple_proj[i], ple_gate[i], ple_norm[i], fin)
        else:
            xf = _odd_layer(xf, bsz, seq, od_norm[j], od_w_in[j], od_w_out[j], lru_conv_w[j], lru_conv_b[j],
                            lru_wa[j], lru_ba[j], lru_wx[j], lru_bx[j], lru_lambda[j],
                            (pf, i), ple_proj[i], ple_gate[i], ple_norm[i], fin)
    return xf.reshape(bsz, seq, d)
```

```python
import functools
import math

import jax
import jax.numpy as jnp
from jax import lax
from jax.experimental import pallas as pl
from jax.experimental.pallas import tpu as pltpu

F32 = jnp.float32
BF16 = jnp.bfloat16

EPS = 1e-6
LOG2E = math.log2(math.e)
LANES = 128
SUBLANES = 8
VMEM_LIMIT = 56 * 1024 * 1024

A_HEADS, A_HEAD_DIM = 8, 128
B_HEADS, B_HEAD_DIM = 16, 64
B_RANK = 64
B_LN_EPS = 64e-5
C_HEADS, C_HEAD_DIM, C_BLOCK, C_TOPK = 8, 64, 256, 3
D_BLOCKS, D_BLOCK_DIM, D_CONV = 8, 128, 4
LRU_C = 8.0
P_DIM = 256

OUTPROJ_ROWS = 512
CHUNK = 64
ROWS = 256
NEG_BIG = -1e30


def _nt(a, b):
    return lax.dot_general(a, b, (((1,), (1,)), ((), ())), preferred_element_type=F32)


def _tn(a, b):
    return lax.dot_general(a, b, (((0,), (0,)), ((), ())), preferred_element_type=F32)


def _nn(a, b):
    return jnp.dot(a, b, preferred_element_type=F32)


def _split2(x):
    h = x.astype(BF16)
    return h, (x - h.astype(F32)).astype(BF16)


def _sigmoid(x):
    return 0.5 * jnp.tanh(0.5 * x) + 0.5


def _round_robin(gens, starts=None):
    starts = starts or [0] * len(gens)
    results = [None] * len(gens)
    done = [False] * len(gens)
    rnd = 0
    while not all(done):
        for i in range(len(gens)):
            if done[i] or rnd < starts[i]:
                continue
            try:
                next(gens[i])
            except StopIteration as stop:
                results[i] = stop.value
                done[i] = True
        rnd += 1
    return results


def _params(sem):
    return pltpu.CompilerParams(dimension_semantics=sem, vmem_limit_bytes=VMEM_LIMIT)


def _inproj_kernel(x_ref, g_ref, w_ref, o_ref, h_ref):
    @pl.when(pl.program_id(1) == 0)
    def _():
        x = x_ref[...]
        ms = jnp.mean(x * x, axis=-1, keepdims=True)
        h_ref[...] = (x * lax.rsqrt(ms + EPS) * g_ref[...]).astype(BF16)

    o_ref[...] = _nn(h_ref[...], w_ref[...]).astype(o_ref.dtype)


MXU_COLS = 256
INPROJ_ROWS = 1024


def _inproj(x, g, w, ntiles):
    t, d = x.shape
    tm = INPROJ_ROWS
    quantum = MXU_COLS * ntiles
    n = -(-w.shape[1] // quantum) * quantum
    w = jnp.pad(w, ((0, 0), (0, n - w.shape[1])))
    tn = n // ntiles
    return pl.pallas_call(
        _inproj_kernel,
        grid=(t // tm, n // tn),
        in_specs=[
            pl.BlockSpec((tm, d), lambda i, j: (i, 0)),
            pl.BlockSpec((1, d), lambda i, j: (0, 0)),
            pl.BlockSpec((d, tn), lambda i, j: (0, j)),
        ],
        out_specs=pl.BlockSpec((tm, tn), lambda i, j: (i, j)),
        out_shape=jax.ShapeDtypeStruct((t, n), BF16),
        scratch_shapes=[pltpu.VMEM((tm, d), BF16)],
        compiler_params=_params(("parallel", "arbitrary")),
        name="inproj",
    )(x, g.reshape(1, d), w)


_LEVELS = tuple(2 ** i for i in range(int(math.log2(CHUNK))))


def _boundary_bcast(g, m):
    c, w = g.shape
    if m >= SUBLANES:
        gr = g.reshape(c // (2 * m), 2 * m, w)
        b = jnp.broadcast_to(gr[:, m - 1:m, :], gr.shape)
        return b.reshape(c, w)
    gr = g.reshape(c // SUBLANES, SUBLANES, w)
    j = lax.broadcasted_iota(jnp.int32, gr.shape, 1)
    rows = [jnp.broadcast_to(gr[:, r:r + 1, :], gr.shape) for r in range(m - 1, SUBLANES, 2 * m)]
    b = rows[-1]
    for idx in range(len(rows) - 2, -1, -1):
        b = jnp.where(j < (idx + 1) * 2 * m, rows[idx], b)
    return b.reshape(c, w)


def _hgrn2_chains(zq_ref, zf_ref, zi_ref, par_ref, o_ref, st_ref, first_block):
    c = CHUNK

    @pl.when(first_block)
    def _():
        st_ref[...] = jnp.zeros_like(st_ref)

    row = lax.broadcasted_iota(jnp.int32, (c, c), 0)
    col = lax.broadcasted_iota(jnp.int32, (c, c), 1)
    ltri = (col <= row).astype(BF16)
    ltri2 = jnp.concatenate([ltri, ltri], axis=1)
    eye = col == row
    lvl_masks = []
    for m in _LEVELS:
        sh = int(math.log2(2 * m))
        same = (row >> sh) == (col >> sh)
        lvl_masks.append(same & ((row & (2 * m - 1)) >= m) & ((col & (2 * m - 1)) < m))
    rowi = lax.broadcasted_iota(jnp.int32, (c, LANES), 0)

    nseq, nheads = st_ref.shape[0], st_ref.shape[1]
    par_all = par_ref[...]

    def head_chunk(hi, zq, zf, v, st):
        cs = slice(hi * LANES, (hi + 1) * LANES)
        log_lb = par_all[0:1, cs]
        log_omlb = par_all[1:2, cs]
        omlb = par_all[2:3, cs]
        gain = par_all[3:4, cs]

        e = jnp.exp(-jnp.abs(zf))
        log_sig = jnp.minimum(zf, 0.0) - jnp.log(1.0 + e)
        sig_neg = _sigmoid(-zf)
        bterm = log_omlb + log_sig
        mx = jnp.maximum(log_lb, bterm)
        log_f = mx + jnp.log(1.0 + jnp.exp(-jnp.abs(log_lb - bterm)))
        k = omlb * sig_neg
        q = zq * _sigmoid(zq)

        lf2 = log_f * LOG2E
        g = _nn(ltri2, jnp.concatenate(_split2(lf2), axis=0))
        yield
        g_last = g[c - 1:c, :]

        qb = q.astype(BF16)
        kb = k.astype(BF16)
        vb = v.astype(BF16)
        attn = jnp.where(eye, _nt(qb, kb), 0.0)
        for m, msk in zip(_LEVELS, lvl_masks):
            second = (rowi & (2 * m - 1)) >= m
            if m == 1:
                fac = jnp.where(second, jnp.exp2(lf2), 1.0)
            else:
                fac = jnp.exp2(-jnp.abs(g - _boundary_bcast(g, m)))
            x = (jnp.where(second, q, k) * fac).astype(BF16)
            attn = attn + jnp.where(msk, _nt(x, x), 0.0)
        yield

        o = _nn(attn.astype(BF16), vb) + _nt((q * jnp.exp2(g)).astype(BF16), st.astype(BF16))
        khat = (k * jnp.exp2(g_last - g)).astype(BF16)
        st_new = st * jnp.exp2(g_last) + _tn(vb, khat)
        yield

        ms = jnp.mean(o * o, axis=-1, keepdims=True)
        return o * lax.rsqrt(ms + EPS) * gain, st_new

    def chains(ci):
        sl = pl.ds(pl.multiple_of(ci * c, c), c)
        out = []
        for bi in range(nseq):
            for hi in range(nheads):
                cs = slice(hi * LANES, (hi + 1) * LANES)

                def store(res, bi=bi, hi=hi, cs=cs):
                    o_ref[bi, sl, cs] = res[0]
                    st_ref[bi, hi] = res[1]

                load = lambda ref: ref[bi, sl, cs].astype(F32)
                out.append((head_chunk(hi, load(zq_ref), load(zf_ref), load(zi_ref), st_ref[bi, hi]), store))
        return out

    return chains


_EXP_M05 = math.exp(-0.5)
(_P_MU_R, _P_MU_K, _P_MU_V, _P_W0, _P_A0, _P_KK, _P_KA, _P_RK, _P_LNG, _P_LNB) = range(10)
_RW_NPAR = 16


def _rwkv_chains(zr_ref, zk_ref, zv_ref, zwa_ref, par_ref, muwa_ref, w2_ref, a2_ref, o_ref,
                 st_ref, ext_ref, extwa_ref, first_block):
    c = CHUNK
    hd = B_HEAD_DIM
    nbat, npair = st_ref.shape[0], st_ref.shape[1]

    @pl.when(first_block)
    def _():
        st_ref[...] = jnp.zeros_like(st_ref)
        ext_ref[:, :, 0:SUBLANES, :] = jnp.zeros((nbat, 3, SUBLANES, npair * LANES), F32)
        extwa_ref[:, 0:SUBLANES, :] = jnp.zeros((nbat, SUBLANES, LANES), F32)

    par_all = par_ref[...]

    @pl.when(jnp.logical_not(first_block))
    def _():
        ext_ref[:, :, 0:SUBLANES, :] = ext_ref[:, :, ROWS:SUBLANES + ROWS, :]
        extwa_ref[:, 0:SUBLANES, :] = extwa_ref[:, ROWS:SUBLANES + ROWS, :]

    for bi in range(nbat):
        ext_ref[bi, 0, SUBLANES:SUBLANES + ROWS, :] = zr_ref[bi].astype(F32)
        ext_ref[bi, 1, SUBLANES:SUBLANES + ROWS, :] = zk_ref[bi].astype(F32)
        ext_ref[bi, 2, SUBLANES:SUBLANES + ROWS, :] = zv_ref[bi].astype(F32)
        extwa_ref[bi, SUBLANES:SUBLANES + ROWS, :] = zwa_ref[bi].astype(F32)

    first_row = lax.broadcasted_iota(jnp.int32, (c, LANES), 0) == 0

    def mixed(ext, ci, cs, mu):
        start = pl.multiple_of(ci * c, c) + SUBLANES
        cur = ext[pl.ds(start, c), cs]
        above = ext[pl.ds(start - SUBLANES, SUBLANES), cs][SUBLANES - 1:SUBLANES]
        prev = jnp.where(first_row, above, pltpu.roll(cur, 1, 0))
        return cur + (prev - cur) * mu

    lane = lax.broadcasted_iota(jnp.int32, (c, LANES), 1)
    lo_l = lane < hd
    lane2 = lax.broadcasted_iota(jnp.int32, (2 * c, LANES), 1)
    row2 = lax.broadcasted_iota(jnp.int32, (2 * c, LANES), 0)
    s_idx = lane2 & (c - 1)
    t_idx = row2 & (c - 1)
    tri = (s_idx < t_idx) | ((row2 >= c) & (s_idx == t_idx))
    eye_s = ((lane & (c - 1)) == lax.broadcasted_iota(jnp.int32, (c, LANES), 0)).astype(F32)
    ltri = (lax.broadcasted_iota(jnp.int32, (c, c), 1)
            <= lax.broadcasted_iota(jnp.int32, (c, c), 0)).astype(BF16)
    ltri2 = jnp.concatenate([ltri, ltri], axis=1)
    seg = ((lax.broadcasted_iota(jnp.int32, (LANES, LANES), 0) >> int(math.log2(hd)))
           == (lax.broadcasted_iota(jnp.int32, (LANES, LANES), 1) >> int(math.log2(hd))))
    seg_b = seg.astype(BF16)
    seg2 = jnp.concatenate([seg_b, seg_b], axis=0)

    def segsum(x):
        return _nn(jnp.concatenate(_split2(x), axis=1), seg2)

    def by_head(x):
        zero = jnp.zeros_like(x)
        return jnp.concatenate([jnp.where(lo_l, x, zero), jnp.where(lo_l, zero, x)], axis=0)

    def pair_chunk(pi, r, kraw, v, st, w2, a2, wab, twab):
        cs = slice(pi * LANES, (pi + 1) * LANES)
        prow = lambda i: par_all[i:i + 1, cs]

        xw = prow(_P_W0) + _nn(twab, w2)
        g = (-_EXP_M05 * LOG2E) * _sigmoid(xw)
        alpha = _sigmoid(prow(_P_A0) + _nn(wab, a2))
        yield
        kk = kraw * prow(_P_KK)
        k = kraw * (1.0 + (alpha - 1.0) * prow(_P_KA))
        sums = segsum(jnp.concatenate([kk * kk, r * k * prow(_P_RK)], axis=0))
        gc = _nn(ltri2, jnp.concatenate(_split2(g), axis=0))
        yield
        kk = kk / jnp.maximum(jnp.sqrt(sums[0:c]), 1e-12)
        rk_sum = sums[c:2 * c]
        bv = kk * alpha
        g_last = gc[c - 1:c, :]
        e_pos = jnp.exp2(gc)
        e_neg = jnp.exp2(-gc)
        e_last = jnp.exp2(g_last)
        at = -kk * jnp.exp2(gc - g)
        rt = r * e_pos
        bt = bv * e_neg
        kt = k * e_neg
        bh = bt * e_last
        kh = kt * e_last

        atb = at.astype(BF16)
        rtb = rt.astype(BF16)
        lhs = jnp.concatenate([atb, rtb], axis=0)
        ab_rb = jnp.where(tri, _nt(lhs, by_head(bt.astype(BF16))), 0.0)
        ak_rk = jnp.where(tri, _nt(lhs, by_head(kt.astype(BF16))), 0.0)
        yield

        stb = st.astype(BF16)
        vh = by_head(v.astype(BF16))
        w = _nt(atb, stb) + _nn(ak_rk[0:c].astype(BF16), vh)
        pw = ab_rb[0:c]
        t_inv = eye_s + pw
        for _ in range(int(math.log2(c)) - 1):
            pwb = pw.astype(BF16)
            pw = _nn(pwb, by_head(pwb))
            yield
            t_inv = t_inv + _nn(pw.astype(BF16), by_head(t_inv.astype(BF16)))
        yield

        u = _nn(t_inv.astype(BF16), by_head(w.astype(BF16)))
        yield
        ub = u.astype(BF16)
        a_r = jnp.concatenate([ab_rb[c:2 * c], ak_rk[c:2 * c]], axis=1).astype(BF16)
        y = _nt(rtb, stb) + _nn(a_r, jnp.concatenate([by_head(ub), vh], axis=0))

        st_new = st * e_last + _tn(jnp.concatenate([ub, v.astype(BF16)], axis=0),
                                   jnp.concatenate([bh.astype(BF16), kh.astype(BF16)], axis=0))
        st_new = jnp.where(seg, st_new, 0.0)
        yield

        inv_n = 1.0 / hd
        moments = segsum(jnp.concatenate([y, y * y], axis=0)) * inv_n
        yield
        mu = moments[0:c]
        var = moments[c:2 * c] - mu * mu
        yn = (y - mu) * lax.rsqrt(var + B_LN_EPS) * prow(_P_LNG) + prow(_P_LNB)
        yn = yn + rk_sum * v
        return yn, st_new

    def chains(ci):
        out = []
        for bi in range(nbat):
            wa = mixed(extwa_ref.at[bi], ci, slice(None), muwa_ref[...])
            wab = wa.astype(BF16)
            twab = jnp.tanh(wa).astype(BF16)
            for pi in range(npair):
                cs = slice(pi * LANES, (pi + 1) * LANES)
                mu_of = lambda i: par_all[i:i + 1, cs]

                def store(res, bi=bi, pi=pi, cs=cs):
                    o_ref[bi, pl.ds(pl.multiple_of(ci * c, c), c), cs] = res[0]
                    st_ref[bi, pi] = res[1]

                out.append((pair_chunk(pi, mixed(ext_ref.at[bi, 0], ci, cs, mu_of(_P_MU_R)),
                                       mixed(ext_ref.at[bi, 1], ci, cs, mu_of(_P_MU_K)),
                                       mixed(ext_ref.at[bi, 2], ci, cs, mu_of(_P_MU_V)),
                                       st_ref[bi, pi], w2_ref[pi], a2_ref[pi], wab, twab), store))
        return out

    return chains


EV_SEQS_PER_STEP = 2
EV_GROUPS = 1
HG_STAGES = 3
RW_STAGES = 13


def _even_mixer_kernel(zq_ref, zf_ref, zi_ref, zr_ref, zk_ref, zv_ref, zwa_ref, hpar_ref, rpar_ref, muwa_ref,
                       w2_ref, a2_ref, oa_ref, ob_ref, hst_ref, rst_ref, ext_ref, extwa_ref):
    first = pl.program_id(1) == 0
    hg = _hgrn2_chains(zq_ref, zf_ref, zi_ref, hpar_ref, oa_ref, hst_ref, first)
    rw = _rwkv_chains(zr_ref, zk_ref, zv_ref, zwa_ref, rpar_ref, muwa_ref, w2_ref, a2_ref, ob_ref,
                      rst_ref, ext_ref, extwa_ref, first)

    def chunk(ci, carry):
        hg_chains = hg(ci)
        rw_chains = rw(ci)
        done = []
        for grp in range(EV_GROUPS):
            hgs = hg_chains[grp::EV_GROUPS]
            rws = rw_chains[grp::EV_GROUPS]
            span = max(RW_STAGES - HG_STAGES, 0)
            starts = [(i * span) // max(len(hgs) - 1, 1) for i in range(len(hgs))] + [0] * len(rws)
            both = hgs + rws
            done += list(zip(both, _round_robin([g for g, _ in both], starts)))
        for (_, store), res in done:
            store(res)
        return carry

    lax.fori_loop(0, ROWS // CHUNK, chunk, 0)


def _even_mixer(z, hpar, rpar, muwa, w2p, a2p, bsz, seq, col_rkv, col_wa):
    t, n = z.shape
    nr = seq // ROWS
    nsq = EV_SEQS_PER_STEP
    aw = A_HEADS * A_HEAD_DIM
    bw = B_HEADS * B_HEAD_DIM
    npair = B_HEADS // 2
    z3 = z.reshape(bsz, seq, n)
    wide = lambda width, col: pl.BlockSpec((nsq, ROWS, width), lambda b, r: (b, r, col // width))
    const = lambda shape: pl.BlockSpec(shape, lambda b, r: (0,) * len(shape))
    oa, ob = pl.pallas_call(
        _even_mixer_kernel,
        grid=(bsz // nsq, nr),
        in_specs=[wide(aw, 0), wide(aw, aw), wide(aw, 2 * aw),
                  wide(bw, col_rkv), wide(bw, col_rkv + bw), wide(bw, col_rkv + 2 * bw),
                  wide(LANES, col_wa),
                  const((SUBLANES, aw)), const((_RW_NPAR, bw)), const((1, LANES)),
                  const((npair, LANES, LANES)), const((npair, LANES, LANES))],
        out_specs=[pl.BlockSpec((nsq, ROWS, aw), lambda b, r: (b, r, 0)),
                   pl.BlockSpec((nsq, ROWS, bw), lambda b, r: (b, r, 0))],
        out_shape=[jax.ShapeDtypeStruct((bsz, seq, aw), F32), jax.ShapeDtypeStruct((bsz, seq, bw), F32)],
        scratch_shapes=[pltpu.VMEM((nsq, A_HEADS, A_HEAD_DIM, A_HEAD_DIM), F32),
                        pltpu.VMEM((nsq, npair, LANES, LANES), F32),
                        pltpu.VMEM((nsq, 3, SUBLANES + ROWS, bw), F32),
                        pltpu.VMEM((nsq, SUBLANES + ROWS, LANES), F32)],
        compiler_params=_params(("parallel", "arbitrary")),
        name="even_mixer",
    )(z3, z3, z3, z3, z3, z3, z3, hpar, rpar, muwa, w2p, a2p)
    return oa.reshape(t, aw), ob.reshape(t, bw)


def _moba_body(q_ref, k_ref, v_ref, o_ref, kmean_ref, kb_ref, vt_ref, bias_ref, n, straight_line_extra):
    nb = kmean_ref.shape[0]
    blk = C_BLOCK
    hd = C_HEAD_DIM
    npair = q_ref.shape[1] // LANES

    @pl.when(n == 0)
    def _():
        for j in range(nb):
            kj = k_ref[j * blk:(j + 1) * blk, :]
            kmean_ref[j:j + 1, :] = jnp.mean(kj.astype(F32), axis=0, keepdims=True)
            kb_ref[j] = kj
            for pi in range(npair):
                vj = v_ref[j * blk:(j + 1) * blk, pi * LANES:(pi + 1) * LANES]
                vt_ref[pi, j] = vj.astype(F32).T.astype(BF16)

    straight_line_extra()

    lane_m = lax.broadcasted_iota(jnp.int32, (nb, LANES), 1)
    jidx = lax.broadcasted_iota(jnp.int32, (nb, blk), 0)
    valid = jidx < n
    row_d = lax.broadcasted_iota(jnp.int32, (LANES, blk), 0)
    lo_d = row_d < hd
    key_i = lax.broadcasted_iota(jnp.int32, (blk, blk), 0)
    qry_i = lax.broadcasted_iota(jnp.int32, (blk, blk), 1)
    causal = key_i <= qry_i
    qscale = (hd ** -0.5) * math.log2(math.e)

    def select_bias(km, q_t):
        gate = lax.dot_general(km, q_t, (((1,), (0,)), ((), ())), precision=lax.Precision.HIGHEST,
                               preferred_element_type=F32)
        gm = jnp.where(valid, gate, -jnp.inf)
        cnt = jnp.zeros((nb, blk), F32)
        for j2 in range(nb):
            gj = gm[j2:j2 + 1, :]
            better = (gj > gm) | ((gj == gm) & (j2 < jidx))
            cnt = cnt + jnp.where(better, 1.0, 0.0)
        sel = valid & (cnt < float(C_TOPK))
        return jnp.where(sel, 0.0, NEG_BIG)

    qt_heads = []
    for pi in range(npair):
        cs = slice(pi * LANES, (pi + 1) * LANES)
        q_t = q_ref[:, cs].astype(F32).T
        kmean = kmean_ref[:, cs]
        bias_ref[2 * pi] = select_bias(jnp.where(lane_m < hd, kmean, 0.0), q_t)
        bias_ref[2 * pi + 1] = select_bias(jnp.where(lane_m >= hd, kmean, 0.0), q_t)
        qs = q_t * qscale
        qt_heads.append((jnp.where(lo_d, qs, 0.0).astype(BF16), jnp.where(lo_d, 0.0, qs).astype(BF16)))

    nq = blk // MB_QUERY_TILE
    lo_q = lo_d[:, 0:MB_QUERY_TILE]

    def pair_block(pi, qi, blocks, prev, own_first):
        qs = slice(qi * MB_QUERY_TILE, (qi + 1) * MB_QUERY_TILE)
        kj = jnp.concatenate([kb_ref[j, :, pi * LANES:(pi + 1) * LANES] for j in blocks], axis=0)
        s = [_nn(kj, qt[:, qs]) for qt in qt_heads[pi]]
        yield

        def masked(x, h, i):
            part = x[i * blk:(i + 1) * blk]
            if own_first and i == 0:
                return jnp.where(causal[:, qs], part, -jnp.inf)
            return part + bias_ref[2 * pi + h, pl.ds(blocks[i], 1), :][:, qs]

        s = [jnp.concatenate([masked(x, h, i) for i in range(len(blocks))], axis=0) for h, x in enumerate(s)]
        smax = [jnp.max(x, axis=0, keepdims=True) for x in s]
        m_new = smax if own_first else [jnp.maximum(m, x) for m, x in zip(prev[0], smax)]
        p = [jnp.exp2(x - m) for x, m in zip(s, m_new)]
        psum = [jnp.sum(x, axis=0, keepdims=True) for x in p]
        vt = jnp.concatenate([vt_ref[pi, j] for j in blocks], axis=1)
        pv = jnp.concatenate([_nn(vt[0:hd], p[0].astype(BF16)), _nn(vt[hd:], p[1].astype(BF16))], axis=0)
        yield
        if own_first:
            return m_new, psum, pv
        corr = [jnp.exp2(m - mn) for m, mn in zip(prev[0], m_new)]
        l_new = [l * c + x for l, c, x in zip(prev[1], corr, psum)]
        return m_new, l_new, prev[2] * jnp.where(lo_q, corr[0], corr[1]) + pv

    def visit(blocks, carry, own_first):
        gens = [pair_block(i // nq, i % nq, blocks, None if own_first else carry[i], own_first)
                for i in range(npair * nq)]
        return tuple(_round_robin(gens))

    rest = jnp.maximum(n - 1, 0)
    res = visit([n, rest], None, True)
    start = 0
    for size in MB_KEY_TILES:
        count = (rest - start) // size
        res = lax.fori_loop(
            0, count, lambda g, carry, s0=start, sz=size: visit([s0 + g * sz + i for i in range(sz)], carry, False), res)
        start = start + count * size
    for pi in range(npair):
        tiles = []
        for qi in range(nq):
            _, l, acc = res[pi * nq + qi]
            tiles.append(acc / jnp.where(lo_q, l[0], l[1]))
        out_t = jnp.concatenate(tiles, axis=1)
        o_ref[:, pi * LANES:(pi + 1) * LANES] = out_t.T


MB_QUERY_TILE = 256
MB_KEY_TILES = (2, 1)


def _rglru_init(ext_ref, h_ref, first_block):
    @pl.when(first_block)
    def _():
        ext_ref[0:SUBLANES, :] = jnp.zeros((SUBLANES, ext_ref.shape[1]), F32)
        h_ref[...] = jnp.zeros_like(h_ref)


def _rglru_gates(x_ref, cw_ref, vec_ref, wa_ref, wx_ref, ext_ref, a_ref, b_ref):
    rows = x_ref.shape[0]

    x = x_ref[...].astype(F32)
    ext_ref[SUBLANES:SUBLANES + rows, :] = x
    ext = ext_ref[...]
    acc = ext * cw_ref[0:1, :]
    for i in range(1, D_CONV):
        acc = ext * cw_ref[i:i + 1, :] + pltpu.roll(acc, 1, 0)
    xc = acc[SUBLANES:SUBLANES + rows] + vec_ref[0:1, :]
    ext_ref[0:SUBLANES, :] = x[rows - SUBLANES:rows, :]

    ba = vec_ref[1:2, :]
    bx = vec_ref[2:3, :]
    lam = vec_ref[3:4, :]
    sp = jnp.maximum(-lam, 0.0) + jnp.log(1.0 + jnp.exp(-jnp.abs(lam)))
    for nblk in range(D_BLOCKS):
        cs = slice(nblk * D_BLOCK_DIM, (nblk + 1) * D_BLOCK_DIM)
        xb = xc[:, cs]
        xbb = xb.astype(BF16)
        rg = _sigmoid(_nn(xbb, wa_ref[nblk]) + ba[:, cs])
        ig = _sigmoid(_nn(xbb, wx_ref[nblk]) + bx[:, cs])
        log_a = -LRU_C * rg * sp[:, cs]
        th = jnp.tanh(log_a)
        one_minus_a2 = -2.0 * th / (1.0 - th)
        a_ref[:, cs] = jnp.exp(log_a)
        b_ref[:, cs] = jnp.sqrt(one_minus_a2) * (ig * xb)


def _rglru_scan(o_ref, a_ref, b_ref, h_ref):
    rows, width = a_ref.shape
    rowi = lax.broadcasted_iota(jnp.int32, (SUBLANES, width), 0)

    def group(gi, carry):
        sl = pl.ds(pl.multiple_of(gi * SUBLANES, SUBLANES), SUBLANES)
        a = a_ref[sl, :]
        b = b_ref[sl, :]
        for d in (1, 2, 4):
            keep = rowi >= d
            a_sh = jnp.where(keep, pltpu.roll(a, d, 0), 1.0)
            b_sh = jnp.where(keep, pltpu.roll(b, d, 0), 0.0)
            b = a * b_sh + b
            a = a * a_sh
        h = a * carry + b
        o_ref[sl, :] = h
        return jnp.broadcast_to(h[SUBLANES - 1:SUBLANES, :], (SUBLANES, width))

    h_ref[...] = lax.fori_loop(0, rows // SUBLANES, group, h_ref[...], unroll=4)


def _odd_mixer_kernel(q_ref, k_ref, v_ref, x_ref, cw_ref, vec_ref, wa_ref, wx_ref, oc_ref, od_ref,
                      kmean_ref, kb_ref, vt_ref, bias_ref, ext_ref, a_ref, b_ref, h_ref):
    n = pl.program_id(1)
    _rglru_init(ext_ref, h_ref, n == 0)
    _moba_body(q_ref, k_ref, v_ref, oc_ref, kmean_ref, kb_ref, vt_ref, bias_ref, n,
               functools.partial(_rglru_gates, x_ref, cw_ref, vec_ref, wa_ref, wx_ref, ext_ref, a_ref, b_ref))
    _rglru_scan(od_ref, a_ref, b_ref, h_ref)


def _odd_mixer(z, conv_w, vecs, wa, wx, bsz, seq, col_q, col_k, col_v, col_x):
    assert C_BLOCK == ROWS and seq // C_BLOCK >= 2
    t = z.shape[0]
    nb = seq // C_BLOCK
    npp = C_HEADS // 2
    cw = C_HEADS * C_HEAD_DIM
    dw = D_BLOCKS * D_BLOCK_DIM
    full = lambda shape: pl.BlockSpec(shape, lambda b, n: (0,) * len(shape))
    return pl.pallas_call(
        _odd_mixer_kernel,
        grid=(bsz, nb),
        in_specs=[pl.BlockSpec((C_BLOCK, cw), lambda b, n: (b * nb + n, col_q // cw)),
                  pl.BlockSpec((seq, cw), lambda b, n: (b, col_k // cw)),
                  pl.BlockSpec((seq, cw), lambda b, n: (b, col_v // cw)),
                  pl.BlockSpec((ROWS, dw), lambda b, n: (b * nb + n, col_x // dw)),
                  full((D_CONV, dw)), full((SUBLANES, dw)),
                  full((D_BLOCKS, D_BLOCK_DIM, D_BLOCK_DIM)), full((D_BLOCKS, D_BLOCK_DIM, D_BLOCK_DIM))],
        out_specs=[pl.BlockSpec((C_BLOCK, cw), lambda b, n: (b * nb + n, 0)),
                   pl.BlockSpec((ROWS, dw), lambda b, n: (b * nb + n, 0))],
        out_shape=[jax.ShapeDtypeStruct((t, cw), F32), jax.ShapeDtypeStruct((t, dw), F32)],
        scratch_shapes=[pltpu.VMEM((nb, cw), F32),
                        pltpu.VMEM((nb, C_BLOCK, cw), BF16),
                        pltpu.VMEM((npp, nb, LANES, C_BLOCK), BF16),
                        pltpu.VMEM((2 * npp, nb, C_BLOCK), F32),
                        pltpu.VMEM((SUBLANES + ROWS, dw), F32),
                        pltpu.VMEM((ROWS, dw), F32),
                        pltpu.VMEM((ROWS, dw), F32),
                        pltpu.VMEM((SUBLANES, dw), F32)],
        compiler_params=_params(("parallel", "arbitrary")),
        name="odd_mixer",
    )(z, z, z, z, conv_w, vecs, wa, wx)


def _outproj_kernel(*refs, nbr, final):
    br = refs[:nbr]
    gt = refs[nbr:2 * nbr]
    ws = refs[2 * nbr:3 * nbr]
    x_ref, p_ref, pp_ref, pg_ref, pn_ref = refs[3 * nbr:3 * nbr + 5]
    rest = refs[3 * nbr + 5:]
    if final:
        fn_ref, o_ref = rest
    else:
        (o_ref,) = rest

    acc = x_ref[...]
    for b, g, w in zip(br, gt, ws):
        gv = g[...].astype(F32)
        y = b[...] * (gv * _sigmoid(gv))
        acc = acc + _nn(y.astype(BF16), w[...])
    ms = jnp.mean(acc * acc, axis=-1, keepdims=True)
    hn = (acc * lax.rsqrt(ms + EPS) * pn_ref[...]).astype(BF16)
    gate = _sigmoid(_nn(hn, pg_ref[...]))
    out = acc + _nn(p_ref[0].astype(BF16), pp_ref[...]) * gate
    if final:
        ms2 = jnp.mean(out * out, axis=-1, keepdims=True)
        out = out * lax.rsqrt(ms2 + EPS) * fn_ref[...]
    o_ref[...] = out


def _outproj(branches, z, gate_cols, w_parts, x, p_i, ple_proj, ple_gate, ple_norm, final_norm, tm):
    t, d = x.shape
    p_all, layer = p_i
    nbr = len(branches)
    in_specs, args = [], []
    for b in branches:
        in_specs.append(pl.BlockSpec((tm, b.shape[1]), lambda i: (i, 0)))
        args.append(b)
    for b, gc in zip(branches, gate_cols):
        in_specs.append(pl.BlockSpec((tm, b.shape[1]), lambda i, gc=gc: (i, gc)))
        args.append(z)
    for w in w_parts:
        in_specs.append(pl.BlockSpec(w.shape, lambda i: (0, 0)))
        args.append(w)
    in_specs += [pl.BlockSpec((tm, d), lambda i: (i, 0)),
                 pl.BlockSpec((1, tm, P_DIM), lambda i: (layer, i, 0)),
                 pl.BlockSpec((P_DIM, d), lambda i: (0, 0)),
                 pl.BlockSpec((d, d), lambda i: (0, 0)),
                 pl.BlockSpec((1, d), lambda i: (0, 0))]
    args += [x, p_all, ple_proj, ple_gate, ple_norm.reshape(1, d)]
    final = final_norm is not None
    if final:
        in_specs.append(pl.BlockSpec((1, d), lambda i: (0, 0)))
        args.append(final_norm.reshape(1, d))
    return pl.pallas_call(
        functools.partial(_outproj_kernel, nbr=nbr, final=final),
        grid=(t // tm,),
        in_specs=in_specs,
        out_specs=pl.BlockSpec((tm, d), lambda i: (i, 0)),
        out_shape=jax.ShapeDtypeStruct((t, d), F32),
        compiler_params=_params(("parallel",)),
        name="outproj",
    )(*args)


def _even_layer(x, bsz, seq, norm_g, w_in, w_out, lb, hg_norm, mu, w0, w2, a0, a2, k_k, k_a, r_k, ln_g, ln_b,
                p_i, ple_proj, ple_gate, ple_norm, final_norm):
    aw = A_HEADS * A_HEAD_DIM
    bw = B_HEADS * B_HEAD_DIM
    rw0 = 3 * aw
    lr0 = rw0 + 3 * bw
    g0 = lr0 + 2 * B_RANK
    w_perm = jnp.concatenate([w_in[:, :lr0], w_in[:, g0:], w_in[:, lr0:g0]], axis=1).astype(BF16)
    z = _inproj(x, norm_g, w_perm, ntiles=3)

    zero = jnp.zeros_like(lb)
    hpar = jnp.stack([jnp.log(lb), jnp.log1p(-lb), 1.0 - lb, hg_norm, zero, zero, zero, zero])

    zrow = jnp.zeros((bw,), F32)
    rows = [mu[0:bw], mu[bw:2 * bw], mu[2 * bw:3 * bw], w0, a0, k_k, k_a, r_k, ln_g, ln_b]
    rpar = jnp.stack(rows + [zrow] * (_RW_NPAR - len(rows)))
    muwa = mu[3 * bw:].reshape(1, 2 * B_RANK)
    npair = B_HEADS // 2
    zpad = jnp.zeros((npair, B_RANK, LANES), F32)
    w2p = jnp.concatenate([w2.reshape(B_RANK, npair, LANES).transpose(1, 0, 2), zpad], axis=1).astype(BF16)
    a2p = jnp.concatenate([zpad, a2.reshape(B_RANK, npair, LANES).transpose(1, 0, 2)], axis=1).astype(BF16)
    oa, ob = _even_mixer(z, hpar, rpar, muwa, w2p, a2p, bsz, seq, rw0, lr0 + aw + bw)

    gate0 = lr0
    w_out_b = w_out.astype(BF16)
    return _outproj([oa, ob], z, [gate0 // aw, (gate0 + aw) // bw], [w_out_b[:aw], w_out_b[aw:]],
                    x, p_i, ple_proj.astype(BF16), ple_gate.astype(BF16), ple_norm, final_norm, tm=OUTPROJ_ROWS)


def _odd_layer(x, bsz, seq, norm_g, w_in, w_out, conv_w, conv_b, wa, ba, wx, bx, lam,
               p_i, ple_proj, ple_gate, ple_norm, final_norm):
    cw = C_HEADS * C_HEAD_DIM
    dw = D_BLOCKS * D_BLOCK_DIM
    x0 = 3 * cw
    g0 = x0 + dw
    w_perm = jnp.concatenate([w_in[:, :x0], w_in[:, g0:g0 + cw], w_in[:, x0:g0], w_in[:, g0 + cw:]],
                             axis=1).astype(BF16)
    z = _inproj(x, norm_g, w_perm, ntiles=2)

    zero = jnp.zeros_like(lam)
    vecs = jnp.stack([conv_b, ba, bx, lam, zero, zero, zero, zero])
    oc, od = _odd_mixer(z, conv_w, vecs, wa.astype(BF16), wx.astype(BF16), bsz, seq, 0, cw, 2 * cw, x0 + cw)

    w_out_b = w_out.astype(BF16)
    return _outproj([oc, od], z, [x0 // cw, (x0 + cw + dw) // dw], [w_out_b[:cw], w_out_b[cw:]],
                    x, p_i, ple_proj.astype(BF16), ple_gate.astype(BF16), ple_norm, final_norm, tm=OUTPROJ_ROWS)


def kernel(x, p, ev_norm, ev_w_in, ev_w_out, hg_lb_logits, hg_norm, rw_mu, rw_w0, rw_w2, rw_a0, rw_a2, rw_k_k, rw_k_a, rw_r_k, rw_ln_g, rw_ln_b, od_norm, od_w_in, od_w_out, lru_conv_w, lru_conv_b, lru_wa, lru_ba, lru_wx, lru_bx, lru_lambda, ple_proj, ple_gate, ple_norm, final_norm):
    bsz, seq, d = x.shape
    depth = p.shape[0]
    s = jax.nn.softmax(hg_lb_logits.astype(F32), axis=0)
    lower_bounds = jnp.maximum(jnp.cumsum(s, axis=0) - s[0], 0.0)
    xf = x.reshape(bsz * seq, d)
    pf = p.reshape(depth, bsz * seq, p.shape[-1])
    for i in range(depth):
        j = i // 2
        fin = final_norm if i == depth - 1 else None
        if i % 2 == 0:
            xf = _even_layer(xf, bsz, seq, ev_norm[j], ev_w_in[j], ev_w_out[j], lower_bounds[j], hg_norm[j],
                             rw_mu[j], rw_w0[j], rw_w2[j], rw_a0[j], rw_a2[j], rw_k_k[j], rw_k_a[j], rw_r_k[j],
                             rw_ln_g[j], rw_ln_b[j], (pf, i), ple_proj[i], ple_gate[i], ple_norm[i], fin)
        else:
            xf = _odd_layer(xf, bsz, seq, od_norm[j], od_w_in[j], od_w_out[j], lru_conv_w[j], lru_conv_b[j],
                            lru_wa[j], lru_ba[j], lru_wx[j], lru_bx[j], lru_lambda[j],
                            (pf, i), ple_proj[i], ple_gate[i], ple_norm[i], fin)
    return xf.reshape(bsz, seq, d)
```

```python
import functools
import math

import jax
import jax.numpy as jnp
from jax import lax
from jax.experimental import pallas as pl
from jax.experimental.pallas import tpu as pltpu

F32 = jnp.float32
BF16 = jnp.bfloat16

EPS = 1e-6
LOG2E = math.log2(math.e)
LANES = 128
SUBLANES = 8
VMEM_LIMIT = 56 * 1024 * 1024

A_HEADS, A_HEAD_DIM = 8, 128
B_HEADS, B_HEAD_DIM = 16, 64
B_RANK = 64
B_LN_EPS = 64e-5
C_HEADS, C_HEAD_DIM, C_BLOCK, C_TOPK = 8, 64, 256, 3
D_BLOCKS, D_BLOCK_DIM, D_CONV = 8, 128, 4
LRU_C = 8.0
P_DIM = 256

OUTPROJ_ROWS = 512
CHUNK = 64
ROWS = 256
NEG_BIG = -1e30


def _nt(a, b):
    return lax.dot_general(a, b, (((1,), (1,)), ((), ())), preferred_element_type=F32)


def _tn(a, b):
    return lax.dot_general(a, b, (((0,), (0,)), ((), ())), preferred_element_type=F32)


def _nn(a, b):
    return jnp.dot(a, b, preferred_element_type=F32)


def _split2(x):
    h = x.astype(BF16)
    return h, (x - h.astype(F32)).astype(BF16)


def _sigmoid(x):
    return 0.5 * jnp.tanh(0.5 * x) + 0.5


def _round_robin(gens, starts=None):
    starts = starts or [0] * len(gens)
    results = [None] * len(gens)
    done = [False] * len(gens)
    rnd = 0
    while not all(done):
        for i in range(len(gens)):
            if done[i] or rnd < starts[i]:
                continue
            try:
                next(gens[i])
            except StopIteration as stop:
                results[i] = stop.value
                done[i] = True
        rnd += 1
    return results


def _params(sem):
    return pltpu.CompilerParams(dimension_semantics=sem, vmem_limit_bytes=VMEM_LIMIT)


def _inproj_kernel(x_ref, g_ref, w_ref, o_ref, h_ref):
    @pl.when(pl.program_id(1) == 0)
    def _():
        x = x_ref[...]
        ms = jnp.mean(x * x, axis=-1, keepdims=True)
        h_ref[...] = (x * lax.rsqrt(ms + EPS) * g_ref[...]).astype(BF16)

    o_ref[...] = _nn(h_ref[...], w_ref[...]).astype(o_ref.dtype)


MXU_COLS = 256
INPROJ_ROWS = 1024


def _inproj(x, g, w, ntiles):
    t, d = x.shape
    tm = INPROJ_ROWS
    quantum = MXU_COLS * ntiles
    n = -(-w.shape[1] // quantum) * quantum
    w = jnp.pad(w, ((0, 0), (0, n - w.shape[1])))
    tn = n // ntiles
    return pl.pallas_call(
        _inproj_kernel,
        grid=(t // tm, n // tn),
        in_specs=[
            pl.BlockSpec((tm, d), lambda i, j: (i, 0)),
            pl.BlockSpec((1, d), lambda i, j: (0, 0)),
            pl.BlockSpec((d, tn), lambda i, j: (0, j)),
        ],
        out_specs=pl.BlockSpec((tm, tn), lambda i, j: (i, j)),
        out_shape=jax.ShapeDtypeStruct((t, n), BF16),
        scratch_shapes=[pltpu.VMEM((tm, d), BF16)],
        compiler_params=_params(("parallel", "arbitrary")),
        name="inproj",
    )(x, g.reshape(1, d), w)


_LEVELS = tuple(2 ** i for i in range(int(math.log2(CHUNK))))

def _boundary_bcast(g, m):
    c, w = g.shape
    if m >= SUBLANES:
        gr = g.reshape(c // (2 * m), 2 * m, w)
        b = jnp.broadcast_to(gr[:, m - 1:m, :], gr.shape)
        return b.reshape(c, w)
    gr = g.reshape(c // SUBLANES, SUBLANES, w)
    j = lax.broadcasted_iota(jnp.int32, gr.shape, 1)
    rows = [jnp.broadcast_to(gr[:, r:r + 1, :], gr.shape) for r in range(m - 1, SUBLANES, 2 * m)]
    b = rows[-1]
    for idx in range(len(rows) - 2, -1, -1):
        b = jnp.where(j < (idx + 1) * 2 * m, rows[idx], b)
    return b.reshape(c, w)


def _hgrn2_chains(zq_ref, zf_ref, zi_ref, par_ref, o_ref, st_ref, first_block):
    c = CHUNK

    @pl.when(first_block)
    def _():
        st_ref[...] = jnp.zeros_like(st_ref)

    row = lax.broadcasted_iota(jnp.int32, (c, c), 0)
    col = lax.broadcasted_iota(jnp.int32, (c, c), 1)
    ltri = (col <= row).astype(BF16)
    ltri2 = jnp.concatenate([ltri, ltri], axis=1)
    eye = col == row
    lvl_masks = []
    for m in _LEVELS:
        sh = int(math.log2(2 * m))
        same = (row >> sh) == (col >> sh)
        lvl_masks.append(same & ((row & (2 * m - 1)) >= m) & ((col & (2 * m - 1)) < m))
    rowi = lax.broadcasted_iota(jnp.int32, (c, LANES), 0)

    nseq, nheads = st_ref.shape[0], st_ref.shape[1]
    par_all = par_ref[...]

    def head_chunk(hi, zq, zf, v, st):
        cs = slice(hi * LANES, (hi + 1) * LANES)
        log_lb = par_all[0:1, cs]
        log_omlb = par_all[1:2, cs]
        omlb = par_all[2:3, cs]
        gain = par_all[3:4, cs]

        e = jnp.exp(-jnp.abs(zf))
        log_sig = jnp.minimum(zf, 0.0) - jnp.log(1.0 + e)
        sig_neg = _sigmoid(-zf)
        bterm = log_omlb + log_sig
        mx = jnp.maximum(log_lb, bterm)
        log_f = mx + jnp.log(1.0 + jnp.exp(-jnp.abs(log_lb - bterm)))
        k = omlb * sig_neg
        q = zq * _sigmoid(zq)

        lf2 = log_f * LOG2E
        g = _nn(ltri2, jnp.concatenate(_split2(lf2), axis=0))
        yield
        g_last = g[c - 1:c, :]

        qb = q.astype(BF16)
        kb = k.astype(BF16)
        vb = v.astype(BF16)
        attn = jnp.where(eye, _nt(qb, kb), 0.0)
        for m, msk in zip(_LEVELS, lvl_masks):
            second = (rowi & (2 * m - 1)) >= m
            if m == 1:
                fac = jnp.where(second, jnp.exp2(lf2), 1.0)
            else:
                fac = jnp.exp2(-jnp.abs(g - _boundary_bcast(g, m)))
            x = (jnp.where(second, q, k) * fac).astype(BF16)
            attn = attn + jnp.where(msk, _nt(x, x), 0.0)
        yield

        o = _nn(attn.astype(BF16), vb) + _nt((q * jnp.exp2(g)).astype(BF16), st.astype(BF16))
        khat = (k * jnp.exp2(g_last - g)).astype(BF16)
        st_new = st * jnp.exp2(g_last) + _tn(vb, khat)
        yield

        ms = jnp.mean(o * o, axis=-1, keepdims=True)
        return o * lax.rsqrt(ms + EPS) * gain, st_new

    def chains(ci):
        sl = pl.ds(pl.multiple_of(ci * c, c), c)
        out = []
        for bi in range(nseq):
            for hi in range(nheads):
                cs = slice(hi * LANES, (hi + 1) * LANES)

                def store(res, bi=bi, hi=hi, cs=cs):
                    o_ref[bi, sl, cs] = res[0]
                    st_ref[bi, hi] = res[1]

                load = lambda ref: ref[bi, sl, cs].astype(F32)
                out.append((head_chunk(hi, load(zq_ref), load(zf_ref), load(zi_ref), st_ref[bi, hi]), store))
        return out

    return chains


_EXP_M05 = math.exp(-0.5)
(_P_MU_R, _P_MU_K, _P_MU_V, _P_W0, _P_A0, _P_KK, _P_KA, _P_RK, _P_LNG, _P_LNB) = range(10)
_RW_NPAR = 16


def _rwkv_chains(zr_ref, zk_ref, zv_ref, zwa_ref, par_ref, muwa_ref, w2_ref, a2_ref, o_ref,
                 st_ref, ext_ref, extwa_ref, first_block):
    c = CHUNK
    hd = B_HEAD_DIM
    nbat, npair = st_ref.shape[0], st_ref.shape[1]

    @pl.when(first_block)
    def _():
        st_ref[...] = jnp.zeros_like(st_ref)
        ext_ref[:, :, 0:SUBLANES, :] = jnp.zeros((nbat, 3, SUBLANES, npair * LANES), F32)
        extwa_ref[:, 0:SUBLANES, :] = jnp.zeros((nbat, SUBLANES, LANES), F32)

    par_all = par_ref[...]

    @pl.when(jnp.logical_not(first_block))
    def _():
        ext_ref[:, :, 0:SUBLANES, :] = ext_ref[:, :, ROWS:SUBLANES + ROWS, :]
        extwa_ref[:, 0:SUBLANES, :] = extwa_ref[:, ROWS:SUBLANES + ROWS, :]

    for bi in range(nbat):
        ext_ref[bi, 0, SUBLANES:SUBLANES + ROWS, :] = zr_ref[bi].astype(F32)
        ext_ref[bi, 1, SUBLANES:SUBLANES + ROWS, :] = zk_ref[bi].astype(F32)
        ext_ref[bi, 2, SUBLANES:SUBLANES + ROWS, :] = zv_ref[bi].astype(F32)
        extwa_ref[bi, SUBLANES:SUBLANES + ROWS, :] = zwa_ref[bi].astype(F32)

    first_row = lax.broadcasted_iota(jnp.int32, (c, LANES), 0) == 0

    def mixed(ext, ci, cs, mu):
        start = pl.multiple_of(ci * c, c) + SUBLANES
        cur = ext[pl.ds(start, c), cs]
        above = ext[pl.ds(start - SUBLANES, SUBLANES), cs][SUBLANES - 1:SUBLANES]
        prev = jnp.where(first_row, above, pltpu.roll(cur, 1, 0))
        return cur + (prev - cur) * mu

    lane = lax.broadcasted_iota(jnp.int32, (c, LANES), 1)
    lo_l = lane < hd
    lane2 = lax.broadcasted_iota(jnp.int32, (2 * c, LANES), 1)
    row2 = lax.broadcasted_iota(jnp.int32, (2 * c, LANES), 0)
    s_idx = lane2 & (c - 1)
    t_idx = row2 & (c - 1)
    tri = (s_idx < t_idx) | ((row2 >= c) & (s_idx == t_idx))
    eye_s = ((lane & (c - 1)) == lax.broadcasted_iota(jnp.int32, (c, LANES), 0)).astype(F32)
    ltri = (lax.broadcasted_iota(jnp.int32, (c, c), 1)
            <= lax.broadcasted_iota(jnp.int32, (c, c), 0)).astype(BF16)
    ltri2 = jnp.concatenate([ltri, ltri], axis=1)
    seg = ((lax.broadcasted_iota(jnp.int32, (LANES, LANES), 0) >> int(math.log2(hd)))
           == (lax.broadcasted_iota(jnp.int32, (LANES, LANES), 1) >> int(math.log2(hd))))
    seg_b = seg.astype(BF16)
    seg2 = jnp.concatenate([seg_b, seg_b], axis=0)

    def segsum(x):
        return _nn(jnp.concatenate(_split2(x), axis=1), seg2)

    def by_head(x):
        zero = jnp.zeros_like(x)
        return jnp.concatenate([jnp.where(lo_l, x, zero), jnp.where(lo_l, zero, x)], axis=0)

    def pair_chunk(pi, r, kraw, v, st, w2, a2, wab, twab):
        cs = slice(pi * LANES, (pi + 1) * LANES)
        prow = lambda i: par_all[i:i + 1, cs]

        xw = prow(_P_W0) + _nn(twab, w2)
        g = (-_EXP_M05 * LOG2E) * _sigmoid(xw)
        alpha = _sigmoid(prow(_P_A0) + _nn(wab, a2))
        yield
        kk = kraw * prow(_P_KK)
        k = kraw * (1.0 + (alpha - 1.0) * prow(_P_KA))
        sums = segsum(jnp.concatenate([kk * kk, r * k * prow(_P_RK)], axis=0))
        gc = _nn(ltri2, jnp.concatenate(_split2(g), axis=0))
        yield
        kk = kk / jnp.maximum(jnp.sqrt(sums[0:c]), 1e-12)
        rk_sum = sums[c:2 * c]
        bv = kk * alpha
        g_last = gc[c - 1:c, :]
        e_pos = jnp.exp2(gc)
        e_neg = jnp.exp2(-gc)
        e_last = jnp.exp2(g_last)
        at = -kk * jnp.exp2(gc - g)
        rt = r * e_pos
        bt = bv * e_neg
        kt = k * e_neg
        bh = bt * e_last
        kh = kt * e_last

        atb = at.astype(BF16)
        rtb = rt.astype(BF16)
        lhs = jnp.concatenate([atb, rtb], axis=0)
        ab_rb = jnp.where(tri, _nt(lhs, by_head(bt.astype(BF16))), 0.0)
        ak_rk = jnp.where(tri, _nt(lhs, by_head(kt.astype(BF16))), 0.0)
        yield

        stb = st.astype(BF16)
        vh = by_head(v.astype(BF16))
        w = _nt(atb, stb) + _nn(ak_rk[0:c].astype(BF16), vh)
        pw = ab_rb[0:c]
        t_inv = eye_s + pw
        for _ in range(int(math.log2(c)) - 1):
            pwb = pw.astype(BF16)
            pw = _nn(pwb, by_head(pwb))
            yield
            t_inv = t_inv + _nn(pw.astype(BF16), by_head(t_inv.astype(BF16)))
        yield

        u = _nn(t_inv.astype(BF16), by_head(w.astype(BF16)))
        yield
        ub = u.astype(BF16)
        a_r = jnp.concatenate([ab_rb[c:2 * c], ak_rk[c:2 * c]], axis=1).astype(BF16)
        y = _nt(rtb, stb) + _nn(a_r, jnp.concatenate([by_head(ub), vh], axis=0))

        st_new = st * e_last + _tn(jnp.concatenate([ub, v.astype(BF16)], axis=0),
                                   jnp.concatenate([bh.astype(BF16), kh.astype(BF16)], axis=0))
        st_new = jnp.where(seg, st_new, 0.0)
        yield

        inv_n = 1.0 / hd
        moments = segsum(jnp.concatenate([y, y * y], axis=0)) * inv_n
        yield
        mu = moments[0:c]
        var = moments[c:2 * c] - mu * mu
        yn = (y - mu) * lax.rsqrt(var + B_LN_EPS) * prow(_P_LNG) + prow(_P_LNB)
        yn = yn + rk_sum * v
        return yn, st_new

    def chains(ci):
        out = []
        for bi in range(nbat):
            wa = mixed(extwa_ref.at[bi], ci, slice(None), muwa_ref[...])
            wab = wa.astype(BF16)
            twab = jnp.tanh(wa).astype(BF16)
            for pi in range(npair):
                cs = slice(pi * LANES, (pi + 1) * LANES)
                mu_of = lambda i: par_all[i:i + 1, cs]

                def store(res, bi=bi, pi=pi, cs=cs):
                    o_ref[bi, pl.ds(pl.multiple_of(ci * c, c), c), cs] = res[0]
                    st_ref[bi, pi] = res[1]

                out.append((pair_chunk(pi, mixed(ext_ref.at[bi, 0], ci, cs, mu_of(_P_MU_R)),
                                       mixed(ext_ref.at[bi, 1], ci, cs, mu_of(_P_MU_K)),
                                       mixed(ext_ref.at[bi, 2], ci, cs, mu_of(_P_MU_V)),
                                       st_ref[bi, pi], w2_ref[pi], a2_ref[pi], wab, twab), store))
        return out

    return chains


EV_SEQS_PER_STEP = 2
EV_GROUPS = 1
HG_STAGES = 3
RW_STAGES = 13


def _even_mixer_kernel(zq_ref, zf_ref, zi_ref, zr_ref, zk_ref, zv_ref, zwa_ref, hpar_ref, rpar_ref, muwa_ref,
                       w2_ref, a2_ref, oa_ref, ob_ref, hst_ref, rst_ref, ext_ref, extwa_ref):
    first = pl.program_id(1) == 0
    hg = _hgrn2_chains(zq_ref, zf_ref, zi_ref, hpar_ref, oa_ref, hst_ref, first)
    rw = _rwkv_chains(zr_ref, zk_ref, zv_ref, zwa_ref, rpar_ref, muwa_ref, w2_ref, a2_ref, ob_ref,
                      rst_ref, ext_ref, extwa_ref, first)

    def chunk(ci, carry):
        hg_chains = hg(ci)
        rw_chains = rw(ci)
        done = []
        for grp in range(EV_GROUPS):
            hgs = hg_chains[grp::EV_GROUPS]
            rws = rw_chains[grp::EV_GROUPS]
            span = max(RW_STAGES - HG_STAGES, 0)
            rw_starts = [i % 2 for i in range(len(rws))]
            hg_starts = [(i * span) // max(len(hgs) - 1, 1) for i in range(len(hgs))]
            both = [c for pair in zip(rws, hgs) for c in pair]
            starts = [s for pair in zip(rw_starts, hg_starts) for s in pair]
            done += list(zip(both, _round_robin([g for g, _ in both], starts)))
        for (_, store), res in done:
            store(res)
        return carry

    lax.fori_loop(0, ROWS // CHUNK, chunk, 0)


def _even_mixer(z, hpar, rpar, muwa, w2p, a2p, bsz, seq, col_rkv, col_wa):
    t, n = z.shape
    nr = seq // ROWS
    nsq = EV_SEQS_PER_STEP
    aw = A_HEADS * A_HEAD_DIM
    bw = B_HEADS * B_HEAD_DIM
    npair = B_HEADS // 2
    z3 = z.reshape(bsz, seq, n)
    wide = lambda width, col: pl.BlockSpec((nsq, ROWS, width), lambda b, r: (b, r, col // width))
    const = lambda shape: pl.BlockSpec(shape, lambda b, r: (0,) * len(shape))
    oa, ob = pl.pallas_call(
        _even_mixer_kernel,
        grid=(bsz // nsq, nr),
        in_specs=[wide(aw, 0), wide(aw, aw), wide(aw, 2 * aw),
                  wide(bw, col_rkv), wide(bw, col_rkv + bw), wide(bw, col_rkv + 2 * bw),
                  wide(LANES, col_wa),
                  const((SUBLANES, aw)), const((_RW_NPAR, bw)), const((1, LANES)),
                  const((npair, LANES, LANES)), const((npair, LANES, LANES))],
        out_specs=[pl.BlockSpec((nsq, ROWS, aw), lambda b, r: (b, r, 0)),
                   pl.BlockSpec((nsq, ROWS, bw), lambda b, r: (b, r, 0))],
        out_shape=[jax.ShapeDtypeStruct((bsz, seq, aw), F32), jax.ShapeDtypeStruct((bsz, seq, bw), F32)],
        scratch_shapes=[pltpu.VMEM((nsq, A_HEADS, A_HEAD_DIM, A_HEAD_DIM), F32),
                        pltpu.VMEM((nsq, npair, LANES, LANES), F32),
                        pltpu.VMEM((nsq, 3, SUBLANES + ROWS, bw), F32),
                        pltpu.VMEM((nsq, SUBLANES + ROWS, LANES), F32)],
        compiler_params=_params(("parallel", "arbitrary")),
        name="even_mixer",
    )(z3, z3, z3, z3, z3, z3, z3, hpar, rpar, muwa, w2p, a2p)
    return oa.reshape(t, aw), ob.reshape(t, bw)


def _moba_body(q_ref, k_ref, v_ref, o_ref, kmean_ref, kb_ref, vt_ref, bias_ref, n, straight_line_extra):
    nb = kmean_ref.shape[0]
    blk = C_BLOCK
    hd = C_HEAD_DIM
    npair = q_ref.shape[1] // LANES

    @pl.when(n == 0)
    def _():
        for j in range(nb):
            kj = k_ref[j * blk:(j + 1) * blk, :]
            kmean_ref[j:j + 1, :] = jnp.mean(kj.astype(F32), axis=0, keepdims=True)
            kb_ref[j] = kj
            for pi in range(npair):
                vj = v_ref[j * blk:(j + 1) * blk, pi * LANES:(pi + 1) * LANES]
                vt_ref[pi, j] = vj.astype(F32).T.astype(BF16)

    straight_line_extra()

    lane_m = lax.broadcasted_iota(jnp.int32, (nb, LANES), 1)
    jidx = lax.broadcasted_iota(jnp.int32, (nb, blk), 0)
    valid = jidx < n
    row_d = lax.broadcasted_iota(jnp.int32, (LANES, blk), 0)
    lo_d = row_d < hd
    key_i = lax.broadcasted_iota(jnp.int32, (blk, blk), 0)
    qry_i = lax.broadcasted_iota(jnp.int32, (blk, blk), 1)
    causal = key_i <= qry_i
    qscale = (hd ** -0.5) * math.log2(math.e)

    def select_bias(km, q_t):
        gate = lax.dot_general(km, q_t, (((1,), (0,)), ((), ())), precision=lax.Precision.HIGHEST,
                               preferred_element_type=F32)
        gm = jnp.where(valid, gate, -jnp.inf)
        cnt = jnp.zeros((nb, blk), F32)
        for j2 in range(nb):
            gj = gm[j2:j2 + 1, :]
            better = (gj > gm) | ((gj == gm) & (j2 < jidx))
            cnt = cnt + jnp.where(better, 1.0, 0.0)
        sel = valid & (cnt < float(C_TOPK))
        return jnp.where(sel, 0.0, NEG_BIG)

    qt_heads = []
    for pi in range(npair):
        cs = slice(pi * LANES, (pi + 1) * LANES)
        q_t = q_ref[:, cs].astype(F32).T
        kmean = kmean_ref[:, cs]
        bias_ref[2 * pi] = select_bias(jnp.where(lane_m < hd, kmean, 0.0), q_t)
        bias_ref[2 * pi + 1] = select_bias(jnp.where(lane_m >= hd, kmean, 0.0), q_t)
        qs = q_t * qscale
        qt_heads.append((jnp.where(lo_d, qs, 0.0).astype(BF16), jnp.where(lo_d, 0.0, qs).astype(BF16)))

    nq = blk // MB_QUERY_TILE
    lo_q = lo_d[:, 0:MB_QUERY_TILE]

    def pair_block(pi, qi, blocks, prev, own_first):
        qs = slice(qi * MB_QUERY_TILE, (qi + 1) * MB_QUERY_TILE)
        kj = jnp.concatenate([kb_ref[j, :, pi * LANES:(pi + 1) * LANES] for j in blocks], axis=0)
        s = [_nn(kj, qt[:, qs]) for qt in qt_heads[pi]]
        yield

        def masked(x, h, i):
            part = x[i * blk:(i + 1) * blk]
            if own_first and i == 0:
                return jnp.where(causal[:, qs], part, -jnp.inf)
            return part + bias_ref[2 * pi + h, pl.ds(blocks[i], 1), :][:, qs]

        s = [jnp.concatenate([masked(x, h, i) for i in range(len(blocks))], axis=0) for h, x in enumerate(s)]
        smax = [jnp.max(x, axis=0, keepdims=True) for x in s]
        m_new = smax if own_first else [jnp.maximum(m, x) for m, x in zip(prev[0], smax)]
        p = [jnp.exp2(x - m) for x, m in zip(s, m_new)]
        psum = [jnp.sum(x, axis=0, keepdims=True) for x in p]
        vt = jnp.concatenate([vt_ref[pi, j] for j in blocks], axis=1)
        pv = jnp.concatenate([_nn(vt[0:hd], p[0].astype(BF16)), _nn(vt[hd:], p[1].astype(BF16))], axis=0)
        yield
        if own_first:
            return m_new, psum, pv
        corr = [jnp.exp2(m - mn) for m, mn in zip(prev[0], m_new)]
        l_new = [l * c + x for l, c, x in zip(prev[1], corr, psum)]
        return m_new, l_new, prev[2] * jnp.where(lo_q, corr[0], corr[1]) + pv

    def visit(blocks, carry, own_first):
        gens = [pair_block(i // nq, i % nq, blocks, None if own_first else carry[i], own_first)
                for i in range(npair * nq)]
        return tuple(_round_robin(gens))

    rest = jnp.maximum(n - 1, 0)
    res = visit([n, rest], None, True)
    start = 0
    for size in MB_KEY_TILES:
        count = (rest - start) // size
        res = lax.fori_loop(
            0, count, lambda g, carry, s0=start, sz=size: visit([s0 + g * sz + i for i in range(sz)], carry, False), res)
        start = start + count * size
    for pi in range(npair):
        tiles = []
        for qi in range(nq):
            _, l, acc = res[pi * nq + qi]
            tiles.append(acc / jnp.where(lo_q, l[0], l[1]))
        out_t = jnp.concatenate(tiles, axis=1)
        o_ref[:, pi * LANES:(pi + 1) * LANES] = out_t.T


MB_QUERY_TILE = 256
MB_KEY_TILES = (2, 1)


def _rglru_init(ext_ref, h_ref, first_block):
    @pl.when(first_block)
    def _():
        ext_ref[0:SUBLANES, :] = jnp.zeros((SUBLANES, ext_ref.shape[1]), F32)
        h_ref[...] = jnp.zeros_like(h_ref)


def _rglru_gates(x_ref, cw_ref, vec_ref, wa_ref, wx_ref, ext_ref, a_ref, b_ref):
    rows = x_ref.shape[0]

    x = x_ref[...].astype(F32)
    ext_ref[SUBLANES:SUBLANES + rows, :] = x
    ext = ext_ref[...]
    acc = ext * cw_ref[0:1, :]
    for i in range(1, D_CONV):
        acc = ext * cw_ref[i:i + 1, :] + pltpu.roll(acc, 1, 0)
    xc = acc[SUBLANES:SUBLANES + rows] + vec_ref[0:1, :]
    ext_ref[0:SUBLANES, :] = x[rows - SUBLANES:rows, :]

    ba = vec_ref[1:2, :]
    bx = vec_ref[2:3, :]
    lam = vec_ref[3:4, :]
    sp = jnp.maximum(-lam, 0.0) + jnp.log(1.0 + jnp.exp(-jnp.abs(lam)))
    for nblk in range(D_BLOCKS):
        cs = slice(nblk * D_BLOCK_DIM, (nblk + 1) * D_BLOCK_DIM)
        xb = xc[:, cs]
        xbb = xb.astype(BF16)
        rg = _sigmoid(_nn(xbb, wa_ref[nblk]) + ba[:, cs])
        ig = _sigmoid(_nn(xbb, wx_ref[nblk]) + bx[:, cs])
        log_a = -LRU_C * rg * sp[:, cs]
        th = jnp.tanh(log_a)
        one_minus_a2 = -2.0 * th / (1.0 - th)
        a_ref[:, cs] = jnp.exp(log_a)
        b_ref[:, cs] = jnp.sqrt(one_minus_a2) * (ig * xb)


def _rglru_scan(o_ref, a_ref, b_ref, h_ref):
    rows, width = a_ref.shape
    rowi = lax.broadcasted_iota(jnp.int32, (SUBLANES, width), 0)

    def group(gi, carry):
        sl = pl.ds(pl.multiple_of(gi * SUBLANES, SUBLANES), SUBLANES)
        a = a_ref[sl, :]
        b = b_ref[sl, :]
        for d in (1, 2, 4):
            keep = rowi >= d
            a_sh = jnp.where(keep, pltpu.roll(a, d, 0), 1.0)
            b_sh = jnp.where(keep, pltpu.roll(b, d, 0), 0.0)
            b = a * b_sh + b
            a = a * a_sh
        h = a * carry + b
        o_ref[sl, :] = h
        return jnp.broadcast_to(h[SUBLANES - 1:SUBLANES, :], (SUBLANES, width))

    h_ref[...] = lax.fori_loop(0, rows // SUBLANES, group, h_ref[...], unroll=4)


def _odd_mixer_kernel(q_ref, k_ref, v_ref, x_ref, cw_ref, vec_ref, wa_ref, wx_ref, oc_ref, od_ref,
                      kmean_ref, kb_ref, vt_ref, bias_ref, ext_ref, a_ref, b_ref, h_ref):
    n = pl.program_id(1)
    _rglru_init(ext_ref, h_ref, n == 0)
    _moba_body(q_ref, k_ref, v_ref, oc_ref, kmean_ref, kb_ref, vt_ref, bias_ref, n,
               functools.partial(_rglru_gates, x_ref, cw_ref, vec_ref, wa_ref, wx_ref, ext_ref, a_ref, b_ref))
    _rglru_scan(od_ref, a_ref, b_ref, h_ref)


def _odd_mixer(z, conv_w, vecs, wa, wx, bsz, seq, col_q, col_k, col_v, col_x):
    assert C_BLOCK == ROWS and seq // C_BLOCK >= 2
    t = z.shape[0]
    nb = seq // C_BLOCK
    npp = C_HEADS // 2
    cw = C_HEADS * C_HEAD_DIM
    dw = D_BLOCKS * D_BLOCK_DIM
    full = lambda shape: pl.BlockSpec(shape, lambda b, n: (0,) * len(shape))
    return pl.pallas_call(
        _odd_mixer_kernel,
        grid=(bsz, nb),
        in_specs=[pl.BlockSpec((C_BLOCK, cw), lambda b, n: (b * nb + n, col_q // cw)),
                  pl.BlockSpec((seq, cw), lambda b, n: (b, col_k // cw)),
                  pl.BlockSpec((seq, cw), lambda b, n: (b, col_v // cw)),
                  pl.BlockSpec((ROWS, dw), lambda b, n: (b * nb + n, col_x // dw)),
                  full((D_CONV, dw)), full((SUBLANES, dw)),
                  full((D_BLOCKS, D_BLOCK_DIM, D_BLOCK_DIM)), full((D_BLOCKS, D_BLOCK_DIM, D_BLOCK_DIM))],
        out_specs=[pl.BlockSpec((C_BLOCK, cw), lambda b, n: (b * nb + n, 0)),
                   pl.BlockSpec((ROWS, dw), lambda b, n: (b * nb + n, 0))],
        out_shape=[jax.ShapeDtypeStruct((t, cw), F32), jax.ShapeDtypeStruct((t, dw), F32)],
        scratch_shapes=[pltpu.VMEM((nb, cw), F32),
                        pltpu.VMEM((nb, C_BLOCK, cw), BF16),
                        pltpu.VMEM((npp, nb, LANES, C_BLOCK), BF16),
                        pltpu.VMEM((2 * npp, nb, C_BLOCK), F32),
                        pltpu.VMEM((SUBLANES + ROWS, dw), F32),
                        pltpu.VMEM((ROWS, dw), F32),
                        pltpu.VMEM((ROWS, dw), F32),
                        pltpu.VMEM((SUBLANES, dw), F32)],
        compiler_params=_params(("parallel", "arbitrary")),
        name="odd_mixer",
    )(z, z, z, z, conv_w, vecs, wa, wx)


def _outproj_kernel(*refs, nbr, final):
    br = refs[:nbr]
    gt = refs[nbr:2 * nbr]
    ws = refs[2 * nbr:3 * nbr]
    x_ref, p_ref, pp_ref, pg_ref, pn_ref = refs[3 * nbr:3 * nbr + 5]
    rest = refs[3 * nbr + 5:]
    if final:
        fn_ref, o_ref = rest
    else:
        (o_ref,) = rest

    acc = x_ref[...]
    for b, g, w in zip(br, gt, ws):
        gv = g[...].astype(F32)
        y = b[...] * (gv * _sigmoid(gv))
        acc = acc + _nn(y.astype(BF16), w[...])
    ms = jnp.mean(acc * acc, axis=-1, keepdims=True)
    hn = (acc * lax.rsqrt(ms + EPS) * pn_ref[...]).astype(BF16)
    gate = _sigmoid(_nn(hn, pg_ref[...]))
    out = acc + _nn(p_ref[0].astype(BF16), pp_ref[...]) * gate
    if final:
        ms2 = jnp.mean(out * out, axis=-1, keepdims=True)
        out = out * lax.rsqrt(ms2 + EPS) * fn_ref[...]
    o_ref[...] = out


def _outproj(branches, z, gate_cols, w_parts, x, p_i, ple_proj, ple_gate, ple_norm, final_norm, tm):
    t, d = x.shape
    p_all, layer = p_i
    nbr = len(branches)
    in_specs, args = [], []
    for b in branches:
        in_specs.append(pl.BlockSpec((tm, b.shape[1]), lambda i: (i, 0)))
        args.append(b)
    for b, gc in zip(branches, gate_cols):
        in_specs.append(pl.BlockSpec((tm, b.shape[1]), lambda i, gc=gc: (i, gc)))
        args.append(z)
    for w in w_parts:
        in_specs.append(pl.BlockSpec(w.shape, lambda i: (0, 0)))
        args.append(w)
    in_specs += [pl.BlockSpec((tm, d), lambda i: (i, 0)),
                 pl.BlockSpec((1, tm, P_DIM), lambda i: (layer, i, 0)),
                 pl.BlockSpec((P_DIM, d), lambda i: (0, 0)),
                 pl.BlockSpec((d, d), lambda i: (0, 0)),
                 pl.BlockSpec((1, d), lambda i: (0, 0))]
    args += [x, p_all, ple_proj, ple_gate, ple_norm.reshape(1, d)]
    final = final_norm is not None
    if final:
        in_specs.append(pl.BlockSpec((1, d), lambda i: (0, 0)))
        args.append(final_norm.reshape(1, d))
    return pl.pallas_call(
        functools.partial(_outproj_kernel, nbr=nbr, final=final),
        grid=(t // tm,),
        in_specs=in_specs,
        out_specs=pl.BlockSpec((tm, d), lambda i: (i, 0)),
        out_shape=jax.ShapeDtypeStruct((t, d), F32),
        compiler_params=_params(("parallel",)),
        name="outproj",
    )(*args)


def _even_layer(x, bsz, seq, norm_g, w_in, w_out, lb, hg_norm, mu, w0, w2, a0, a2, k_k, k_a, r_k, ln_g, ln_b,
                p_i, ple_proj, ple_gate, ple_norm, final_norm):
    aw = A_HEADS * A_HEAD_DIM
    bw = B_HEADS * B_HEAD_DIM
    rw0 = 3 * aw
    lr0 = rw0 + 3 * bw
    g0 = lr0 + 2 * B_RANK
    w_perm = jnp.concatenate([w_in[:, :lr0], w_in[:, g0:], w_in[:, lr0:g0]], axis=1).astype(BF16)
    z = _inproj(x, norm_g, w_perm, ntiles=3)

    zero = jnp.zeros_like(lb)
    hpar = jnp.stack([jnp.log(lb), jnp.log1p(-lb), 1.0 - lb, hg_norm, zero, zero, zero, zero])

    zrow = jnp.zeros((bw,), F32)
    rows = [mu[0:bw], mu[bw:2 * bw], mu[2 * bw:3 * bw], w0, a0, k_k, k_a, r_k, ln_g, ln_b]
    rpar = jnp.stack(rows + [zrow] * (_RW_NPAR - len(rows)))
    muwa = mu[3 * bw:].reshape(1, 2 * B_RANK)
    npair = B_HEADS // 2
    zpad = jnp.zeros((npair, B_RANK, LANES), F32)
    w2p = jnp.concatenate([w2.reshape(B_RANK, npair, LANES).transpose(1, 0, 2), zpad], axis=1).astype(BF16)
    a2p = jnp.concatenate([zpad, a2.reshape(B_RANK, npair, LANES).transpose(1, 0, 2)], axis=1).astype(BF16)
    oa, ob = _even_mixer(z, hpar, rpar, muwa, w2p, a2p, bsz, seq, rw0, lr0 + aw + bw)

    gate0 = lr0
    w_out_b = w_out.astype(BF16)
    return _outproj([oa, ob], z, [gate0 // aw, (gate0 + aw) // bw], [w_out_b[:aw], w_out_b[aw:]],
                    x, p_i, ple_proj.astype(BF16), ple_gate.astype(BF16), ple_norm, final_norm, tm=OUTPROJ_ROWS)


def _odd_layer(x, bsz, seq, norm_g, w_in, w_out, conv_w, conv_b, wa, ba, wx, bx, lam,
               p_i, ple_proj, ple_gate, ple_norm, final_norm):
    cw = C_HEADS * C_HEAD_DIM
    dw = D_BLOCKS * D_BLOCK_DIM
    x0 = 3 * cw
    g0 = x0 + dw
    w_perm = jnp.concatenate([w_in[:, :x0], w_in[:, g0:g0 + cw], w_in[:, x0:g0], w_in[:, g0 + cw:]],
                             axis=1).astype(BF16)
    z = _inproj(x, norm_g, w_perm, ntiles=2)

    zero = jnp.zeros_like(lam)
    vecs = jnp.stack([conv_b, ba, bx, lam, zero, zero, zero, zero])
    oc, od = _odd_mixer(z, conv_w, vecs, wa.astype(BF16), wx.astype(BF16), bsz, seq, 0, cw, 2 * cw, x0 + cw)

    w_out_b = w_out.astype(BF16)
    return _outproj([oc, od], z, [x0 // cw, (x0 + cw + dw) // dw], [w_out_b[:cw], w_out_b[cw:]],
                    x, p_i, ple_proj.astype(BF16), ple_gate.astype(BF16), ple_norm, final_norm, tm=OUTPROJ_ROWS)


def kernel(x, p, ev_norm, ev_w_in, ev_w_out, hg_lb_logits, hg_norm, rw_mu, rw_w0, rw_w2, rw_a0, rw_a2, rw_k_k, rw_k_a, rw_r_k, rw_ln_g, rw_ln_b, od_norm, od_w_in, od_w_out, lru_conv_w, lru_conv_b, lru_wa, lru_ba, lru_wx, lru_bx, lru_lambda, ple_proj, ple_gate, ple_norm, final_norm):
    bsz, seq, d = x.shape
    depth = p.shape[0]
    s = jax.nn.softmax(hg_lb_logits.astype(F32), axis=0)
    lower_bounds = jnp.maximum(jnp.cumsum(s, axis=0) - s[0], 0.0)
    xf = x.reshape(bsz * seq, d)
    pf = p.reshape(depth, bsz * seq, p.shape[-1])
    for i in range(depth):
        j = i // 2
        fin = final_norm if i == depth - 1 else None
        if i % 2 == 0:
            xf = _even_layer(xf, bsz, seq, ev_norm[j], ev_w_in[j], ev_w_out[j], lower_bounds[j], hg_norm[j],
                             rw_mu[j], rw_w0[j], rw_w2[j], rw_a0[j], rw_a2[j], rw_k_k[j], rw_k_a[j], rw_r_k[j],
                             rw_ln_g[j], rw_ln_b[j], (pf, i), ple_proj[i], ple_gate[i], ple_norm[i], fin)
        else:
            xf = _odd_layer(xf, bsz, seq, od_norm[j], od_w_in[j], od_w_out[j], lru_conv_w[j], lru_conv_b[j],
                            lru_wa[j], lru_ba[j], lru_wx[j], lru_bx[j], lru_lambda[j],
                            (pf, i), ple_proj[i], ple_gate[i], ple_norm[i], fin)
    return xf.reshape(bsz, seq, d)
```

```python
import functools
import math

import jax
import jax.numpy as jnp
from jax import lax
from jax.experimental import pallas as pl
from jax.experimental.pallas import tpu as pltpu

F32 = jnp.float32
BF16 = jnp.bfloat16

EPS = 1e-6
LOG2E = math.log2(math.e)
LANES = 128
SUBLANES = 8
VMEM_LIMIT = 56 * 1024 * 1024

A_HEADS, A_HEAD_DIM = 8, 128
B_HEADS, B_HEAD_DIM = 16, 64
B_RANK = 64
B_LN_EPS = 64e-5
C_HEADS, C_HEAD_DIM, C_BLOCK, C_TOPK = 8, 64, 256, 3
D_BLOCKS, D_BLOCK_DIM, D_CONV = 8, 128, 4
LRU_C = 8.0
P_DIM = 256

OUTPROJ_ROWS = 512
CHUNK = 64
ROWS = 256
NEG_BIG = -1e30


def _nt(a, b):
    return lax.dot_general(a, b, (((1,), (1,)), ((), ())), preferred_element_type=F32)


def _tn(a, b):
    return lax.dot_general(a, b, (((0,), (0,)), ((), ())), preferred_element_type=F32)


def _nn(a, b):
    return jnp.dot(a, b, preferred_element_type=F32)


def _split2(x):
    h = x.astype(BF16)
    return h, (x - h.astype(F32)).astype(BF16)


def _sigmoid(x):
    return 0.5 * jnp.tanh(0.5 * x) + 0.5


def _round_robin(gens, starts=None):
    starts = starts or [0] * len(gens)
    results = [None] * len(gens)
    done = [False] * len(gens)
    rnd = 0
    while not all(done):
        for i in range(len(gens)):
            if done[i] or rnd < starts[i]:
                continue
            try:
                next(gens[i])
            except StopIteration as stop:
                results[i] = stop.value
                done[i] = True
        rnd += 1
    return results


def _params(sem):
    return pltpu.CompilerParams(dimension_semantics=sem, vmem_limit_bytes=VMEM_LIMIT)


def _inproj_kernel(x_ref, g_ref, w_ref, o_ref, h_ref):
    @pl.when(pl.program_id(1) == 0)
    def _():
        x = x_ref[...]
        ms = jnp.mean(x * x, axis=-1, keepdims=True)
        h_ref[...] = (x * lax.rsqrt(ms + EPS) * g_ref[...]).astype(BF16)

    o_ref[...] = _nn(h_ref[...], w_ref[...]).astype(o_ref.dtype)


MXU_COLS = 256
INPROJ_ROWS = 1024


def _inproj(x, g, w, ntiles):
    t, d = x.shape
    tm = INPROJ_ROWS
    quantum = MXU_COLS * ntiles
    n = -(-w.shape[1] // quantum) * quantum
    w = jnp.pad(w, ((0, 0), (0, n - w.shape[1])))
    tn = n // ntiles
    return pl.pallas_call(
        _inproj_kernel,
        grid=(t // tm, n // tn),
        in_specs=[
            pl.BlockSpec((tm, d), lambda i, j: (i, 0)),
            pl.BlockSpec((1, d), lambda i, j: (0, 0)),
            pl.BlockSpec((d, tn), lambda i, j: (0, j)),
        ],
        out_specs=pl.BlockSpec((tm, tn), lambda i, j: (i, j)),
        out_shape=jax.ShapeDtypeStruct((t, n), BF16),
        scratch_shapes=[pltpu.VMEM((tm, d), BF16)],
        compiler_params=_params(("parallel", "arbitrary")),
        name="inproj",
    )(x, g.reshape(1, d), w)


_LEVELS = tuple(2 ** i for i in range(int(math.log2(CHUNK))))

def _boundary_bcast(g, m):
    c, w = g.shape
    if m >= SUBLANES:
        gr = g.reshape(c // (2 * m), 2 * m, w)
        b = jnp.broadcast_to(gr[:, m - 1:m, :], gr.shape)
        return b.reshape(c, w)
    gr = g.reshape(c // SUBLANES, SUBLANES, w)
    j = lax.broadcasted_iota(jnp.int32, gr.shape, 1)
    rows = [jnp.broadcast_to(gr[:, r:r + 1, :], gr.shape) for r in range(m - 1, SUBLANES, 2 * m)]
    b = rows[-1]
    for idx in range(len(rows) - 2, -1, -1):
        b = jnp.where(j < (idx + 1) * 2 * m, rows[idx], b)
    return b.reshape(c, w)


def _hgrn2_chains(zq_ref, zf_ref, zi_ref, par_ref, o_ref, st_ref, first_block):
    c = CHUNK

    @pl.when(first_block)
    def _():
        st_ref[...] = jnp.zeros_like(st_ref)

    row = lax.broadcasted_iota(jnp.int32, (c, c), 0)
    col = lax.broadcasted_iota(jnp.int32, (c, c), 1)
    ltri = (col <= row).astype(BF16)
    ltri2 = jnp.concatenate([ltri, ltri], axis=1)
    eye = col == row
    lvl_masks = []
    for m in _LEVELS:
        sh = int(math.log2(2 * m))
        same = (row >> sh) == (col >> sh)
        lvl_masks.append(same & ((row & (2 * m - 1)) >= m) & ((col & (2 * m - 1)) < m))
    rowi = lax.broadcasted_iota(jnp.int32, (c, LANES), 0)

    nseq, nheads = st_ref.shape[0], st_ref.shape[1]
    par_all = par_ref[...]

    def head_chunk(hi, zq, zf, v, st):
        cs = slice(hi * LANES, (hi + 1) * LANES)
        log_lb = par_all[0:1, cs]
        log_omlb = par_all[1:2, cs]
        omlb = par_all[2:3, cs]
        gain = par_all[3:4, cs]

        e = jnp.exp(-jnp.abs(zf))
        log_sig = jnp.minimum(zf, 0.0) - jnp.log(1.0 + e)
        sig_neg = _sigmoid(-zf)
        bterm = log_omlb + log_sig
        mx = jnp.maximum(log_lb, bterm)
        log_f = mx + jnp.log(1.0 + jnp.exp(-jnp.abs(log_lb - bterm)))
        k = omlb * sig_neg
        q = zq * _sigmoid(zq)

        lf2 = log_f * LOG2E
        g = _nn(ltri2, jnp.concatenate(_split2(lf2), axis=0))
        yield
        g_last = g[c - 1:c, :]

        qb = q.astype(BF16)
        kb = k.astype(BF16)
        vb = v.astype(BF16)
        attn = jnp.where(eye, _nt(qb, kb), 0.0)
        for m, msk in zip(_LEVELS, lvl_masks):
            second = (rowi & (2 * m - 1)) >= m
            if m == 1:
                fac = jnp.where(second, jnp.exp2(lf2), 1.0)
            else:
                fac = jnp.exp2(-jnp.abs(g - _boundary_bcast(g, m)))
            x = (jnp.where(second, q, k) * fac).astype(BF16)
            attn = attn + jnp.where(msk, _nt(x, x), 0.0)
        yield

        o = _nn(attn.astype(BF16), vb) + _nt((q * jnp.exp2(g)).astype(BF16), st.astype(BF16))
        khat = (k * jnp.exp2(g_last - g)).astype(BF16)
        st_new = st * jnp.exp2(g_last) + _tn(vb, khat)
        yield

        ms = jnp.mean(o * o, axis=-1, keepdims=True)
        return o * lax.rsqrt(ms + EPS) * gain, st_new

    def chains(ci):
        sl = pl.ds(pl.multiple_of(ci * c, c), c)
        out = []
        for bi in range(nseq):
            for hi in range(nheads):
                cs = slice(hi * LANES, (hi + 1) * LANES)

                def store(res, bi=bi, hi=hi, cs=cs):
                    o_ref[bi, sl, cs] = res[0]
                    st_ref[bi, hi] = res[1]

                load = lambda ref: ref[bi, sl, cs].astype(F32)
                out.append((head_chunk(hi, load(zq_ref), load(zf_ref), load(zi_ref), st_ref[bi, hi]), store))
        return out

    return chains


_EXP_M05 = math.exp(-0.5)
(_P_MU_R, _P_MU_K, _P_MU_V, _P_W0, _P_A0, _P_KK, _P_KA, _P_RK, _P_LNG, _P_LNB) = range(10)
_RW_NPAR = 16


def _rwkv_chains(zr_ref, zk_ref, zv_ref, zwa_ref, par_ref, muwa_ref, w2_ref, a2_ref, o_ref,
                 st_ref, ext_ref, extwa_ref, first_block):
    c = CHUNK
    hd = B_HEAD_DIM
    nbat, npair = st_ref.shape[0], st_ref.shape[1]

    @pl.when(first_block)
    def _():
        st_ref[...] = jnp.zeros_like(st_ref)
        ext_ref[:, :, 0:SUBLANES, :] = jnp.zeros((nbat, 3, SUBLANES, npair * LANES), F32)
        extwa_ref[:, 0:SUBLANES, :] = jnp.zeros((nbat, SUBLANES, LANES), F32)

    par_all = par_ref[...]

    @pl.when(jnp.logical_not(first_block))
    def _():
        ext_ref[:, :, 0:SUBLANES, :] = ext_ref[:, :, ROWS:SUBLANES + ROWS, :]
        extwa_ref[:, 0:SUBLANES, :] = extwa_ref[:, ROWS:SUBLANES + ROWS, :]

    for bi in range(nbat):
        ext_ref[bi, 0, SUBLANES:SUBLANES + ROWS, :] = zr_ref[bi].astype(F32)
        ext_ref[bi, 1, SUBLANES:SUBLANES + ROWS, :] = zk_ref[bi].astype(F32)
        ext_ref[bi, 2, SUBLANES:SUBLANES + ROWS, :] = zv_ref[bi].astype(F32)
        extwa_ref[bi, SUBLANES:SUBLANES + ROWS, :] = zwa_ref[bi].astype(F32)

    first_row = lax.broadcasted_iota(jnp.int32, (c, LANES), 0) == 0

    def mixed(ext, ci, cs, mu):
        start = pl.multiple_of(ci * c, c) + SUBLANES
        cur = ext[pl.ds(start, c), cs]
        above = ext[pl.ds(start - SUBLANES, SUBLANES), cs][SUBLANES - 1:SUBLANES]
        prev = jnp.where(first_row, above, pltpu.roll(cur, 1, 0))
        return cur + (prev - cur) * mu

    lane = lax.broadcasted_iota(jnp.int32, (c, LANES), 1)
    lo_l = lane < hd
    lane2 = lax.broadcasted_iota(jnp.int32, (2 * c, LANES), 1)
    row2 = lax.broadcasted_iota(jnp.int32, (2 * c, LANES), 0)
    s_idx = lane2 & (c - 1)
    t_idx = row2 & (c - 1)
    tri = (s_idx < t_idx) | ((row2 >= c) & (s_idx == t_idx))
    eye_s = ((lane & (c - 1)) == lax.broadcasted_iota(jnp.int32, (c, LANES), 0)).astype(F32)
    ltri = (lax.broadcasted_iota(jnp.int32, (c, c), 1)
            <= lax.broadcasted_iota(jnp.int32, (c, c), 0)).astype(BF16)
    ltri2 = jnp.concatenate([ltri, ltri], axis=1)
    seg = ((lax.broadcasted_iota(jnp.int32, (LANES, LANES), 0) >> int(math.log2(hd)))
           == (lax.broadcasted_iota(jnp.int32, (LANES, LANES), 1) >> int(math.log2(hd))))
    seg_b = seg.astype(BF16)
    seg2 = jnp.concatenate([seg_b, seg_b], axis=0)

    def segsum(x):
        return _nn(jnp.concatenate(_split2(x), axis=1), seg2)

    def by_head(x):
        zero = jnp.zeros_like(x)
        return jnp.concatenate([jnp.where(lo_l, x, zero), jnp.where(lo_l, zero, x)], axis=0)

    def pair_chunk(pi, r, kraw, v, st, w2, a2, wab, twab):
        cs = slice(pi * LANES, (pi + 1) * LANES)
        prow = lambda i: par_all[i:i + 1, cs]

        half_c = -0.5 * _EXP_M05 * LOG2E
        g = half_c * jnp.tanh(prow(_P_W0) + _nn(twab, w2)) + half_c
        alpha = 0.5 * jnp.tanh(prow(_P_A0) + _nn(wab, a2)) + 0.5
        yield
        kk = kraw * prow(_P_KK)
        k = kraw * (1.0 + (alpha - 1.0) * prow(_P_KA))
        sums = segsum(jnp.concatenate([kk * kk, r * k * prow(_P_RK)], axis=0))
        gc = _nn(ltri2, jnp.concatenate(_split2(g), axis=0))
        yield
        kk = kk / jnp.maximum(jnp.sqrt(sums[0:c]), 1e-12)
        rk_sum = sums[c:2 * c]
        bv = kk * alpha
        g_last = gc[c - 1:c, :]
        e_pos = jnp.exp2(gc)
        e_neg = jnp.exp2(-gc)
        e_last = jnp.exp2(g_last)
        at = -kk * jnp.exp2(gc - g)
        rt = r * e_pos
        bt = bv * e_neg
        kt = k * e_neg
        bh = bt * e_last
        kh = kt * e_last

        atb = at.astype(BF16)
        rtb = rt.astype(BF16)
        lhs = jnp.concatenate([atb, rtb], axis=0)
        ab_rb = jnp.where(tri, _nt(lhs, by_head(bt.astype(BF16))), 0.0)
        ak_rk = jnp.where(tri, _nt(lhs, by_head(kt.astype(BF16))), 0.0)
        yield

        stb = st.astype(BF16)
        vh = by_head(v.astype(BF16))
        w = _nt(atb, stb) + _nn(ak_rk[0:c].astype(BF16), vh)
        pw = ab_rb[0:c]
        t_inv = eye_s + pw
        for _ in range(int(math.log2(c)) - 1):
            pwb = pw.astype(BF16)
            pw = _nn(pwb, by_head(pwb))
            yield
            t_inv = t_inv + _nn(pw.astype(BF16), by_head(t_inv.astype(BF16)))
        yield

        u = _nn(t_inv.astype(BF16), by_head(w.astype(BF16)))
        yield
        ub = u.astype(BF16)
        a_r = jnp.concatenate([ab_rb[c:2 * c], ak_rk[c:2 * c]], axis=1).astype(BF16)
        y = _nt(rtb, stb) + _nn(a_r, jnp.concatenate([by_head(ub), vh], axis=0))

        st_new = st * e_last + _tn(jnp.concatenate([ub, v.astype(BF16)], axis=0),
                                   jnp.concatenate([bh.astype(BF16), kh.astype(BF16)], axis=0))
        st_new = jnp.where(seg, st_new, 0.0)
        yield

        inv_n = 1.0 / hd
        moments = segsum(jnp.concatenate([y, y * y], axis=0)) * inv_n
        yield
        mu = moments[0:c]
        var = moments[c:2 * c] - mu * mu
        yn = (y - mu) * lax.rsqrt(var + B_LN_EPS) * prow(_P_LNG) + prow(_P_LNB)
        yn = yn + rk_sum * v
        return yn, st_new

    def chains(ci):
        out = []
        for bi in range(nbat):
            wa = mixed(extwa_ref.at[bi], ci, slice(None), muwa_ref[...])
            wab = wa.astype(BF16)
            twab = jnp.tanh(wa).astype(BF16)
            for pi in range(npair):
                cs = slice(pi * LANES, (pi + 1) * LANES)
                mu_of = lambda i: par_all[i:i + 1, cs]

                def store(res, bi=bi, pi=pi, cs=cs):
                    o_ref[bi, pl.ds(pl.multiple_of(ci * c, c), c), cs] = res[0]
                    st_ref[bi, pi] = res[1]

                out.append((pair_chunk(pi, mixed(ext_ref.at[bi, 0], ci, cs, mu_of(_P_MU_R)),
                                       mixed(ext_ref.at[bi, 1], ci, cs, mu_of(_P_MU_K)),
                                       mixed(ext_ref.at[bi, 2], ci, cs, mu_of(_P_MU_V)),
                                       st_ref[bi, pi], w2_ref[pi], a2_ref[pi], wab, twab), store))
        return out

    return chains


EV_SEQS_PER_STEP = 2
EV_GROUPS = 1
HG_STAGES = 3
RW_STAGES = 13


def _even_mixer_kernel(zq_ref, zf_ref, zi_ref, zr_ref, zk_ref, zv_ref, zwa_ref, hpar_ref, rpar_ref, muwa_ref,
                       w2_ref, a2_ref, oa_ref, ob_ref, hst_ref, rst_ref, ext_ref, extwa_ref):
    first = pl.program_id(1) == 0
    hg = _hgrn2_chains(zq_ref, zf_ref, zi_ref, hpar_ref, oa_ref, hst_ref, first)
    rw = _rwkv_chains(zr_ref, zk_ref, zv_ref, zwa_ref, rpar_ref, muwa_ref, w2_ref, a2_ref, ob_ref,
                      rst_ref, ext_ref, extwa_ref, first)

    def chunk(ci, carry):
        hg_chains = hg(ci)
        rw_chains = rw(ci)
        done = []
        for grp in range(EV_GROUPS):
            hgs = hg_chains[grp::EV_GROUPS]
            rws = rw_chains[grp::EV_GROUPS]
            span = max(RW_STAGES - HG_STAGES, 0)
            rw_starts = [i % 2 for i in range(len(rws))]
            hg_starts = [(i * span) // max(len(hgs) - 1, 1) for i in range(len(hgs))]
            both = [c for pair in zip(rws, hgs) for c in pair]
            starts = [s for pair in zip(rw_starts, hg_starts) for s in pair]
            done += list(zip(both, _round_robin([g for g, _ in both], starts)))
        for (_, store), res in done:
            store(res)
        return carry

    lax.fori_loop(0, ROWS // CHUNK, chunk, 0)


def _even_mixer(z, hpar, rpar, muwa, w2p, a2p, bsz, seq, col_rkv, col_wa):
    t, n = z.shape
    nr = seq // ROWS
    nsq = EV_SEQS_PER_STEP
    aw = A_HEADS * A_HEAD_DIM
    bw = B_HEADS * B_HEAD_DIM
    npair = B_HEADS // 2
    z3 = z.reshape(bsz, seq, n)
    wide = lambda width, col: pl.BlockSpec((nsq, ROWS, width), lambda b, r: (b, r, col // width))
    const = lambda shape: pl.BlockSpec(shape, lambda b, r: (0,) * len(shape))
    oa, ob = pl.pallas_call(
        _even_mixer_kernel,
        grid=(bsz // nsq, nr),
        in_specs=[wide(aw, 0), wide(aw, aw), wide(aw, 2 * aw),
                  wide(bw, col_rkv), wide(bw, col_rkv + bw), wide(bw, col_rkv + 2 * bw),
                  wide(LANES, col_wa),
                  const((SUBLANES, aw)), const((_RW_NPAR, bw)), const((1, LANES)),
                  const((npair, LANES, LANES)), const((npair, LANES, LANES))],
        out_specs=[pl.BlockSpec((nsq, ROWS, aw), lambda b, r: (b, r, 0)),
                   pl.BlockSpec((nsq, ROWS, bw), lambda b, r: (b, r, 0))],
        out_shape=[jax.ShapeDtypeStruct((bsz, seq, aw), F32), jax.ShapeDtypeStruct((bsz, seq, bw), F32)],
        scratch_shapes=[pltpu.VMEM((nsq, A_HEADS, A_HEAD_DIM, A_HEAD_DIM), F32),
                        pltpu.VMEM((nsq, npair, LANES, LANES), F32),
                        pltpu.VMEM((nsq, 3, SUBLANES + ROWS, bw), F32),
                        pltpu.VMEM((nsq, SUBLANES + ROWS, LANES), F32)],
        compiler_params=_params(("parallel", "arbitrary")),
        name="even_mixer",
    )(z3, z3, z3, z3, z3, z3, z3, hpar, rpar, muwa, w2p, a2p)
    return oa.reshape(t, aw), ob.reshape(t, bw)


def _moba_body(q_ref, k_ref, v_ref, o_ref, kmean_ref, kb_ref, vt_ref, bias_ref, n, straight_line_extra):
    nb = kmean_ref.shape[0]
    blk = C_BLOCK
    hd = C_HEAD_DIM
    npair = q_ref.shape[1] // LANES

    @pl.when(n == 0)
    def _():
        for j in range(nb):
            kj = k_ref[j * blk:(j + 1) * blk, :]
            kmean_ref[j:j + 1, :] = jnp.mean(kj.astype(F32), axis=0, keepdims=True)
            kb_ref[j] = kj
            for pi in range(npair):
                vj = v_ref[j * blk:(j + 1) * blk, pi * LANES:(pi + 1) * LANES]
                vt_ref[pi, j] = vj.astype(F32).T.astype(BF16)

    straight_line_extra()

    lane_m = lax.broadcasted_iota(jnp.int32, (nb, LANES), 1)
    jidx = lax.broadcasted_iota(jnp.int32, (nb, blk), 0)
    valid = jidx < n
    row_d = lax.broadcasted_iota(jnp.int32, (LANES, blk), 0)
    lo_d = row_d < hd
    key_i = lax.broadcasted_iota(jnp.int32, (blk, blk), 0)
    qry_i = lax.broadcasted_iota(jnp.int32, (blk, blk), 1)
    causal = key_i <= qry_i
    qscale = (hd ** -0.5) * math.log2(math.e)

    def select_bias(km, q_t):
        gate = lax.dot_general(km, q_t, (((1,), (0,)), ((), ())), precision=lax.Precision.HIGHEST,
                               preferred_element_type=F32)
        gm = jnp.where(valid, gate, -jnp.inf)
        cnt = jnp.zeros((nb, blk), F32)
        for j2 in range(nb):
            gj = gm[j2:j2 + 1, :]
            better = (gj > gm) | ((gj == gm) & (j2 < jidx))
            cnt = cnt + jnp.where(better, 1.0, 0.0)
        sel = valid & (cnt < float(C_TOPK))
        return jnp.where(sel, 0.0, NEG_BIG)

    qt_heads = []
    for pi in range(npair):
        cs = slice(pi * LANES, (pi + 1) * LANES)
        q_t = q_ref[:, cs].astype(F32).T
        kmean = kmean_ref[:, cs]
        bias_ref[2 * pi] = select_bias(jnp.where(lane_m < hd, kmean, 0.0), q_t)
        bias_ref[2 * pi + 1] = select_bias(jnp.where(lane_m >= hd, kmean, 0.0), q_t)
        qs = q_t * qscale
        qt_heads.append((jnp.where(lo_d, qs, 0.0).astype(BF16), jnp.where(lo_d, 0.0, qs).astype(BF16)))

    nq = blk // MB_QUERY_TILE
    lo_q = lo_d[:, 0:MB_QUERY_TILE]

    def pair_block(pi, qi, blocks, prev, own_first):
        qs = slice(qi * MB_QUERY_TILE, (qi + 1) * MB_QUERY_TILE)
        kj = jnp.concatenate([kb_ref[j, :, pi * LANES:(pi + 1) * LANES] for j in blocks], axis=0)
        s = [_nn(kj, qt[:, qs]) for qt in qt_heads[pi]]
        yield

        def masked(x, h, i):
            part = x[i * blk:(i + 1) * blk]
            if own_first and i == 0:
                return jnp.where(causal[:, qs], part, -jnp.inf)
            return part + bias_ref[2 * pi + h, pl.ds(blocks[i], 1), :][:, qs]

        s = [jnp.concatenate([masked(x, h, i) for i in range(len(blocks))], axis=0) for h, x in enumerate(s)]
        smax = [jnp.max(x, axis=0, keepdims=True) for x in s]
        m_new = smax if own_first else [jnp.maximum(m, x) for m, x in zip(prev[0], smax)]
        p = [jnp.exp2(x - m) for x, m in zip(s, m_new)]
        psum = [jnp.sum(x, axis=0, keepdims=True) for x in p]
        vt = jnp.concatenate([vt_ref[pi, j] for j in blocks], axis=1)
        pv = jnp.concatenate([_nn(vt[0:hd], p[0].astype(BF16)), _nn(vt[hd:], p[1].astype(BF16))], axis=0)
        yield
        if own_first:
            return m_new, psum, pv
        corr = [jnp.exp2(m - mn) for m, mn in zip(prev[0], m_new)]
        l_new = [l * c + x for l, c, x in zip(prev[1], corr, psum)]
        return m_new, l_new, prev[2] * jnp.where(lo_q, corr[0], corr[1]) + pv

    def visit(blocks, carry, own_first):
        gens = [pair_block(i // nq, i % nq, blocks, None if own_first else carry[i], own_first)
                for i in range(npair * nq)]
        return tuple(_round_robin(gens))

    rest = jnp.maximum(n - 1, 0)
    res = visit([n, rest], None, True)
    start = 0
    for size in MB_KEY_TILES:
        count = (rest - start) // size
        res = lax.fori_loop(
            0, count, lambda g, carry, s0=start, sz=size: visit([s0 + g * sz + i for i in range(sz)], carry, False), res)
        start = start + count * size
    for pi in range(npair):
        tiles = []
        for qi in range(nq):
            _, l, acc = res[pi * nq + qi]
            tiles.append(acc / jnp.where(lo_q, l[0], l[1]))
        out_t = jnp.concatenate(tiles, axis=1)
        o_ref[:, pi * LANES:(pi + 1) * LANES] = out_t.T


MB_QUERY_TILE = 256
MB_KEY_TILES = (2, 1)


def _rglru_init(ext_ref, h_ref, first_block):
    @pl.when(first_block)
    def _():
        ext_ref[0:SUBLANES, :] = jnp.zeros((SUBLANES, ext_ref.shape[1]), F32)
        h_ref[...] = jnp.zeros_like(h_ref)


def _rglru_gates(x_ref, cw_ref, vec_ref, wa_ref, wx_ref, ext_ref, a_ref, b_ref):
    rows = x_ref.shape[0]

    x = x_ref[...].astype(F32)
    ext_ref[SUBLANES:SUBLANES + rows, :] = x
    ext = ext_ref[...]
    acc = ext * cw_ref[0:1, :]
    for i in range(1, D_CONV):
        acc = ext * cw_ref[i:i + 1, :] + pltpu.roll(acc, 1, 0)
    xc = acc[SUBLANES:SUBLANES + rows] + vec_ref[0:1, :]
    ext_ref[0:SUBLANES, :] = x[rows - SUBLANES:rows, :]

    ba = vec_ref[1:2, :]
    bx = vec_ref[2:3, :]
    lam = vec_ref[3:4, :]
    sp = jnp.maximum(-lam, 0.0) + jnp.log(1.0 + jnp.exp(-jnp.abs(lam)))
    half_c_sp = (-0.5 * LRU_C) * sp
    for nblk in range(D_BLOCKS):
        cs = slice(nblk * D_BLOCK_DIM, (nblk + 1) * D_BLOCK_DIM)
        xb = xc[:, cs]
        xbb = xb.astype(BF16)
        tr = jnp.tanh(_nn(xbb, wa_ref[nblk]) + ba[:, cs])
        ig = 0.5 * jnp.tanh(_nn(xbb, wx_ref[nblk]) + bx[:, cs]) + 0.5
        log_a = half_c_sp[:, cs] * tr + half_c_sp[:, cs]
        th = jnp.tanh(log_a)
        one_minus_a2 = -2.0 * th / (1.0 - th)
        a_ref[:, cs] = jnp.exp(log_a)
        b_ref[:, cs] = jnp.sqrt(one_minus_a2) * (ig * xb)


def _rglru_scan(o_ref, a_ref, b_ref, h_ref):
    rows, width = a_ref.shape
    rowi = lax.broadcasted_iota(jnp.int32, (SUBLANES, width), 0)

    def group(gi, carry):
        sl = pl.ds(pl.multiple_of(gi * SUBLANES, SUBLANES), SUBLANES)
        a = a_ref[sl, :]
        b = b_ref[sl, :]
        for d in (1, 2, 4):
            keep = rowi >= d
            a_sh = jnp.where(keep, pltpu.roll(a, d, 0), 1.0)
            b_sh = jnp.where(keep, pltpu.roll(b, d, 0), 0.0)
            b = a * b_sh + b
            a = a * a_sh
        h = a * carry + b
        o_ref[sl, :] = h
        return jnp.broadcast_to(h[SUBLANES - 1:SUBLANES, :], (SUBLANES, width))

    h_ref[...] = lax.fori_loop(0, rows // SUBLANES, group, h_ref[...], unroll=4)


def _odd_mixer_kernel(q_ref, k_ref, v_ref, x_ref, cw_ref, vec_ref, wa_ref, wx_ref, oc_ref, od_ref,
                      kmean_ref, kb_ref, vt_ref, bias_ref, ext_ref, a_ref, b_ref, h_ref):
    n = pl.program_id(1)
    _rglru_init(ext_ref, h_ref, n == 0)
    _moba_body(q_ref, k_ref, v_ref, oc_ref, kmean_ref, kb_ref, vt_ref, bias_ref, n,
               functools.partial(_rglru_gates, x_ref, cw_ref, vec_ref, wa_ref, wx_ref, ext_ref, a_ref, b_ref))
    _rglru_scan(od_ref, a_ref, b_ref, h_ref)


def _odd_mixer(z, conv_w, vecs, wa, wx, bsz, seq, col_q, col_k, col_v, col_x):
    assert C_BLOCK == ROWS and seq // C_BLOCK >= 2
    t = z.shape[0]
    nb = seq // C_BLOCK
    npp = C_HEADS // 2
    cw = C_HEADS * C_HEAD_DIM
    dw = D_BLOCKS * D_BLOCK_DIM
    full = lambda shape: pl.BlockSpec(shape, lambda b, n: (0,) * len(shape))
    return pl.pallas_call(
        _odd_mixer_kernel,
        grid=(bsz, nb),
        in_specs=[pl.BlockSpec((C_BLOCK, cw), lambda b, n: (b * nb + n, col_q // cw)),
                  pl.BlockSpec((seq, cw), lambda b, n: (b, col_k // cw)),
                  pl.BlockSpec((seq, cw), lambda b, n: (b, col_v // cw)),
                  pl.BlockSpec((ROWS, dw), lambda b, n: (b * nb + n, col_x // dw)),
                  full((D_CONV, dw)), full((SUBLANES, dw)),
                  full((D_BLOCKS, D_BLOCK_DIM, D_BLOCK_DIM)), full((D_BLOCKS, D_BLOCK_DIM, D_BLOCK_DIM))],
        out_specs=[pl.BlockSpec((C_BLOCK, cw), lambda b, n: (b * nb + n, 0)),
                   pl.BlockSpec((ROWS, dw), lambda b, n: (b * nb + n, 0))],
        out_shape=[jax.ShapeDtypeStruct((t, cw), F32), jax.ShapeDtypeStruct((t, dw), F32)],
        scratch_shapes=[pltpu.VMEM((nb, cw), F32),
                        pltpu.VMEM((nb, C_BLOCK, cw), BF16),
                        pltpu.VMEM((npp, nb, LANES, C_BLOCK), BF16),
                        pltpu.VMEM((2 * npp, nb, C_BLOCK), F32),
                        pltpu.VMEM((SUBLANES + ROWS, dw), F32),
                        pltpu.VMEM((ROWS, dw), F32),
                        pltpu.VMEM((ROWS, dw), F32),
                        pltpu.VMEM((SUBLANES, dw), F32)],
        compiler_params=_params(("parallel", "arbitrary")),
        name="odd_mixer",
    )(z, z, z, z, conv_w, vecs, wa, wx)


def _outproj_kernel(*refs, nbr, final):
    br = refs[:nbr]
    gt = refs[nbr:2 * nbr]
    ws = refs[2 * nbr:3 * nbr]
    x_ref, p_ref, pp_ref, pg_ref, pn_ref = refs[3 * nbr:3 * nbr + 5]
    rest = refs[3 * nbr + 5:]
    if final:
        fn_ref, o_ref = rest
    else:
        (o_ref,) = rest

    acc = x_ref[...]
    for b, g, w in zip(br, gt, ws):
        gv = g[...].astype(F32)
        y = b[...] * (gv * _sigmoid(gv))
        acc = acc + _nn(y.astype(BF16), w[...])
    ms = jnp.mean(acc * acc, axis=-1, keepdims=True)
    hn = (acc * lax.rsqrt(ms + EPS) * pn_ref[...]).astype(BF16)
    gate = _sigmoid(_nn(hn, pg_ref[...]))
    out = acc + _nn(p_ref[0].astype(BF16), pp_ref[...]) * gate
    if final:
        ms2 = jnp.mean(out * out, axis=-1, keepdims=True)
        out = out * lax.rsqrt(ms2 + EPS) * fn_ref[...]
    o_ref[...] = out


def _outproj(branches, z, gate_cols, w_parts, x, p_i, ple_proj, ple_gate, ple_norm, final_norm, tm):
    t, d = x.shape
    p_all, layer = p_i
    nbr = len(branches)
    in_specs, args = [], []
    for b in branches:
        in_specs.append(pl.BlockSpec((tm, b.shape[1]), lambda i: (i, 0)))
        args.append(b)
    for b, gc in zip(branches, gate_cols):
        in_specs.append(pl.BlockSpec((tm, b.shape[1]), lambda i, gc=gc: (i, gc)))
        args.append(z)
    for w in w_parts:
        in_specs.append(pl.BlockSpec(w.shape, lambda i: (0, 0)))
        args.append(w)
    in_specs += [pl.BlockSpec((tm, d), lambda i: (i, 0)),
                 pl.BlockSpec((1, tm, P_DIM), lambda i: (layer, i, 0)),
                 pl.BlockSpec((P_DIM, d), lambda i: (0, 0)),
                 pl.BlockSpec((d, d), lambda i: (0, 0)),
                 pl.BlockSpec((1, d), lambda i: (0, 0))]
    args += [x, p_all, ple_proj, ple_gate, ple_norm.reshape(1, d)]
    final = final_norm is not None
    if final:
        in_specs.append(pl.BlockSpec((1, d), lambda i: (0, 0)))
        args.append(final_norm.reshape(1, d))
    return pl.pallas_call(
        functools.partial(_outproj_kernel, nbr=nbr, final=final),
        grid=(t // tm,),
        in_specs=in_specs,
        out_specs=pl.BlockSpec((tm, d), lambda i: (i, 0)),
        out_shape=jax.ShapeDtypeStruct((t, d), F32),
        compiler_params=_params(("parallel",)),
        name="outproj",
    )(*args)


def _even_layer(x, bsz, seq, norm_g, w_in, w_out, lb, hg_norm, mu, w0, w2, a0, a2, k_k, k_a, r_k, ln_g, ln_b,
                p_i, ple_proj, ple_gate, ple_norm, final_norm):
    aw = A_HEADS * A_HEAD_DIM
    bw = B_HEADS * B_HEAD_DIM
    rw0 = 3 * aw
    lr0 = rw0 + 3 * bw
    g0 = lr0 + 2 * B_RANK
    w_perm = jnp.concatenate([w_in[:, :lr0], w_in[:, g0:], w_in[:, lr0:g0]], axis=1).astype(BF16)
    z = _inproj(x, norm_g, w_perm, ntiles=3)

    zero = jnp.zeros_like(lb)
    hpar = jnp.stack([jnp.log(lb), jnp.log1p(-lb), 1.0 - lb, hg_norm, zero, zero, zero, zero])

    zrow = jnp.zeros((bw,), F32)
    rows = [mu[0:bw], mu[bw:2 * bw], mu[2 * bw:3 * bw], 0.5 * w0, 0.5 * a0, k_k, k_a, r_k, ln_g, ln_b]
    rpar = jnp.stack(rows + [zrow] * (_RW_NPAR - len(rows)))
    muwa = mu[3 * bw:].reshape(1, 2 * B_RANK)
    npair = B_HEADS // 2
    zpad = jnp.zeros((npair, B_RANK, LANES), F32)
    w2p = jnp.concatenate([(0.5 * w2).reshape(B_RANK, npair, LANES).transpose(1, 0, 2), zpad], axis=1).astype(BF16)
    a2p = jnp.concatenate([zpad, (0.5 * a2).reshape(B_RANK, npair, LANES).transpose(1, 0, 2)], axis=1).astype(BF16)
    oa, ob = _even_mixer(z, hpar, rpar, muwa, w2p, a2p, bsz, seq, rw0, lr0 + aw + bw)

    gate0 = lr0
    w_out_b = w_out.astype(BF16)
    return _outproj([oa, ob], z, [gate0 // aw, (gate0 + aw) // bw], [w_out_b[:aw], w_out_b[aw:]],
                    x, p_i, ple_proj.astype(BF16), ple_gate.astype(BF16), ple_norm, final_norm, tm=OUTPROJ_ROWS)


def _odd_layer(x, bsz, seq, norm_g, w_in, w_out, conv_w, conv_b, wa, ba, wx, bx, lam,
               p_i, ple_proj, ple_gate, ple_norm, final_norm):
    cw = C_HEADS * C_HEAD_DIM
    dw = D_BLOCKS * D_BLOCK_DIM
    x0 = 3 * cw
    g0 = x0 + dw
    w_perm = jnp.concatenate([w_in[:, :x0], w_in[:, g0:g0 + cw], w_in[:, x0:g0], w_in[:, g0 + cw:]],
                             axis=1).astype(BF16)
    z = _inproj(x, norm_g, w_perm, ntiles=2)

    zero = jnp.zeros_like(lam)
    vecs = jnp.stack([conv_b, 0.5 * ba, 0.5 * bx, lam, zero, zero, zero, zero])
    oc, od = _odd_mixer(z, conv_w, vecs, (0.5 * wa).astype(BF16), (0.5 * wx).astype(BF16), bsz, seq,
                        0, cw, 2 * cw, x0 + cw)

    w_out_b = w_out.astype(BF16)
    return _outproj([oc, od], z, [x0 // cw, (x0 + cw + dw) // dw], [w_out_b[:cw], w_out_b[cw:]],
                    x, p_i, ple_proj.astype(BF16), ple_gate.astype(BF16), ple_norm, final_norm, tm=OUTPROJ_ROWS)


def kernel(x, p, ev_norm, ev_w_in, ev_w_out, hg_lb_logits, hg_norm, rw_mu, rw_w0, rw_w2, rw_a0, rw_a2, rw_k_k, rw_k_a, rw_r_k, rw_ln_g, rw_ln_b, od_norm, od_w_in, od_w_out, lru_conv_w, lru_conv_b, lru_wa, lru_ba, lru_wx, lru_bx, lru_lambda, ple_proj, ple_gate, ple_norm, final_norm):
    bsz, seq, d = x.shape
    depth = p.shape[0]
    s = jax.nn.softmax(hg_lb_logits.astype(F32), axis=0)
    lower_bounds = jnp.maximum(jnp.cumsum(s, axis=0) - s[0], 0.0)
    xf = x.reshape(bsz * seq, d)
    pf = p.reshape(depth, bsz * seq, p.shape[-1])
    for i in range(depth):
        j = i // 2
        fin = final_norm if i == depth - 1 else None
        if i % 2 == 0:
            xf = _even_layer(xf, bsz, seq, ev_norm[j], ev_w_in[j], ev_w_out[j], lower_bounds[j], hg_norm[j],
                             rw_mu[j], rw_w0[j], rw_w2[j], rw_a0[j], rw_a2[j], rw_k_k[j], rw_k_a[j], rw_r_k[j],
                             rw_ln_g[j], rw_ln_b[j], (pf, i), ple_proj[i], ple_gate[i], ple_norm[i], fin)
        else:
            xf = _odd_layer(xf, bsz, seq, od_norm[j], od_w_in[j], od_w_out[j], lru_conv_w[j], lru_conv_b[j],
                            lru_wa[j], lru_ba[j], lru_wx[j], lru_bx[j], lru_lambda[j],
                            (pf, i), ple_proj[i], ple_gate[i], ple_norm[i], fin)
    return xf.reshape(bsz, seq, d)
```
